```python
import math
import jax, jax.numpy as jnp
from jax import lax
import numpy as np

D_MODEL = 1024
BATCH = 8
SEQ = 4096
DEPTH = 1

GRID_W = 64
CTX_LEN = 256

N_FOURIER_GROUPS = 4
FOURIER_GROUP = 128
D_FOURIER = N_FOURIER_GROUPS * FOURIER_GROUP

D_RNN = D_MODEL
N_RNN_HEADS = 8
RNN_HEAD = D_RNN // N_RNN_HEADS
CONV_W = 4
CONV_LEFT = 2
LRU_C = 8.0
N_DIRS = 2

COL_F = D_FOURIER
COL_X = COL_F + D_RNN
COL_G = COL_X + D_RNN
COL_A = COL_G + D_MODEL
D_IN = COL_A + D_MODEL

N_EXPERTS = 16
CAPACITY_FACTOR = 2
D_EXPERT = 1536

N_MOD = 6
EPS = 1e-6
POS_MAX_PERIOD = 10000.0

kernel_name = 'hybrid_fourier_rglru_ec_moe_dit'


def rmsnorm(x, g):
    xf = x.astype(jnp.float32)
    y = xf * lax.rsqrt(jnp.mean(xf * xf, axis=-1, keepdims=True) + EPS)
    return (y * g.astype(jnp.float32)).astype(x.dtype)


def ada_params(cvec, w_ada, b_ada):
    m = jnp.einsum('nd,de->ne', jax.nn.silu(cvec), w_ada) + b_ada
    return [t[:, None, :] for t in jnp.split(m, N_MOD, axis=-1)]


def modulate(h, shift, scale):
    return h * (1.0 + scale) + shift


def grid_pos_embed(n_tokens, dtype):
    rows = n_tokens // GRID_W
    quarter = D_MODEL // 4
    freqs = jnp.exp(-math.log(POS_MAX_PERIOD) * jnp.arange(quarter, dtype=jnp.float32) / quarter)
    ang_r = jnp.arange(rows, dtype=jnp.float32)[:, None] * freqs
    ang_c = jnp.arange(GRID_W, dtype=jnp.float32)[:, None] * freqs
    emb_r = jnp.concatenate([jnp.sin(ang_r), jnp.cos(ang_r)], axis=-1)
    emb_c = jnp.concatenate([jnp.sin(ang_c), jnp.cos(ang_c)], axis=-1)
    emb = jnp.concatenate([
        jnp.broadcast_to(emb_r[:, None, :], (rows, GRID_W, D_MODEL // 2)),
        jnp.broadcast_to(emb_c[None, :, :], (rows, GRID_W, D_MODEL // 2)),
    ], axis=-1)
    return emb.reshape(rows * GRID_W, D_MODEL).astype(dtype)


def fourier_mix(u):
    B, T, _ = u.shape
    g = u.reshape(B, T, N_FOURIER_GROUPS, FOURIER_GROUP).astype(jnp.float32)
    f = jnp.fft.fft2(g, axes=(1, 3), norm='ortho').real
    return f.reshape(B, T, D_FOURIER).astype(u.dtype)


def dwconv(u, w, b):
    T = u.shape[1]
    up = jnp.pad(u, ((0, 0), (CONV_LEFT, CONV_W - 1 - CONV_LEFT), (0, 0)))
    out = b
    for k in range(CONV_W):
        out = out + w[k] * up[:, k:k + T]
    return out


def lru_coeffs(u, lam, wa, ba, wi, bi):
    B, T, _ = u.shape
    uh = u.reshape(B, T, N_RNN_HEADS, RNN_HEAD)
    r = jax.nn.sigmoid(jnp.einsum('bthi,hij->bthj', uh, wa).reshape(B, T, D_RNN) + ba)
    i = jax.nn.sigmoid(jnp.einsum('bthi,hij->bthj', uh, wi).reshape(B, T, D_RNN) + bi)
    log_a = -LRU_C * r.astype(jnp.float32) * jax.nn.softplus(-lam.astype(jnp.float32))
    a = jnp.exp(log_a)
    b = jnp.sqrt(-jnp.expm1(2.0 * log_a)) * (i * u).astype(jnp.float32)
    return a, b


def _combine(left, right):
    a_l, b_l = left
    a_r, b_r = right
    return a_l * a_r, a_r * b_l + b_r


def linear_scan(a, b, h0):
    b = b.at[:, 0].add(a[:, 0] * h0)
    _, h = lax.associative_scan(_combine, (a, b), axis=1)
    return h


def rglru_bidir(u_lat, u_ctx, lam, wa, ba, wi, bi, ctx_out):
    y_lat = None
    y_ctx = None
    h0 = jnp.zeros((u_ctx.shape[0], D_RNN), jnp.float32)
    for d in range(N_DIRS):
        a_c, b_c = lru_coeffs(u_ctx, lam[d], wa[d], ba[d], wi[d], bi[d])
        a_l, b_l = lru_coeffs(u_lat, lam[d], wa[d], ba[d], wi[d], bi[d])
        if d == 1:
            a_c, b_c, a_l, b_l = [jnp.flip(t, axis=1) for t in (a_c, b_c, a_l, b_l)]
        h_c = linear_scan(a_c, b_c, h0)
        h_l = linear_scan(a_l, b_l, h_c[:, -1])
        if d == 1:
            h_c = jnp.flip(h_c, axis=1)
            h_l = jnp.flip(h_l, axis=1)
        y_lat = h_l if y_lat is None else y_lat + h_l
        if ctx_out:
            y_ctx = h_c if y_ctx is None else y_ctx + h_c
    y_lat = y_lat.astype(u_lat.dtype)
    if ctx_out:
        y_ctx = y_ctx.astype(u_ctx.dtype)
    return y_lat, y_ctx


def merge_branches(z, r, w_four, w_lru, w_out):
    y_four = jnp.einsum('btf,fd->btd', fourier_mix(z[..., :COL_F]), w_four)
    y_rnn = jnp.einsum('btr,rd->btd', jax.nn.gelu(z[..., COL_X:COL_G]) * r, w_lru)
    g_four = jax.nn.sigmoid(z[..., COL_G:COL_A])
    g_rnn = jax.nn.sigmoid(z[..., COL_A:])
    return jnp.einsum('btd,de->bte', g_four * y_four + g_rnn * y_rnn, w_out)


def token_mixer(h_lat, h_ctx, w_in, w_four, conv_w, conv_b, lam, wa, ba, wi, bi, w_lru, w_out, ctx_out):
    z_lat = jnp.einsum('btd,de->bte', h_lat, w_in)
    if ctx_out:
        z_ctx = jnp.einsum('btd,de->bte', h_ctx, w_in)
        x_ctx = z_ctx[..., COL_F:COL_X]
    else:
        z_ctx = None
        x_ctx = jnp.einsum('btd,de->bte', h_ctx, w_in[:, COL_F:COL_X])
    u_lat = dwconv(z_lat[..., COL_F:COL_X], conv_w, conv_b)
    u_ctx = dwconv(x_ctx, conv_w, conv_b)
    r_lat, r_ctx = rglru_bidir(u_lat, u_ctx, lam, wa, ba, wi, bi, ctx_out)
    y_lat = merge_branches(z_lat, r_lat, w_four, w_lru, w_out)
    y_ctx = merge_branches(z_ctx, r_ctx, w_four, w_lru, w_out) if ctx_out else None
    return y_lat, y_ctx


def expert_choice_ffn(h, w_router, w_gate, w_up, w_down):
    B, T, D = h.shape
    cap = CAPACITY_FACTOR * T // N_EXPERTS
    scores = jax.nn.softmax(jnp.einsum('btd,de->bte', h, w_router).astype(jnp.float32), axis=-1)
    g, idx = lax.top_k(jnp.swapaxes(scores, 1, 2), cap)
    xg = jax.vmap(lambda hb, ib: hb[ib])(h, idx)
    hid = jax.nn.silu(jnp.einsum('becd,edf->becf', xg, w_gate)) * jnp.einsum('becd,edf->becf', xg, w_up)
    out = jnp.einsum('becf,efd->becd', hid, w_down) * g[..., None].astype(h.dtype)
    return jax.vmap(lambda ib, ob: jnp.zeros((T, D), h.dtype).at[ib.reshape(-1)].add(ob.reshape(-1, D)))(idx, out)


def setup_inputs(seed: int = 0) -> dict:
    key = jax.random.key(seed)
    ks = jax.random.split(key, 24)
    f32 = jnp.float32

    def nrm(k, shape, fan_in):
        return jax.random.normal(k, shape, f32) * (fan_in ** -0.5)

    u = jax.random.uniform(ks[10], (DEPTH, N_DIRS, D_RNN), f32, 0.9, 0.999)
    s = u ** (1.0 / LRU_C)
    lru_lambda = jnp.log(s) - jnp.log1p(-s)
    return {
        'x': jax.random.normal(ks[0], (BATCH, SEQ, D_MODEL), f32),
        'c': jax.random.normal(ks[1], (BATCH, D_MODEL), f32),
        'ctx': jax.random.normal(ks[2], (BATCH, CTX_LEN, D_MODEL), f32),
        'c_ctx': jax.random.normal(ks[3], (D_MODEL,), f32),
        'w_ada': nrm(ks[4], (DEPTH, D_MODEL, N_MOD * D_MODEL), D_MODEL),
        'b_ada': 0.01 * jax.random.normal(ks[5], (DEPTH, N_MOD * D_MODEL), f32),
        'norm1_g': 1.0 + 0.02 * jax.random.normal(ks[6], (DEPTH, D_MODEL), f32),
        'norm2_g': 1.0 + 0.02 * jax.random.normal(ks[7], (DEPTH, D_MODEL), f32),
        'w_in': nrm(ks[8], (DEPTH, D_MODEL, D_IN), D_MODEL),
        'w_four': nrm(ks[9], (DEPTH, D_FOURIER, D_MODEL), D_FOURIER),
        'conv_w': nrm(ks[11], (DEPTH, CONV_W, D_RNN), CONV_W),
        'conv_b': 0.01 * jax.random.normal(ks[12], (DEPTH, D_RNN), f32),
        'lru_lambda': lru_lambda,
        'lru_wa': nrm(ks[13], (DEPTH, N_DIRS, N_RNN_HEADS, RNN_HEAD, RNN_HEAD), RNN_HEAD),
        'lru_ba': 0.01 * jax.random.normal(ks[14], (DEPTH, N_DIRS, D_RNN), f32),
        'lru_wi': nrm(ks[15], (DEPTH, N_DIRS, N_RNN_HEADS, RNN_HEAD, RNN_HEAD), RNN_HEAD),
        'lru_bi': 0.01 * jax.random.normal(ks[16], (DEPTH, N_DIRS, D_RNN), f32),
        'w_lru': nrm(ks[17], (DEPTH, D_RNN, D_MODEL), D_RNN),
        'w_out': nrm(ks[18], (DEPTH, D_MODEL, D_MODEL), D_MODEL),
        'w_router': nrm(ks[19], (DEPTH, D_MODEL, N_EXPERTS), D_MODEL),
        'w_gate_e': nrm(ks[20], (DEPTH, N_EXPERTS, D_MODEL, D_EXPERT), D_MODEL),
        'w_up_e': nrm(ks[21], (DEPTH, N_EXPERTS, D_MODEL, D_EXPERT), D_MODEL),
        'w_down_e': nrm(ks[22], (DEPTH, N_EXPERTS, D_EXPERT, D_MODEL), D_EXPERT),
        'final_g': 1.0 + 0.02 * jax.random.normal(ks[23], (D_MODEL,), f32),
    }


def reference(x, c, ctx, c_ctx, w_ada, b_ada, norm1_g, norm2_g, w_in, w_four, conv_w, conv_b,
              lru_lambda, lru_wa, lru_ba, lru_wi, lru_bi, w_lru, w_out, w_router,
              w_gate_e, w_up_e, w_down_e, final_g):
    x = x + grid_pos_embed(x.shape[1], x.dtype)[None]
    for l in range(DEPTH):
        ctx_out = l < DEPTH - 1
        sh1, sc1, g1, sh2, sc2, g2 = ada_params(c, w_ada[l], b_ada[l])
        csh1, csc1, cg1, csh2, csc2, cg2 = ada_params(c_ctx[None], w_ada[l], b_ada[l])

        h_lat = modulate(rmsnorm(x, norm1_g[l]), sh1, sc1)
        h_ctx = modulate(rmsnorm(ctx, norm1_g[l]), csh1, csc1)
        y_lat, y_ctx = token_mixer(h_lat, h_ctx, w_in[l], w_four[l], conv_w[l], conv_b[l],
                                   lru_lambda[l], lru_wa[l], lru_ba[l], lru_wi[l], lru_bi[l],
                                   w_lru[l], w_out[l], ctx_out)
        x = x + g1 * y_lat
        if ctx_out:
            ctx = ctx + cg1 * y_ctx

        h_lat = modulate(rmsnorm(x, norm2_g[l]), sh2, sc2)
        x = x + g2 * expert_choice_ffn(h_lat, w_router[l], w_gate_e[l], w_up_e[l], w_down_e[l])
        if ctx_out:
            h_ctx = modulate(rmsnorm(ctx, norm2_g[l]), csh2, csc2)
            ctx = ctx + cg2 * expert_choice_ffn(h_ctx, w_router[l], w_gate_e[l], w_up_e[l], w_down_e[l])
    return rmsnorm(x, final_g)
```

```python
import functools
import math

import numpy as np
import jax
import jax.numpy as jnp
from jax import lax
from jax.experimental import pallas as pl
from jax.experimental.pallas import tpu as pltpu

F32 = jnp.float32
BF16 = jnp.bfloat16

D_MODEL = 1024
GRID_W = 64
N_FOURIER_GROUPS = 4
FOURIER_GROUP = 128
D_FOURIER = N_FOURIER_GROUPS * FOURIER_GROUP
D_RNN = D_MODEL
N_RNN_HEADS = 8
RNN_HEAD = D_RNN // N_RNN_HEADS
CONV_W = 4
CONV_LEFT = 2
LRU_C = 8.0
N_EXPERTS = 16
CAPACITY_FACTOR = 2
D_EXPERT = 1536
N_MOD = 6
EPS = 1e-6
POS_MAX_PERIOD = 10000.0

LANES = 128
SUBLANES = 8
BF16_ROWS = 16
VMEM_LIMIT = 56 * 1024 * 1024

TOKEN_BLOCK = 512
SCAN_CHUNK = 128
SCAN_STRIDE = SCAN_CHUNK + SUBLANES


def _params(*sem):
    return pltpu.CompilerParams(dimension_semantics=sem, vmem_limit_bytes=VMEM_LIMIT)


def _sigmoid(x):
    return 0.5 * jnp.tanh(0.5 * x) + 0.5


def _rmsnorm(x, g):
    return x * lax.rsqrt(jnp.mean(x * x, axis=-1, keepdims=True) + EPS) * g


def _dot(a, b):
    return jnp.dot(a, b, preferred_element_type=F32)


def _pos_kernel(o_ref):
    rows = o_ref.shape[0]
    q = D_MODEL // 4
    k = lax.broadcasted_iota(jnp.int32, (1, q), 1).astype(F32)
    freqs = jnp.exp(-math.log(POS_MAX_PERIOD) * k / q)
    ang_r = lax.broadcasted_iota(jnp.int32, (rows, q), 0).astype(F32) * freqs
    ang_c = lax.broadcasted_iota(jnp.int32, (GRID_W, q), 0).astype(F32) * freqs
    shape = (rows, GRID_W, q)
    o_ref[:, :, 0 * q:1 * q] = jnp.broadcast_to(jnp.sin(ang_r)[:, None, :], shape)
    o_ref[:, :, 1 * q:2 * q] = jnp.broadcast_to(jnp.cos(ang_r)[:, None, :], shape)
    o_ref[:, :, 2 * q:3 * q] = jnp.broadcast_to(jnp.sin(ang_c)[None, :, :], shape)
    o_ref[:, :, 3 * q:4 * q] = jnp.broadcast_to(jnp.cos(ang_c)[None, :, :], shape)


def _pos_table(n_tokens):
    rows = n_tokens // GRID_W
    out = pl.pallas_call(
        _pos_kernel,
        out_shape=jax.ShapeDtypeStruct((rows, GRID_W, D_MODEL), F32),
        name="pos_table",
    )()
    return out.reshape(n_tokens, D_MODEL)


def _dft_kernel(c_ref, s_ref, *, n):
    tm = c_ref.shape[0]
    w = 2.0 * math.pi / n
    scale = 1.0 / math.sqrt(n)
    k = lax.broadcasted_iota(jnp.int32, (tm, LANES), 0) + pl.program_id(0) * tm
    lane = lax.broadcasted_iota(jnp.int32, (tm, LANES), 1)
    beta = ((k * lane) & (n - 1)).astype(F32) * w
    cb = jnp.cos(beta) * scale
    sb = jnp.sin(beta) * scale
    alpha = ((k * lane * LANES) & (n - 1)).astype(F32) * w
    ca = jnp.cos(alpha)
    sa = jnp.sin(alpha)
    for j in range(n // LANES):
        caj = jnp.broadcast_to(ca[:, j:j + 1], (tm, LANES))
        saj = jnp.broadcast_to(sa[:, j:j + 1], (tm, LANES))
        c_ref[:, j * LANES:(j + 1) * LANES] = (caj * cb - saj * sb).astype(BF16)
        s_ref[:, j * LANES:(j + 1) * LANES] = (-(saj * cb + caj * sb)).astype(BF16)


def _dft_tables(n):
    assert n & (n - 1) == 0 and n % LANES == 0
    tm = min(TOKEN_BLOCK, n)
    return pl.pallas_call(
        functools.partial(_dft_kernel, n=n),
        grid=(n // tm,),
        out_specs=[pl.BlockSpec((tm, n), lambda i: (i, 0))] * 2,
        out_shape=[jax.ShapeDtypeStruct((n, n), BF16)] * 2,
        compiler_params=_params("parallel"),
        name="dft_tables",
    )()


def _fold_kernel(w_ref, c_ref, s_ref, oc_ref, os_ref):
    for g in range(N_FOURIER_GROUPS):
        sl = slice(g * FOURIER_GROUP, (g + 1) * FOURIER_GROUP)
        w = w_ref[:, sl]
        oc_ref[:, sl] = jnp.dot(w, c_ref[...], preferred_element_type=F32,
                                precision=lax.Precision.HIGHEST).astype(BF16)
        os_ref[:, sl] = jnp.dot(w, s_ref[...], preferred_element_type=F32,
                                precision=lax.Precision.HIGHEST).astype(BF16)


def _fold_channel_dft(w_f):
    p = np.arange(FOURIER_GROUP)
    ang = 2.0 * np.pi * ((p[:, None] * p[None, :]) % FOURIER_GROUP) / FOURIER_GROUP
    cp = jnp.asarray(np.cos(ang) / math.sqrt(FOURIER_GROUP), F32)
    sp = jnp.asarray(np.sin(ang) / math.sqrt(FOURIER_GROUP), F32)
    return pl.pallas_call(
        _fold_kernel,
        out_shape=[jax.ShapeDtypeStruct(w_f.shape, BF16)] * 2,
        compiler_params=_params(),
        name="fold_channel_dft",
    )(w_f, cp, sp)


def _ada_kernel(c_ref, w_ref, b_ref, o_ref):
    c = c_ref[...]
    o_ref[...] = jnp.dot(c * _sigmoid(c), w_ref[...], preferred_element_type=F32,
                         precision=lax.Precision.HIGHEST) + b_ref[...]


def _ada(cc, w_ada, b_ada):
    n = w_ada.shape[1]
    tn = D_MODEL
    return pl.pallas_call(
        _ada_kernel,
        grid=(n // tn,),
        in_specs=[pl.BlockSpec(cc.shape, lambda j: (0, 0)),
                  pl.BlockSpec((D_MODEL, tn), lambda j: (0, j)),
                  pl.BlockSpec((1, tn), lambda j: (0, j))],
        out_specs=pl.BlockSpec((cc.shape[0], tn), lambda j: (0, j)),
        out_shape=jax.ShapeDtypeStruct((cc.shape[0], n), F32),
        compiler_params=_params("parallel"),
        name="ada",
    )(cc, w_ada, b_ada.reshape(1, n))


def _proj_kernel(*refs, widths, has_pos):
    if has_pos:
        x_ref, pos_ref, mod_ref, g_ref, w_ref = refs[:5]
        outs = refs[5:]
        x = x_ref[...] + pos_ref[...]
    else:
        x_ref, mod_ref, g_ref, w_ref = refs[:4]
        outs = refs[4:]
        x = x_ref[...]
    h = _rmsnorm(x, g_ref[...]) * (1.0 + mod_ref[1:2, :]) + mod_ref[0:1, :]
    hb = h.astype(BF16)
    c0 = 0
    for o_ref, wd in zip(outs, widths):
        o_ref[...] = _dot(hb, w_ref[:, c0:c0 + wd]).astype(BF16)
        c0 += wd


def _proj(x, pos, mods, mod_row, g, w, widths):
    b, t, d = x.shape
    tm = min(TOKEN_BLOCK, t)
    has_pos = pos is not None
    in_specs = [pl.BlockSpec((None, tm, d), lambda i, bb: (bb, i, 0))]
    args = [x]
    if has_pos:
        in_specs.append(pl.BlockSpec((tm, d), lambda i, bb: (i, 0)))
        args.append(pos)
    if mod_row is None:
        mod_map = lambda i, bb: (bb, 0, 0)
    else:
        mod_map = lambda i, bb: (mod_row, 0, 0)
    in_specs += [pl.BlockSpec((None, SUBLANES, d), mod_map),
                 pl.BlockSpec((1, d), lambda i, bb: (0, 0)),
                 pl.BlockSpec(w.shape, lambda i, bb: (0, 0))]
    args += [mods, g, w]
    return pl.pallas_call(
        functools.partial(_proj_kernel, widths=widths, has_pos=has_pos),
        grid=(t // tm, b),
        in_specs=in_specs,
        out_specs=[pl.BlockSpec((None, tm, wd), lambda i, bb: (bb, i, 0)) for wd in widths],
        out_shape=[jax.ShapeDtypeStruct((b, t, wd), BF16) for wd in widths],
        compiler_params=_params("parallel", "parallel"),
        name="proj",
    )(*args)


def _gelu(x):
    return 0.5 * x * (1.0 + jnp.tanh(math.sqrt(2.0 / math.pi) * (x + 0.044715 * x * x * x)))


def _scan_kernel(*refs, reverse, fuse, n_chunks):
    if fuse:
        (zx_ref, zp_ref, zn_ref, vec_ref, wg_ref, h0_ref, hf_ref, zg_ref,
         o_ref, hT_ref, ext_s, a_s, b_s, hc_s) = refs
    else:
        (zx_ref, zp_ref, zn_ref, vec_ref, wg_ref, h0_ref,
         o_ref, hT_ref, ext_s, a_s, b_s, hc_s) = refs
    nb, tc, c = zx_ref.shape
    step = pl.program_id(0)
    ci = (n_chunks - 1 - step) if reverse else step

    @pl.when(step == 0)
    def _():
        for hh in range(N_RNN_HEADS):
            hc_s[hh] = h0_ref[:, hh * RNN_HEAD:(hh + 1) * RNN_HEAD]

    ext_s[:, BF16_ROWS:BF16_ROWS + tc, :] = zx_ref[...].astype(F32)
    ext_s[:, 0:BF16_ROWS, :] = jnp.where(ci > 0, zp_ref[...].astype(F32), 0.0)
    ext_s[:, BF16_ROWS + tc:, :] = jnp.where(ci < n_chunks - 1, zn_ref[...].astype(F32), 0.0)

    lam = vec_ref[0:1, :]
    b_a = vec_ref[1:2, :]
    b_i = vec_ref[2:3, :]
    conv_b = vec_ref[3:4, :]
    neg_c_sp = -LRU_C * (jnp.maximum(-lam, 0.0) + jnp.log1p(jnp.exp(-jnp.abs(lam))))

    for bb in range(nb):
        u = conv_b
        for kk in range(CONV_W):
            r0 = BF16_ROWS - CONV_LEFT + kk
            u = u + vec_ref[4 + kk:5 + kk, :] * ext_s[bb, r0:r0 + tc, :]
        ub = u.astype(BF16)
        row0 = bb * SCAN_STRIDE
        for hh in range(N_RNN_HEADS):
            sl = slice(hh * RNN_HEAD, (hh + 1) * RNN_HEAD)
            gates = _dot(ub[:, sl], wg_ref[hh])
            r = _sigmoid(gates[:, :RNN_HEAD] + b_a[:, sl])
            ig = _sigmoid(gates[:, RNN_HEAD:] + b_i[:, sl])
            a = jnp.exp(r * neg_c_sp[:, sl])
            a_s[hh, row0:row0 + tc, :] = a
            b_s[hh, row0:row0 + tc, :] = jnp.sqrt(1.0 - a * a) * (ig * u[:, sl])

    def body(j, hs):
        t = (tc - 1 - j) if reverse else j
        idx = pl.ds(t, nb, stride=SCAN_STRIDE)
        new = []
        for hh in range(N_RNN_HEADS):
            h = a_s[hh, idx, :] * hs[hh] + b_s[hh, idx, :]
            b_s[hh, idx, :] = h
            new.append(h)
        return tuple(new)

    hs = lax.fori_loop(0, tc, body, tuple(hc_s[hh] for hh in range(N_RNN_HEADS)), unroll=8)
    for hh in range(N_RNN_HEADS):
        hc_s[hh] = hs[hh]
        hT_ref[:, hh * RNN_HEAD:(hh + 1) * RNN_HEAD] = hs[hh]

    for bb in range(nb):
        row0 = bb * SCAN_STRIDE
        for hh in range(N_RNN_HEADS):
            sl = slice(hh * RNN_HEAD, (hh + 1) * RNN_HEAD)
            hb = b_s[hh, row0:row0 + tc, :]
            if fuse:
                hb = _gelu(zg_ref[bb, :, sl].astype(F32)) * (hf_ref[bb, :, sl].astype(F32) + hb)
            o_ref[bb, :, sl] = hb.astype(BF16)


def _scan(zx, vecs, wg, h0, *, reverse, hf=None, zg=None):
    nb, t, c = zx.shape
    tc = SCAN_CHUNK
    n_chunks = t // tc
    fuse = hf is not None
    hpc = tc // BF16_ROWS
    n_halo = t // BF16_ROWS

    def cidx(i):
        return (n_chunks - 1 - i) if reverse else i

    main_spec = pl.BlockSpec((nb, tc, c), lambda i: (0, cidx(i), 0))
    in_specs = [
        main_spec,
        pl.BlockSpec((nb, BF16_ROWS, c), lambda i: (0, jnp.maximum(cidx(i) * hpc - 1, 0), 0)),
        pl.BlockSpec((nb, BF16_ROWS, c), lambda i: (0, jnp.minimum((cidx(i) + 1) * hpc, n_halo - 1), 0)),
        pl.BlockSpec(vecs.shape, lambda i: (0, 0)),
        pl.BlockSpec(wg.shape, lambda i: (0, 0, 0)),
        pl.BlockSpec(h0.shape, lambda i: (0, 0)),
    ]
    args = [zx, zx, zx, vecs, wg, h0]
    if fuse:
        in_specs += [main_spec, main_spec]
        args += [hf, zg]
    return pl.pallas_call(
        functools.partial(_scan_kernel, reverse=reverse, fuse=fuse, n_chunks=n_chunks),
        grid=(n_chunks,),
        in_specs=in_specs,
        out_specs=[main_spec, pl.BlockSpec((nb, c), lambda i: (0, 0))],
        out_shape=[jax.ShapeDtypeStruct((nb, t, c), BF16), jax.ShapeDtypeStruct((nb, c), F32)],
        scratch_shapes=[pltpu.VMEM((nb, tc + 2 * BF16_ROWS, c), F32),
                        pltpu.VMEM((N_RNN_HEADS, nb * SCAN_STRIDE, RNN_HEAD), F32),
                        pltpu.VMEM((N_RNN_HEADS, nb * SCAN_STRIDE, RNN_HEAD), F32),
                        pltpu.VMEM((N_RNN_HEADS, nb, RNN_HEAD), F32)],
        compiler_params=_params("arbitrary"),
        name="scan_bwd" if reverse else "scan_fwd",
    )(*args)


def _dftmm_kernel(c_ref, s_ref, zc_ref, zs_ref, o_ref):
    o_ref[...] = (_dot(c_ref[...], zc_ref[...]) + _dot(s_ref[...], zs_ref[...])).astype(BF16)


def _dft_apply(ct, sn, zfc, zfs):
    b, t, f = zfc.shape
    tm = min(TOKEN_BLOCK, t)
    return pl.pallas_call(
        _dftmm_kernel,
        grid=(t // tm, b),
        in_specs=[pl.BlockSpec((tm, t), lambda i, bb: (i, 0)),
                  pl.BlockSpec((tm, t), lambda i, bb: (i, 0)),
                  pl.BlockSpec((None, t, f), lambda i, bb: (bb, 0, 0)),
                  pl.BlockSpec((None, t, f), lambda i, bb: (bb, 0, 0))],
        out_specs=pl.BlockSpec((None, tm, f), lambda i, bb: (bb, i, 0)),
        out_shape=jax.ShapeDtypeStruct((b, t, f), BF16),
        compiler_params=_params("parallel", "parallel"),
        name="dft_apply",
    )(ct, sn, zfc, zfs)


def _merge_kernel(f_ref, gr_ref, za_ref, zb_ref, x_ref, pos_ref, mod_ref, g2_ref,
                  wf_ref, wl_ref, wo_ref, wr_ref, x1_ref, h2_ref, sc_ref):
    y_four = _dot(f_ref[...], wf_ref[...])
    y_rnn = _dot(gr_ref[...], wl_ref[...])
    m = _sigmoid(za_ref[...].astype(F32)) * y_four + _sigmoid(zb_ref[...].astype(F32)) * y_rnn
    y = _dot(m.astype(BF16), wo_ref[...])
    x1 = (x_ref[...] + pos_ref[...]) + mod_ref[2:3, :] * y
    x1_ref[...] = x1
    h2 = (_rmsnorm(x1, g2_ref[...]) * (1.0 + mod_ref[4:5, :]) + mod_ref[3:4, :]).astype(BF16)
    h2_ref[...] = h2
    logits = lax.dot_general(wr_ref[...], h2, (((1,), (1,)), ((), ())), preferred_element_type=F32)
    e = jnp.exp(logits - jnp.max(logits, axis=0, keepdims=True))
    sc_ref[...] = e / jnp.sum(e, axis=0, keepdims=True)


def _merge(f, gr, za, zb, x, pos, mods, g2, w_four, w_lru, w_out, w_rt):
    b, t, d = x.shape
    tm = min(TOKEN_BLOCK, t)
    tok = lambda wd: pl.BlockSpec((None, tm, wd), lambda i, bb: (bb, i, 0))
    full = lambda a: pl.BlockSpec(a.shape, lambda i, bb: (0,) * a.ndim)
    return pl.pallas_call(
        _merge_kernel,
        grid=(t // tm, b),
        in_specs=[tok(D_FOURIER), tok(d), tok(d), tok(d), tok(d),
                  pl.BlockSpec((tm, d), lambda i, bb: (i, 0)),
                  pl.BlockSpec((None, SUBLANES, d), lambda i, bb: (bb, 0, 0)),
                  full(g2), full(w_four), full(w_lru), full(w_out), full(w_rt)],
        out_specs=[tok(d), tok(d), pl.BlockSpec((None, N_EXPERTS, tm), lambda i, bb: (bb, 0, i))],
        out_shape=[jax.ShapeDtypeStruct((b, t, d), F32),
                   jax.ShapeDtypeStruct((b, t, d), BF16),
                   jax.ShapeDtypeStruct((b, N_EXPERTS, t), F32)],
        compiler_params=_params("parallel", "parallel"),
        name="merge",
    )(f, gr, za, zb, x, pos, mods, g2, w_four, w_lru, w_out, w_rt)


MAX_EXP = 127
EXP_STEPS = 7
VAL_STEPS = 56


def _select_kernel(s_ref, slot_ref, *, cap):
    rows, t = s_ref.shape

    def count(mask):
        return jnp.sum(jnp.where(mask, 1.0, 0.0), axis=1, keepdims=True)

    def exp_step(_, carry):
        k_lo, k_hi = carry
        k_mid = jnp.floor(0.5 * (k_lo + k_hi))
        ok = count(s_ref[...] >= jnp.exp2(-k_mid)) >= cap
        return jnp.where(ok, k_lo, k_mid), jnp.where(ok, k_mid, k_hi)

    k_lo, k_hi = lax.fori_loop(
        0, EXP_STEPS, exp_step,
        (jnp.full((rows, 1), -1.0, F32), jnp.full((rows, 1), float(MAX_EXP), F32)))
    lo0 = jnp.where(k_hi >= MAX_EXP, 0.0, jnp.exp2(-k_hi))
    hi0 = jnp.exp2(-k_lo)

    def val_step(_, carry):
        lo, hi = carry
        mid = 0.5 * (lo + hi)
        ok = count(s_ref[...] >= mid) >= cap
        return jnp.where(ok, mid, lo), jnp.where(ok, hi, mid)

    lo, hi = lax.fori_loop(0, VAL_STEPS, val_step, (lo0, hi0))
    s = s_ref[...]
    gt = s >= hi
    eq = (s >= lo) & (s < hi)
    need = cap - count(gt)

    p = lax.broadcasted_iota(jnp.int32, (LANES, LANES), 0)
    q = lax.broadcasted_iota(jnp.int32, (LANES, LANES), 1)
    tri = (p < q).astype(BF16)
    run_gt = jnp.zeros((rows, 1), F32)
    run_eq = jnp.zeros((rows, 1), F32)
    for j in range(t // LANES):
        sl = slice(j * LANES, (j + 1) * LANES)
        g = gt[:, sl]
        q_ = eq[:, sl]
        pg = _dot(g.astype(BF16), tri) + run_gt
        pe = _dot(q_.astype(BF16), tri) + run_eq
        sel = g | (q_ & (pe < need))
        pos = pg + jnp.minimum(pe, need)
        slot_ref[:, sl] = jnp.where(sel, pos, -1.0).astype(jnp.int32)
        run_gt = run_gt + count(g)
        run_eq = run_eq + count(q_)


def _select(scores, cap):
    rows, t = scores.shape
    return pl.pallas_call(
        functools.partial(_select_kernel, cap=cap),
        out_shape=jax.ShapeDtypeStruct((rows, t), jnp.int32),
        compiler_params=_params(),
        name="select",
    )(scores)


def _expert_kernel(slot_ref, sc_ref, h_ref, wg_ref, wu_ref, wd_ref, o_ref, *, cap):
    t = h_ref.shape[0]
    tk = min(TOKEN_BLOCK, t)
    slot_iota = lax.broadcasted_iota(jnp.int32, (cap, tk), 0)
    xg = jnp.zeros((cap, D_MODEL), F32)
    gs = jnp.zeros((cap, 1), F32)
    for j in range(t // tk):
        sl = slice(j * tk, (j + 1) * tk)
        onehot = slot_iota == slot_ref[:, sl]
        xg = xg + _dot(onehot.astype(BF16), h_ref[sl, :])
        gs = gs + jnp.sum(jnp.where(onehot, sc_ref[:, sl], 0.0), axis=1, keepdims=True)
    xb = xg.astype(BF16)
    gate = _dot(xb, wg_ref[...])
    hid = (gate * _sigmoid(gate)) * _dot(xb, wu_ref[...])
    y = _dot(hid.astype(BF16), wd_ref[...])
    o_ref[...] = (y * gs).astype(BF16)


def _experts(slot, scores, h2, wg, wu, wd, cap):
    b, t, d = h2.shape
    row = pl.BlockSpec((None, 1, t), lambda bb, e: (bb * N_EXPERTS + e, 0, 0))
    return pl.pallas_call(
        functools.partial(_expert_kernel, cap=cap),
        grid=(b, N_EXPERTS),
        in_specs=[row, row,
                  pl.BlockSpec((None, t, d), lambda bb, e: (bb, 0, 0)),
                  pl.BlockSpec((None, d, D_EXPERT), lambda bb, e: (e, 0, 0)),
                  pl.BlockSpec((None, d, D_EXPERT), lambda bb, e: (e, 0, 0)),
                  pl.BlockSpec((None, D_EXPERT, d), lambda bb, e: (e, 0, 0))],
        out_specs=pl.BlockSpec((None, cap, d), lambda bb, e: (bb * N_EXPERTS + e, 0, 0)),
        out_shape=jax.ShapeDtypeStruct((b * N_EXPERTS, cap, d), BF16),
        compiler_params=_params("parallel", "parallel"),
        name="experts",
    )(slot.reshape(b * N_EXPERTS, 1, t), scores.reshape(b * N_EXPERTS, 1, t), h2, wg, wu, wd)


def _combine_kernel(slot_ref, y_ref, x1_ref, mod_ref, g_ref, o_ref, *, cap):
    tm = x1_ref.shape[0]
    lane_iota = lax.broadcasted_iota(jnp.int32, (tm, cap), 1)
    acc = jnp.zeros((tm, D_MODEL), F32)
    for e in range(N_EXPERTS):
        onehot = lane_iota == slot_ref[:, e:e + 1]
        acc = acc + _dot(onehot.astype(BF16), y_ref[e * cap:(e + 1) * cap, :])
    x2 = x1_ref[...] + mod_ref[5:6, :] * acc
    o_ref[...] = _rmsnorm(x2, g_ref[...])


def _combine(slot_t, y, x1, mods, g, cap):
    b, t, d = x1.shape
    tm = min(TOKEN_BLOCK, t)
    return pl.pallas_call(
        functools.partial(_combine_kernel, cap=cap),
        grid=(b, t // tm),
        in_specs=[pl.BlockSpec((None, tm, N_EXPERTS), lambda bb, i: (bb, i, 0)),
                  pl.BlockSpec((None, N_EXPERTS * cap, d), lambda bb, i: (bb, 0, 0)),
                  pl.BlockSpec((None, tm, d), lambda bb, i: (bb, i, 0)),
                  pl.BlockSpec((None, SUBLANES, d), lambda bb, i: (bb, 0, 0)),
                  pl.BlockSpec((1, d), lambda bb, i: (0, 0))],
        out_specs=pl.BlockSpec((None, tm, d), lambda bb, i: (bb, i, 0)),
        out_shape=jax.ShapeDtypeStruct((b, t, d), F32),
        compiler_params=_params("parallel", "parallel"),
        name="combine",
    )(slot_t, y, x1, mods, g)


def kernel(x, c, ctx, c_ctx, w_ada, b_ada, norm1_g, norm2_g, w_in, w_four, conv_w, conv_b, lru_lambda, lru_wa, lru_ba, lru_wi, lru_bi, w_lru, w_out, w_router, w_gate_e, w_up_e, w_down_e, final_g):
    b, t, d = x.shape
    assert d == D_MODEL and b == SUBLANES and t % TOKEN_BLOCK == 0 and ctx.shape[1] % SCAN_CHUNK == 0
    assert w_ada.shape[0] == 1, "single-layer problem: the context stream is only read by the recurrence"
    cap = CAPACITY_FACTOR * t // N_EXPERTS
    col_x = D_FOURIER + D_RNN
    col_g = col_x + D_RNN
    col_a = col_g + D_MODEL

    pos = _pos_table(t)
    ct, sn = _dft_tables(t)

    cc = jnp.concatenate([c, c_ctx[None], jnp.zeros((SUBLANES - 1, d), F32)], axis=0)
    mods = _ada(cc, w_ada[0], b_ada[0]).reshape(cc.shape[0], N_MOD, d)
    mods = jnp.pad(mods, ((0, 0), (0, SUBLANES - N_MOD), (0, 0)))

    w = w_in[0]
    wfc, wfs = _fold_channel_dft(w[:, :D_FOURIER])
    w_x = w[:, D_FOURIER:col_x].astype(BF16)
    w_cat = jnp.concatenate([w_x, w[:, col_x:].astype(BF16), wfc, wfs], axis=1)
    g1 = norm1_g[0].reshape(1, d)
    zx, zg, za, zb, zfc, zfs = _proj(x, pos, mods, None, g1, w_cat,
                                     (D_RNN, D_RNN, D_MODEL, D_MODEL, D_FOURIER, D_FOURIER))
    (zx_ctx,) = _proj(ctx, None, mods, b, g1, w_x, (D_RNN,))

    for dr in range(2):
        vecs = jnp.concatenate([lru_lambda[0, dr][None], lru_ba[0, dr][None], lru_bi[0, dr][None],
                                conv_b[0][None], conv_w[0]], axis=0)
        wg = jnp.concatenate([lru_wa[0, dr], lru_wi[0, dr]], axis=-1).astype(BF16)
        _, h_ctx = _scan(zx_ctx, vecs, wg, jnp.zeros((b, D_RNN), F32), reverse=bool(dr))
        if dr == 0:
            h_fwd, _ = _scan(zx, vecs, wg, h_ctx, reverse=False)
        else:
            gr, _ = _scan(zx, vecs, wg, h_ctx, reverse=True, hf=h_fwd, zg=zg)

    f = _dft_apply(ct, sn, zfc, zfs)

    x1, h2, scores = _merge(f, gr, za, zb, x, pos, mods, norm2_g[0].reshape(1, d),
                            w_four[0].astype(BF16), w_lru[0].astype(BF16), w_out[0].astype(BF16),
                            w_router[0].T.astype(BF16))

    slot = _select(scores.reshape(b * N_EXPERTS, t), cap)
    y = _experts(slot, scores, h2, w_gate_e[0].astype(BF16), w_up_e[0].astype(BF16),
                 w_down_e[0].astype(BF16), cap)
    slot_t = jnp.swapaxes(slot.reshape(b, N_EXPERTS, t), 1, 2)
    return _combine(slot_t, y.reshape(b, N_EXPERTS * cap, d), x1, mods, final_g.reshape(1, d), cap)
```

```python
import functools
import math

import numpy as np
import jax
import jax.numpy as jnp
from jax import lax
from jax.experimental import pallas as pl
from jax.experimental.pallas import tpu as pltpu

F32 = jnp.float32
BF16 = jnp.bfloat16

D_MODEL = 1024
GRID_W = 64
N_FOURIER_GROUPS = 4
FOURIER_GROUP = 128
D_FOURIER = N_FOURIER_GROUPS * FOURIER_GROUP
D_RNN = D_MODEL
N_RNN_HEADS = 8
RNN_HEAD = D_RNN // N_RNN_HEADS
CONV_W = 4
CONV_LEFT = 2
LRU_C = 8.0
N_EXPERTS = 16
CAPACITY_FACTOR = 2
D_EXPERT = 1536
N_MOD = 6
EPS = 1e-6
POS_MAX_PERIOD = 10000.0

LANES = 128
SUBLANES = 8
BF16_ROWS = 16
VMEM_LIMIT = 56 * 1024 * 1024

TOKEN_BLOCK = 512
SCAN_CHUNK = 128
SCAN_STRIDE = SCAN_CHUNK + SUBLANES


def _params(*sem):
    return pltpu.CompilerParams(dimension_semantics=sem, vmem_limit_bytes=VMEM_LIMIT)


def _sigmoid(x):
    return 0.5 * jnp.tanh(0.5 * x) + 0.5


def _rmsnorm(x, g):
    return x * lax.rsqrt(jnp.mean(x * x, axis=-1, keepdims=True) + EPS) * g


def _dot(a, b):
    return jnp.dot(a, b, preferred_element_type=F32)


def _pos_kernel(o_ref):
    rows = o_ref.shape[0]
    q = D_MODEL // 4
    k = lax.broadcasted_iota(jnp.int32, (1, q), 1).astype(F32)
    freqs = jnp.exp(-math.log(POS_MAX_PERIOD) * k / q)
    ang_r = lax.broadcasted_iota(jnp.int32, (rows, q), 0).astype(F32) * freqs
    ang_c = lax.broadcasted_iota(jnp.int32, (GRID_W, q), 0).astype(F32) * freqs
    shape = (rows, GRID_W, q)
    o_ref[:, :, 0 * q:1 * q] = jnp.broadcast_to(jnp.sin(ang_r)[:, None, :], shape)
    o_ref[:, :, 1 * q:2 * q] = jnp.broadcast_to(jnp.cos(ang_r)[:, None, :], shape)
    o_ref[:, :, 2 * q:3 * q] = jnp.broadcast_to(jnp.sin(ang_c)[None, :, :], shape)
    o_ref[:, :, 3 * q:4 * q] = jnp.broadcast_to(jnp.cos(ang_c)[None, :, :], shape)


def _pos_table(n_tokens):
    rows = n_tokens // GRID_W
    out = pl.pallas_call(
        _pos_kernel,
        out_shape=jax.ShapeDtypeStruct((rows, GRID_W, D_MODEL), F32),
        name="pos_table",
    )()
    return out.reshape(n_tokens, D_MODEL)


RADIX = 8


def _dft_kernel(c_ref, s_ref, *, n):
    ns = c_ref.shape[0]
    k = RADIX * lax.broadcasted_iota(jnp.int32, (ns, ns), 0) + pl.program_id(0)
    t2 = lax.broadcasted_iota(jnp.int32, (ns, ns), 1)
    ang = ((k * t2) & (n - 1)).astype(F32) * (2.0 * math.pi / n)
    scale = 1.0 / math.sqrt(n)
    c_ref[...] = (jnp.cos(ang) * scale).astype(BF16)
    s_ref[...] = (jnp.sin(ang) * scale).astype(BF16)


def _dft_tables(n):
    assert n & (n - 1) == 0 and n % (RADIX * BF16_ROWS) == 0
    ns = n // RADIX
    return pl.pallas_call(
        functools.partial(_dft_kernel, n=n),
        grid=(RADIX,),
        out_specs=[pl.BlockSpec((None, ns, ns), lambda i: (i, 0, 0))] * 2,
        out_shape=[jax.ShapeDtypeStruct((RADIX, ns, ns), BF16)] * 2,
        compiler_params=_params("parallel"),
        name="dft_tables",
    )()


def _fold_kernel(w_ref, c_ref, s_ref, oc_ref, os_ref):
    for g in range(N_FOURIER_GROUPS):
        sl = slice(g * FOURIER_GROUP, (g + 1) * FOURIER_GROUP)
        w = w_ref[:, sl]
        oc_ref[:, sl] = jnp.dot(w, c_ref[...], preferred_element_type=F32,
                                precision=lax.Precision.HIGHEST).astype(BF16)
        os_ref[:, sl] = jnp.dot(w, s_ref[...], preferred_element_type=F32,
                                precision=lax.Precision.HIGHEST).astype(BF16)


def _fold_channel_dft(w_f):
    p = np.arange(FOURIER_GROUP)
    ang = 2.0 * np.pi * ((p[:, None] * p[None, :]) % FOURIER_GROUP) / FOURIER_GROUP
    cp = jnp.asarray(np.cos(ang) / math.sqrt(FOURIER_GROUP), F32)
    sp = jnp.asarray(np.sin(ang) / math.sqrt(FOURIER_GROUP), F32)
    return pl.pallas_call(
        _fold_kernel,
        out_shape=[jax.ShapeDtypeStruct(w_f.shape, BF16)] * 2,
        compiler_params=_params(),
        name="fold_channel_dft",
    )(w_f, cp, sp)


def _ada_kernel(c_ref, w_ref, b_ref, o_ref):
    c = c_ref[...]
    o_ref[...] = jnp.dot(c * _sigmoid(c), w_ref[...], preferred_element_type=F32,
                         precision=lax.Precision.HIGHEST) + b_ref[...]


def _ada(cc, w_ada, b_ada):
    n = w_ada.shape[1]
    tn = D_MODEL
    return pl.pallas_call(
        _ada_kernel,
        grid=(n // tn,),
        in_specs=[pl.BlockSpec(cc.shape, lambda j: (0, 0)),
                  pl.BlockSpec((D_MODEL, tn), lambda j: (0, j)),
                  pl.BlockSpec((1, tn), lambda j: (0, j))],
        out_specs=pl.BlockSpec((cc.shape[0], tn), lambda j: (0, j)),
        out_shape=jax.ShapeDtypeStruct((cc.shape[0], n), F32),
        compiler_params=_params("parallel"),
        name="ada",
    )(cc, w_ada, b_ada.reshape(1, n))


def _proj_kernel(*refs, widths, has_pos):
    if has_pos:
        x_ref, pos_ref, mod_ref, g_ref, w_ref = refs[:5]
        outs = refs[5:]
        x = x_ref[...] + pos_ref[...]
    else:
        x_ref, mod_ref, g_ref, w_ref = refs[:4]
        outs = refs[4:]
        x = x_ref[...]
    h = _rmsnorm(x, g_ref[...]) * (1.0 + mod_ref[1:2, :]) + mod_ref[0:1, :]
    hb = h.astype(BF16)
    c0 = 0
    for o_ref, wd in zip(outs, widths):
        o_ref[...] = _dot(hb, w_ref[:, c0:c0 + wd]).astype(BF16)
        c0 += wd


def _proj(x, pos, mods, mod_row, g, w, widths):
    b, t, d = x.shape
    tm = min(TOKEN_BLOCK, t)
    has_pos = pos is not None
    in_specs = [pl.BlockSpec((None, tm, d), lambda i, bb: (bb, i, 0))]
    args = [x]
    if has_pos:
        in_specs.append(pl.BlockSpec((tm, d), lambda i, bb: (i, 0)))
        args.append(pos)
    if mod_row is None:
        mod_map = lambda i, bb: (bb, 0, 0)
    else:
        mod_map = lambda i, bb: (mod_row, 0, 0)
    in_specs += [pl.BlockSpec((None, SUBLANES, d), mod_map),
                 pl.BlockSpec((1, d), lambda i, bb: (0, 0)),
                 pl.BlockSpec(w.shape, lambda i, bb: (0, 0))]
    args += [mods, g, w]
    return pl.pallas_call(
        functools.partial(_proj_kernel, widths=widths, has_pos=has_pos),
        grid=(t // tm, b),
        in_specs=in_specs,
        out_specs=[pl.BlockSpec((None, tm, wd), lambda i, bb: (bb, i, 0)) for wd in widths],
        out_shape=[jax.ShapeDtypeStruct((b, t, wd), BF16) for wd in widths],
        compiler_params=_params("parallel", "parallel"),
        name="proj",
    )(*args)


def _gelu(x):
    return 0.5 * x * (1.0 + jnp.tanh(math.sqrt(2.0 / math.pi) * (x + 0.044715 * x * x * x)))


def _scan_kernel(*refs, reverse, fuse, n_chunks):
    if fuse:
        (zx_ref, zp_ref, zn_ref, vec_ref, wg_ref, h0_ref, hf_ref, zg_ref,
         o_ref, hT_ref, ext_s, a_s, b_s, hc_s) = refs
    else:
        (zx_ref, zp_ref, zn_ref, vec_ref, wg_ref, h0_ref,
         o_ref, hT_ref, ext_s, a_s, b_s, hc_s) = refs
    nb, tc, c = zx_ref.shape
    step = pl.program_id(0)
    ci = (n_chunks - 1 - step) if reverse else step

    @pl.when(step == 0)
    def _():
        for hh in range(N_RNN_HEADS):
            hc_s[hh] = h0_ref[:, hh * RNN_HEAD:(hh + 1) * RNN_HEAD]

    ext_s[:, BF16_ROWS:BF16_ROWS + tc, :] = zx_ref[...].astype(F32)
    ext_s[:, 0:BF16_ROWS, :] = jnp.where(ci > 0, zp_ref[...].astype(F32), 0.0)
    ext_s[:, BF16_ROWS + tc:, :] = jnp.where(ci < n_chunks - 1, zn_ref[...].astype(F32), 0.0)

    lam = vec_ref[0:1, :]
    b_a = vec_ref[1:2, :]
    b_i = vec_ref[2:3, :]
    conv_b = vec_ref[3:4, :]
    neg_c_sp = -LRU_C * (jnp.maximum(-lam, 0.0) + jnp.log1p(jnp.exp(-jnp.abs(lam))))

    for bb in range(nb):
        u = conv_b
        for kk in range(CONV_W):
            r0 = BF16_ROWS - CONV_LEFT + kk
            u = u + vec_ref[4 + kk:5 + kk, :] * ext_s[bb, r0:r0 + tc, :]
        ub = u.astype(BF16)
        row0 = bb * SCAN_STRIDE
        for hh in range(N_RNN_HEADS):
            sl = slice(hh * RNN_HEAD, (hh + 1) * RNN_HEAD)
            gates = _dot(ub[:, sl], wg_ref[hh])
            r = _sigmoid(gates[:, :RNN_HEAD] + b_a[:, sl])
            ig = _sigmoid(gates[:, RNN_HEAD:] + b_i[:, sl])
            a = jnp.exp(r * neg_c_sp[:, sl])
            a_s[hh, row0:row0 + tc, :] = a
            b_s[hh, row0:row0 + tc, :] = jnp.sqrt(1.0 - a * a) * (ig * u[:, sl])

    def body(j, hs):
        t = (tc - 1 - j) if reverse else j
        idx = pl.ds(t, nb, stride=SCAN_STRIDE)
        new = []
        for hh in range(N_RNN_HEADS):
            h = a_s[hh, idx, :] * hs[hh] + b_s[hh, idx, :]
            b_s[hh, idx, :] = h
            new.append(h)
        return tuple(new)

    hs = lax.fori_loop(0, tc, body, tuple(hc_s[hh] for hh in range(N_RNN_HEADS)), unroll=8)
    for hh in range(N_RNN_HEADS):
        hc_s[hh] = hs[hh]
        hT_ref[:, hh * RNN_HEAD:(hh + 1) * RNN_HEAD] = hs[hh]

    for bb in range(nb):
        row0 = bb * SCAN_STRIDE
        for hh in range(N_RNN_HEADS):
            sl = slice(hh * RNN_HEAD, (hh + 1) * RNN_HEAD)
            hb = b_s[hh, row0:row0 + tc, :]
            if fuse:
                hb = _gelu(zg_ref[bb, :, sl].astype(F32)) * (hf_ref[bb, :, sl].astype(F32) + hb)
            o_ref[bb, :, sl] = hb.astype(BF16)


def _scan(zx, vecs, wg, h0, *, reverse, hf=None, zg=None):
    nb, t, c = zx.shape
    tc = SCAN_CHUNK
    n_chunks = t // tc
    fuse = hf is not None
    hpc = tc // BF16_ROWS
    n_halo = t // BF16_ROWS

    def cidx(i):
        return (n_chunks - 1 - i) if reverse else i

    main_spec = pl.BlockSpec((nb, tc, c), lambda i: (0, cidx(i), 0))
    in_specs = [
        main_spec,
        pl.BlockSpec((nb, BF16_ROWS, c), lambda i: (0, jnp.maximum(cidx(i) * hpc - 1, 0), 0)),
        pl.BlockSpec((nb, BF16_ROWS, c), lambda i: (0, jnp.minimum((cidx(i) + 1) * hpc, n_halo - 1), 0)),
        pl.BlockSpec(vecs.shape, lambda i: (0, 0)),
        pl.BlockSpec(wg.shape, lambda i: (0, 0, 0)),
        pl.BlockSpec(h0.shape, lambda i: (0, 0)),
    ]
    args = [zx, zx, zx, vecs, wg, h0]
    if fuse:
        in_specs += [main_spec, main_spec]
        args += [hf, zg]
    return pl.pallas_call(
        functools.partial(_scan_kernel, reverse=reverse, fuse=fuse, n_chunks=n_chunks),
        grid=(n_chunks,),
        in_specs=in_specs,
        out_specs=[main_spec, pl.BlockSpec((nb, c), lambda i: (0, 0))],
        out_shape=[jax.ShapeDtypeStruct((nb, t, c), BF16), jax.ShapeDtypeStruct((nb, c), F32)],
        scratch_shapes=[pltpu.VMEM((nb, tc + 2 * BF16_ROWS, c), F32),
                        pltpu.VMEM((N_RNN_HEADS, nb * SCAN_STRIDE, RNN_HEAD), F32),
                        pltpu.VMEM((N_RNN_HEADS, nb * SCAN_STRIDE, RNN_HEAD), F32),
                        pltpu.VMEM((N_RNN_HEADS, nb, RNN_HEAD), F32)],
        compiler_params=_params("arbitrary"),
        name="scan_bwd" if reverse else "scan_fwd",
    )(*args)


DFT_ROWS = 64
DFT_COLS = 256


def _dftmm_kernel(zc_ref, zs_ref, c_ref, s_ref, o_ref, ar_s, ai_s, o_s):
    t, f = zc_ref.shape
    ns = t // RADIX
    rb = min(DFT_ROWS, ns)
    r2 = math.sqrt(0.5)

    def dft4(x):
        s0 = (x[0][0] + x[2][0], x[0][1] + x[2][1])
        s1 = (x[0][0] - x[2][0], x[0][1] - x[2][1])
        s2 = (x[1][0] + x[3][0], x[1][1] + x[3][1])
        s3 = (x[1][0] - x[3][0], x[1][1] - x[3][1])
        return [(s0[0] + s2[0], s0[1] + s2[1]), (s1[0] + s3[1], s1[1] - s3[0]),
                (s0[0] - s2[0], s0[1] - s2[1]), (s1[0] - s3[1], s1[1] + s3[0])]

    def butterfly(i, carry):
        r0 = pl.multiple_of(i * rb, rb)
        w = [(zc_ref[pl.ds(j * ns + r0, rb), :].astype(F32), -zs_ref[pl.ds(j * ns + r0, rb), :].astype(F32))
             for j in range(RADIX)]
        u = [(w[j][0] + w[j + 4][0], w[j][1] + w[j + 4][1]) for j in range(4)]
        v = [(w[j][0] - w[j + 4][0], w[j][1] - w[j + 4][1]) for j in range(4)]
        v = [v[0],
             ((v[1][0] + v[1][1]) * r2, (v[1][1] - v[1][0]) * r2),
             (v[2][1], -v[2][0]),
             ((v[3][1] - v[3][0]) * r2, (-v[3][0] - v[3][1]) * r2)]
        even = dft4(u)
        odd = dft4(v)
        for m in range(4):
            for k1, a in ((2 * m, even[m]), (2 * m + 1, odd[m])):
                ar_s[k1, pl.ds(r0, rb), :] = a[0].astype(BF16)
                ai_s[k1, pl.ds(r0, rb), :] = a[1].astype(BF16)
        return carry

    lax.fori_loop(0, ns // rb, butterfly, 0)

    for c0 in range(0, f, DFT_COLS):
        for k1 in range(RADIX):
            res = (_dot(c_ref[k1], ar_s[k1, :, c0:c0 + DFT_COLS])
                   + _dot(s_ref[k1], ai_s[k1, :, c0:c0 + DFT_COLS]))
            for j in range(DFT_COLS // LANES):
                o_s[j, pl.ds(k1, ns, stride=RADIX), :] = res[:, j * LANES:(j + 1) * LANES]
        for j in range(DFT_COLS // LANES):
            o_ref[:, c0 + j * LANES:c0 + (j + 1) * LANES] = o_s[j].astype(BF16)


def _dft_apply(ct, st, zfc, zfs):
    b, t, f = zfc.shape
    ns = t // RADIX
    table = pl.BlockSpec((RADIX, ns, ns), lambda bb: (0, 0, 0), pipeline_mode=pl.Buffered(1))
    return pl.pallas_call(
        _dftmm_kernel,
        grid=(b,),
        in_specs=[pl.BlockSpec((None, t, f), lambda bb: (bb, 0, 0)),
                  pl.BlockSpec((None, t, f), lambda bb: (bb, 0, 0)),
                  table, table],
        out_specs=pl.BlockSpec((None, t, f), lambda bb: (bb, 0, 0)),
        out_shape=jax.ShapeDtypeStruct((b, t, f), BF16),
        scratch_shapes=[pltpu.VMEM((RADIX, ns, f), BF16),
                        pltpu.VMEM((RADIX, ns, f), BF16),
                        pltpu.VMEM((DFT_COLS // LANES, t, LANES), F32)],
        compiler_params=_params("parallel"),
        name="dft_apply",
    )(zfc, zfs, ct, st)


def _merge_kernel(f_ref, gr_ref, za_ref, zb_ref, x_ref, pos_ref, mod_ref, g2_ref,
                  wf_ref, wl_ref, wo_ref, wr_ref, x1_ref, h2_ref, sc_ref):
    y_four = _dot(f_ref[...], wf_ref[...])
    y_rnn = _dot(gr_ref[...], wl_ref[...])
    m = _sigmoid(za_ref[...].astype(F32)) * y_four + _sigmoid(zb_ref[...].astype(F32)) * y_rnn
    y = _dot(m.astype(BF16), wo_ref[...])
    x1 = (x_ref[...] + pos_ref[...]) + mod_ref[2:3, :] * y
    x1_ref[...] = x1
    h2 = (_rmsnorm(x1, g2_ref[...]) * (1.0 + mod_ref[4:5, :]) + mod_ref[3:4, :]).astype(BF16)
    h2_ref[...] = h2
    logits = lax.dot_general(wr_ref[...], h2, (((1,), (1,)), ((), ())), preferred_element_type=F32)
    e = jnp.exp(logits - jnp.max(logits, axis=0, keepdims=True))
    sc_ref[...] = e / jnp.sum(e, axis=0, keepdims=True)


def _merge(f, gr, za, zb, x, pos, mods, g2, w_four, w_lru, w_out, w_rt):
    b, t, d = x.shape
    tm = min(TOKEN_BLOCK, t)
    tok = lambda wd: pl.BlockSpec((None, tm, wd), lambda i, bb: (bb, i, 0))
    full = lambda a: pl.BlockSpec(a.shape, lambda i, bb: (0,) * a.ndim)
    return pl.pallas_call(
        _merge_kernel,
        grid=(t // tm, b),
        in_specs=[tok(D_FOURIER), tok(d), tok(d), tok(d), tok(d),
                  pl.BlockSpec((tm, d), lambda i, bb: (i, 0)),
                  pl.BlockSpec((None, SUBLANES, d), lambda i, bb: (bb, 0, 0)),
                  full(g2), full(w_four), full(w_lru), full(w_out), full(w_rt)],
        out_specs=[tok(d), tok(d), pl.BlockSpec((None, N_EXPERTS, tm), lambda i, bb: (bb, 0, i))],
        out_shape=[jax.ShapeDtypeStruct((b, t, d), F32),
                   jax.ShapeDtypeStruct((b, t, d), BF16),
                   jax.ShapeDtypeStruct((b, N_EXPERTS, t), F32)],
        compiler_params=_params("parallel", "parallel"),
        name="merge",
    )(f, gr, za, zb, x, pos, mods, g2, w_four, w_lru, w_out, w_rt)


MAX_EXP = 127
EXP_STEPS = 7
VAL_STEPS = 56


OFF_CHUNK = 256


def _select_kernel(s_ref, slot_ref, offs_ref, *, cap):
    rows, t = s_ref.shape

    def count(mask):
        return jnp.sum(jnp.where(mask, 1.0, 0.0), axis=1, keepdims=True)

    def exp_step(_, carry):
        k_lo, k_hi = carry
        k_mid = jnp.floor(0.5 * (k_lo + k_hi))
        ok = count(s_ref[...] >= jnp.exp2(-k_mid)) >= cap
        return jnp.where(ok, k_lo, k_mid), jnp.where(ok, k_mid, k_hi)

    k_lo, k_hi = lax.fori_loop(
        0, EXP_STEPS, exp_step,
        (jnp.full((rows, 1), -1.0, F32), jnp.full((rows, 1), float(MAX_EXP), F32)))
    lo0 = jnp.where(k_hi >= MAX_EXP, 0.0, jnp.exp2(-k_hi))
    hi0 = jnp.exp2(-k_lo)

    def val_step(_, carry):
        lo, hi = carry
        mid = 0.5 * (lo + hi)
        ok = count(s_ref[...] >= mid) >= cap
        return jnp.where(ok, mid, lo), jnp.where(ok, hi, mid)

    lo, hi = lax.fori_loop(0, VAL_STEPS, val_step, (lo0, hi0))
    s = s_ref[...]
    gt = s >= hi
    eq = (s >= lo) & (s < hi)
    need = cap - count(gt)

    p = lax.broadcasted_iota(jnp.int32, (LANES, LANES), 0)
    q = lax.broadcasted_iota(jnp.int32, (LANES, LANES), 1)
    tri = (p < q).astype(BF16)
    run_gt = jnp.zeros((rows, 1), F32)
    run_eq = jnp.zeros((rows, 1), F32)
    off_lane = lax.broadcasted_iota(jnp.int32, (rows, LANES), 1)
    offs = jnp.zeros((rows, LANES), F32)
    for j in range(t // LANES):
        if (j * LANES) % OFF_CHUNK == 0:
            offs = jnp.where(off_lane == (j * LANES) // OFF_CHUNK, run_gt + jnp.minimum(run_eq, need), offs)
        sl = slice(j * LANES, (j + 1) * LANES)
        g = gt[:, sl]
        q_ = eq[:, sl]
        pg = _dot(g.astype(BF16), tri) + run_gt
        pe = _dot(q_.astype(BF16), tri) + run_eq
        sel = g | (q_ & (pe < need))
        pos = pg + jnp.minimum(pe, need)
        slot_ref[:, sl] = jnp.where(sel, pos, -1.0).astype(jnp.int32)
        run_gt = run_gt + count(g)
        run_eq = run_eq + count(q_)
    offs = jnp.where(off_lane == t // OFF_CHUNK, run_gt + jnp.minimum(run_eq, need), offs)
    offs_ref[...] = offs.astype(jnp.int32)


def _select(scores, cap):
    rows, t = scores.shape
    assert t % OFF_CHUNK == 0 and t // OFF_CHUNK < LANES
    return pl.pallas_call(
        functools.partial(_select_kernel, cap=cap),
        out_shape=[jax.ShapeDtypeStruct((rows, t), jnp.int32),
                   jax.ShapeDtypeStruct((rows, LANES), jnp.int32)],
        compiler_params=_params(),
        name="select",
    )(scores)


def _expert_kernel(slot_ref, sc_ref, h_ref, wg_ref, wu_ref, wd_ref, o_ref, *, cap):
    t = h_ref.shape[0]
    tk = min(TOKEN_BLOCK, t)
    slot_iota = lax.broadcasted_iota(jnp.int32, (cap, tk), 0)
    xg = jnp.zeros((cap, D_MODEL), F32)
    gs = jnp.zeros((cap, 1), F32)
    for j in range(t // tk):
        sl = slice(j * tk, (j + 1) * tk)
        onehot = slot_iota == slot_ref[:, sl]
        xg = xg + _dot(onehot.astype(BF16), h_ref[sl, :])
        gs = gs + jnp.sum(jnp.where(onehot, sc_ref[:, sl], 0.0), axis=1, keepdims=True)
    xb = xg.astype(BF16)
    gate = _dot(xb, wg_ref[...])
    hid = (gate * _sigmoid(gate)) * _dot(xb, wu_ref[...])
    y = _dot(hid.astype(BF16), wd_ref[...])
    o_ref[...] = (y * gs).astype(BF16)


def _experts(slot, scores, h2, wg, wu, wd, cap):
    b, t, d = h2.shape
    row = pl.BlockSpec((None, 1, t), lambda bb, e: (bb * N_EXPERTS + e, 0, 0))
    return pl.pallas_call(
        functools.partial(_expert_kernel, cap=cap),
        grid=(b, N_EXPERTS),
        in_specs=[row, row,
                  pl.BlockSpec((None, t, d), lambda bb, e: (bb, 0, 0)),
                  pl.BlockSpec((None, d, D_EXPERT), lambda bb, e: (e, 0, 0)),
                  pl.BlockSpec((None, d, D_EXPERT), lambda bb, e: (e, 0, 0)),
                  pl.BlockSpec((None, D_EXPERT, d), lambda bb, e: (e, 0, 0))],
        out_specs=pl.BlockSpec((None, cap, d), lambda bb, e: (bb * N_EXPERTS + e, 0, 0)),
        out_shape=jax.ShapeDtypeStruct((b * N_EXPERTS, cap, d), BF16),
        compiler_params=_params("parallel", "parallel"),
        name="experts",
    )(slot.reshape(b * N_EXPERTS, 1, t), scores.reshape(b * N_EXPERTS, 1, t), h2, wg, wu, wd)


COMBINE_BLOCK = 1024
SLOT_WINDOW = 256


def _combine_kernel(offs_ref, slot_ref, y_ref, x1_ref, mod_ref, g_ref, o_ref, acc_s, *, cap, n_off):
    tm = x1_ref.shape[0]
    kc = min(SLOT_WINDOW, cap)
    per_block = tm // OFF_CHUNK
    lane_iota = lax.broadcasted_iota(jnp.int32, (tm, kc), 1)
    acc_s[...] = jnp.zeros_like(acc_s)
    for e in range(N_EXPERTS):
        o0 = (pl.program_id(0) * N_EXPERTS + e) * n_off + pl.program_id(1) * per_block
        first = offs_ref[o0]
        last = offs_ref[o0 + per_block]
        start0 = (first // BF16_ROWS) * BF16_ROWS
        col = slot_ref[:, e:e + 1]

        def window(c, carry):
            lo = start0 + c * kc
            st = pl.multiple_of(jnp.minimum(lo, cap - kc), BF16_ROWS)
            onehot = (col == st + lane_iota) & (col >= lo)
            acc_s[...] += _dot(onehot.astype(BF16), y_ref[pl.ds(e * cap + st, kc), :])
            return carry

        lax.fori_loop(0, (last - start0 + kc - 1) // kc, window, 0)
    x2 = x1_ref[...] + mod_ref[5:6, :] * acc_s[...]
    o_ref[...] = _rmsnorm(x2, g_ref[...])


def _combine(offs, slot_t, y, x1, mods, g, cap):
    b, t, d = x1.shape
    tm = min(COMBINE_BLOCK, t)
    n_off = offs.shape[0] // (b * N_EXPERTS)
    grid_spec = pltpu.PrefetchScalarGridSpec(
        num_scalar_prefetch=1,
        grid=(b, t // tm),
        in_specs=[pl.BlockSpec((None, tm, N_EXPERTS), lambda bb, i, o: (bb, i, 0)),
                  pl.BlockSpec((None, N_EXPERTS * cap, d), lambda bb, i, o: (bb, 0, 0),
                               pipeline_mode=pl.Buffered(1)),
                  pl.BlockSpec((None, tm, d), lambda bb, i, o: (bb, i, 0)),
                  pl.BlockSpec((None, SUBLANES, d), lambda bb, i, o: (bb, 0, 0)),
                  pl.BlockSpec((1, d), lambda bb, i, o: (0, 0))],
        out_specs=pl.BlockSpec((None, tm, d), lambda bb, i, o: (bb, i, 0)),
        scratch_shapes=[pltpu.VMEM((tm, d), F32)],
    )
    return pl.pallas_call(
        functools.partial(_combine_kernel, cap=cap, n_off=n_off),
        grid_spec=grid_spec,
        out_shape=jax.ShapeDtypeStruct((b, t, d), F32),
        compiler_params=_params("parallel", "parallel"),
        name="combine",
    )(offs, slot_t, y, x1, mods, g)


def kernel(x, c, ctx, c_ctx, w_ada, b_ada, norm1_g, norm2_g, w_in, w_four, conv_w, conv_b, lru_lambda, lru_wa, lru_ba, lru_wi, lru_bi, w_lru, w_out, w_router, w_gate_e, w_up_e, w_down_e, final_g):
    b, t, d = x.shape
    assert d == D_MODEL and b == SUBLANES and t % TOKEN_BLOCK == 0 and ctx.shape[1] % SCAN_CHUNK == 0
    assert w_ada.shape[0] == 1, "single-layer problem: the context stream is only read by the recurrence"
    cap = CAPACITY_FACTOR * t // N_EXPERTS
    col_x = D_FOURIER + D_RNN
    col_g = col_x + D_RNN
    col_a = col_g + D_MODEL

    pos = _pos_table(t)
    ct, st = _dft_tables(t)

    cc = jnp.concatenate([c, c_ctx[None], jnp.zeros((SUBLANES - 1, d), F32)], axis=0)
    mods = _ada(cc, w_ada[0], b_ada[0]).reshape(cc.shape[0], N_MOD, d)
    mods = jnp.pad(mods, ((0, 0), (0, SUBLANES - N_MOD), (0, 0)))

    w = w_in[0]
    wfc, wfs = _fold_channel_dft(w[:, :D_FOURIER])
    w_x = w[:, D_FOURIER:col_x].astype(BF16)
    w_cat = jnp.concatenate([w_x, w[:, col_x:].astype(BF16), wfc, wfs], axis=1)
    g1 = norm1_g[0].reshape(1, d)
    zx, zg, za, zb, zfc, zfs = _proj(x, pos, mods, None, g1, w_cat,
                                     (D_RNN, D_RNN, D_MODEL, D_MODEL, D_FOURIER, D_FOURIER))
    (zx_ctx,) = _proj(ctx, None, mods, b, g1, w_x, (D_RNN,))

    for dr in range(2):
        vecs = jnp.concatenate([lru_lambda[0, dr][None], lru_ba[0, dr][None], lru_bi[0, dr][None],
                                conv_b[0][None], conv_w[0]], axis=0)
        wg = jnp.concatenate([lru_wa[0, dr], lru_wi[0, dr]], axis=-1).astype(BF16)
        _, h_ctx = _scan(zx_ctx, vecs, wg, jnp.zeros((b, D_RNN), F32), reverse=bool(dr))
        if dr == 0:
            h_fwd, _ = _scan(zx, vecs, wg, h_ctx, reverse=False)
        else:
            gr, _ = _scan(zx, vecs, wg, h_ctx, reverse=True, hf=h_fwd, zg=zg)

    f = _dft_apply(ct, st, zfc, zfs)

    x1, h2, scores = _merge(f, gr, za, zb, x, pos, mods, norm2_g[0].reshape(1, d),
                            w_four[0].astype(BF16), w_lru[0].astype(BF16), w_out[0].astype(BF16),
                            w_router[0].T.astype(BF16))

    slot, offs = _select(scores.reshape(b * N_EXPERTS, t), cap)
    offs = offs[:, :t // OFF_CHUNK + 1].reshape(-1)
    y = _experts(slot, scores, h2, w_gate_e[0].astype(BF16), w_up_e[0].astype(BF16),
                 w_down_e[0].astype(BF16), cap)
    slot_t = jnp.swapaxes(slot.reshape(b, N_EXPERTS, t), 1, 2)
    return _combine(offs, slot_t, y.reshape(b, N_EXPERTS * cap, d), x1, mods, final_g.reshape(1, d), cap)
```

```python
import functools
import math

import numpy as np
import jax
import jax.numpy as jnp
from jax import lax
from jax.experimental import pallas as pl
from jax.experimental.pallas import tpu as pltpu

F32 = jnp.float32
BF16 = jnp.bfloat16

D_MODEL = 1024
GRID_W = 64
N_FOURIER_GROUPS = 4
FOURIER_GROUP = 128
D_FOURIER = N_FOURIER_GROUPS * FOURIER_GROUP
D_RNN = D_MODEL
N_RNN_HEADS = 8
RNN_HEAD = D_RNN // N_RNN_HEADS
CONV_W = 4
CONV_LEFT = 2
LRU_C = 8.0
N_EXPERTS = 16
CAPACITY_FACTOR = 2
D_EXPERT = 1536
N_MOD = 6
EPS = 1e-6
POS_MAX_PERIOD = 10000.0

LANES = 128
SUBLANES = 8
BF16_ROWS = 16
VMEM_LIMIT = 56 * 1024 * 1024

TOKEN_BLOCK = 512
SCAN_CHUNK = 128
SCAN_STRIDE = SCAN_CHUNK + SUBLANES


def _params(*sem):
    return pltpu.CompilerParams(dimension_semantics=sem, vmem_limit_bytes=VMEM_LIMIT)


def _sigmoid(x):
    return 0.5 * jnp.tanh(0.5 * x) + 0.5


def _rmsnorm(x, g):
    return x * lax.rsqrt(jnp.mean(x * x, axis=-1, keepdims=True) + EPS) * g


def _dot(a, b):
    return jnp.dot(a, b, preferred_element_type=F32)


def _pos_kernel(o_ref):
    rows = o_ref.shape[0]
    q = D_MODEL // 4
    k = lax.broadcasted_iota(jnp.int32, (1, q), 1).astype(F32)
    freqs = jnp.exp(-math.log(POS_MAX_PERIOD) * k / q)
    ang_r = lax.broadcasted_iota(jnp.int32, (rows, q), 0).astype(F32) * freqs
    ang_c = lax.broadcasted_iota(jnp.int32, (GRID_W, q), 0).astype(F32) * freqs
    shape = (rows, GRID_W, q)
    o_ref[:, :, 0 * q:1 * q] = jnp.broadcast_to(jnp.sin(ang_r)[:, None, :], shape)
    o_ref[:, :, 1 * q:2 * q] = jnp.broadcast_to(jnp.cos(ang_r)[:, None, :], shape)
    o_ref[:, :, 2 * q:3 * q] = jnp.broadcast_to(jnp.sin(ang_c)[None, :, :], shape)
    o_ref[:, :, 3 * q:4 * q] = jnp.broadcast_to(jnp.cos(ang_c)[None, :, :], shape)


def _pos_table(n_tokens):
    rows = n_tokens // GRID_W
    out = pl.pallas_call(
        _pos_kernel,
        out_shape=jax.ShapeDtypeStruct((rows, GRID_W, D_MODEL), F32),
        name="pos_table",
    )()
    return out.reshape(n_tokens, D_MODEL)


RADIX = 8


def _dft_kernel(c_ref, s_ref, *, n):
    ns = c_ref.shape[0]
    k = RADIX * lax.broadcasted_iota(jnp.int32, (ns, ns), 0) + pl.program_id(0)
    t2 = lax.broadcasted_iota(jnp.int32, (ns, ns), 1)
    ang = ((k * t2) & (n - 1)).astype(F32) * (2.0 * math.pi / n)
    scale = 1.0 / math.sqrt(n)
    c_ref[...] = (jnp.cos(ang) * scale).astype(BF16)
    s_ref[...] = (jnp.sin(ang) * scale).astype(BF16)


def _dft_tables(n):
    assert n & (n - 1) == 0 and n % (RADIX * BF16_ROWS) == 0
    ns = n // RADIX
    return pl.pallas_call(
        functools.partial(_dft_kernel, n=n),
        grid=(RADIX,),
        out_specs=[pl.BlockSpec((None, ns, ns), lambda i: (i, 0, 0))] * 2,
        out_shape=[jax.ShapeDtypeStruct((RADIX, ns, ns), BF16)] * 2,
        compiler_params=_params("parallel"),
        name="dft_tables",
    )()


def _fold_kernel(w_ref, c_ref, s_ref, oc_ref, os_ref):
    for g in range(N_FOURIER_GROUPS):
        sl = slice(g * FOURIER_GROUP, (g + 1) * FOURIER_GROUP)
        w = w_ref[:, sl]
        oc_ref[:, sl] = jnp.dot(w, c_ref[...], preferred_element_type=F32,
                                precision=lax.Precision.HIGHEST).astype(BF16)
        os_ref[:, sl] = jnp.dot(w, s_ref[...], preferred_element_type=F32,
                                precision=lax.Precision.HIGHEST).astype(BF16)


def _fold_channel_dft(w_f):
    p = np.arange(FOURIER_GROUP)
    ang = 2.0 * np.pi * ((p[:, None] * p[None, :]) % FOURIER_GROUP) / FOURIER_GROUP
    cp = jnp.asarray(np.cos(ang) / math.sqrt(FOURIER_GROUP), F32)
    sp = jnp.asarray(np.sin(ang) / math.sqrt(FOURIER_GROUP), F32)
    return pl.pallas_call(
        _fold_kernel,
        out_shape=[jax.ShapeDtypeStruct(w_f.shape, BF16)] * 2,
        compiler_params=_params(),
        name="fold_channel_dft",
    )(w_f, cp, sp)


def _ada_kernel(c_ref, w_ref, b_ref, o_ref):
    c = c_ref[...]
    o_ref[...] = jnp.dot(c * _sigmoid(c), w_ref[...], preferred_element_type=F32,
                         precision=lax.Precision.HIGHEST) + b_ref[...]


def _ada(cc, w_ada, b_ada):
    n = w_ada.shape[1]
    tn = D_MODEL
    return pl.pallas_call(
        _ada_kernel,
        grid=(n // tn,),
        in_specs=[pl.BlockSpec(cc.shape, lambda j: (0, 0)),
                  pl.BlockSpec((D_MODEL, tn), lambda j: (0, j)),
                  pl.BlockSpec((1, tn), lambda j: (0, j))],
        out_specs=pl.BlockSpec((cc.shape[0], tn), lambda j: (0, j)),
        out_shape=jax.ShapeDtypeStruct((cc.shape[0], n), F32),
        compiler_params=_params("parallel"),
        name="ada",
    )(cc, w_ada, b_ada.reshape(1, n))


def _proj_kernel(*refs, widths, has_pos):
    if has_pos:
        x_ref, pos_ref, mod_ref, g_ref, w_ref = refs[:5]
        outs = refs[5:]
        x = x_ref[...] + pos_ref[...]
    else:
        x_ref, mod_ref, g_ref, w_ref = refs[:4]
        outs = refs[4:]
        x = x_ref[...]
    h = _rmsnorm(x, g_ref[...]) * (1.0 + mod_ref[1:2, :]) + mod_ref[0:1, :]
    hb = h.astype(BF16)
    c0 = 0
    for o_ref, wd in zip(outs, widths):
        o_ref[...] = _dot(hb, w_ref[:, c0:c0 + wd]).astype(BF16)
        c0 += wd


def _proj(x, pos, mods, mod_row, g, w, widths):
    b, t, d = x.shape
    tm = min(TOKEN_BLOCK, t)
    has_pos = pos is not None
    in_specs = [pl.BlockSpec((None, tm, d), lambda i, bb: (bb, i, 0))]
    args = [x]
    if has_pos:
        in_specs.append(pl.BlockSpec((tm, d), lambda i, bb: (i, 0)))
        args.append(pos)
    if mod_row is None:
        mod_map = lambda i, bb: (bb, 0, 0)
    else:
        mod_map = lambda i, bb: (mod_row, 0, 0)
    in_specs += [pl.BlockSpec((None, SUBLANES, d), mod_map),
                 pl.BlockSpec((1, d), lambda i, bb: (0, 0)),
                 pl.BlockSpec(w.shape, lambda i, bb: (0, 0))]
    args += [mods, g, w]
    return pl.pallas_call(
        functools.partial(_proj_kernel, widths=widths, has_pos=has_pos),
        grid=(t // tm, b),
        in_specs=in_specs,
        out_specs=[pl.BlockSpec((None, tm, wd), lambda i, bb: (bb, i, 0)) for wd in widths],
        out_shape=[jax.ShapeDtypeStruct((b, t, wd), BF16) for wd in widths],
        compiler_params=_params("parallel", "parallel"),
        name="proj",
    )(*args)


def _gelu(x):
    return 0.5 * x * (1.0 + jnp.tanh(math.sqrt(2.0 / math.pi) * (x + 0.044715 * x * x * x)))


def _scan_kernel(*refs, reverse, fuse, n_chunks):
    if fuse:
        (zx_ref, zp_ref, zn_ref, vec_ref, wg_ref, h0_ref, hf_ref, zg_ref,
         o_ref, hT_ref, ext_s, a_s, b_s, hc_s) = refs
    else:
        (zx_ref, zp_ref, zn_ref, vec_ref, wg_ref, h0_ref,
         o_ref, hT_ref, ext_s, a_s, b_s, hc_s) = refs
    nb, tc, c = zx_ref.shape
    step = pl.program_id(0)
    ci = (n_chunks - 1 - step) if reverse else step

    @pl.when(step == 0)
    def _():
        for hh in range(N_RNN_HEADS):
            hc_s[hh] = h0_ref[:, hh * RNN_HEAD:(hh + 1) * RNN_HEAD]

    ext_s[:, BF16_ROWS:BF16_ROWS + tc, :] = zx_ref[...].astype(F32)
    ext_s[:, 0:BF16_ROWS, :] = jnp.where(ci > 0, zp_ref[...].astype(F32), 0.0)
    ext_s[:, BF16_ROWS + tc:, :] = jnp.where(ci < n_chunks - 1, zn_ref[...].astype(F32), 0.0)

    lam = vec_ref[0:1, :]
    b_a = vec_ref[1:2, :]
    b_i = vec_ref[2:3, :]
    conv_b = vec_ref[3:4, :]
    neg_c_sp = -LRU_C * (jnp.maximum(-lam, 0.0) + jnp.log1p(jnp.exp(-jnp.abs(lam))))

    for bb in range(nb):
        u = conv_b
        for kk in range(CONV_W):
            r0 = BF16_ROWS - CONV_LEFT + kk
            u = u + vec_ref[4 + kk:5 + kk, :] * ext_s[bb, r0:r0 + tc, :]
        ub = u.astype(BF16)
        row0 = bb * SCAN_STRIDE
        for hh in range(N_RNN_HEADS):
            sl = slice(hh * RNN_HEAD, (hh + 1) * RNN_HEAD)
            gates = _dot(ub[:, sl], wg_ref[hh])
            r = _sigmoid(gates[:, :RNN_HEAD] + b_a[:, sl])
            ig = _sigmoid(gates[:, RNN_HEAD:] + b_i[:, sl])
            a = jnp.exp(r * neg_c_sp[:, sl])
            a_s[hh, row0:row0 + tc, :] = a
            b_s[hh, row0:row0 + tc, :] = jnp.sqrt(1.0 - a * a) * (ig * u[:, sl])

    def body(j, hs):
        t = (tc - 1 - j) if reverse else j
        idx = pl.ds(t, nb, stride=SCAN_STRIDE)
        new = []
        for hh in range(N_RNN_HEADS):
            h = a_s[hh, idx, :] * hs[hh] + b_s[hh, idx, :]
            b_s[hh, idx, :] = h
            new.append(h)
        return tuple(new)

    hs = lax.fori_loop(0, tc, body, tuple(hc_s[hh] for hh in range(N_RNN_HEADS)), unroll=8)
    for hh in range(N_RNN_HEADS):
        hc_s[hh] = hs[hh]
        hT_ref[:, hh * RNN_HEAD:(hh + 1) * RNN_HEAD] = hs[hh]

    for bb in range(nb):
        row0 = bb * SCAN_STRIDE
        for hh in range(N_RNN_HEADS):
            sl = slice(hh * RNN_HEAD, (hh + 1) * RNN_HEAD)
            hb = b_s[hh, row0:row0 + tc, :]
            if fuse:
                hb = _gelu(zg_ref[bb, :, sl].astype(F32)) * (hf_ref[bb, :, sl].astype(F32) + hb)
            o_ref[bb, :, sl] = hb.astype(BF16)


def _scan(zx, vecs, wg, h0, *, reverse, hf=None, zg=None):
    nb, t, c = zx.shape
    tc = SCAN_CHUNK
    n_chunks = t // tc
    fuse = hf is not None
    hpc = tc // BF16_ROWS
    n_halo = t // BF16_ROWS

    def cidx(i):
        return (n_chunks - 1 - i) if reverse else i

    main_spec = pl.BlockSpec((nb, tc, c), lambda i: (0, cidx(i), 0))
    in_specs = [
        main_spec,
        pl.BlockSpec((nb, BF16_ROWS, c), lambda i: (0, jnp.maximum(cidx(i) * hpc - 1, 0), 0)),
        pl.BlockSpec((nb, BF16_ROWS, c), lambda i: (0, jnp.minimum((cidx(i) + 1) * hpc, n_halo - 1), 0)),
        pl.BlockSpec(vecs.shape, lambda i: (0, 0)),
        pl.BlockSpec(wg.shape, lambda i: (0, 0, 0)),
        pl.BlockSpec(h0.shape, lambda i: (0, 0)),
    ]
    args = [zx, zx, zx, vecs, wg, h0]
    if fuse:
        in_specs += [main_spec, main_spec]
        args += [hf, zg]
    return pl.pallas_call(
        functools.partial(_scan_kernel, reverse=reverse, fuse=fuse, n_chunks=n_chunks),
        grid=(n_chunks,),
        in_specs=in_specs,
        out_specs=[main_spec, pl.BlockSpec((nb, c), lambda i: (0, 0))],
        out_shape=[jax.ShapeDtypeStruct((nb, t, c), BF16), jax.ShapeDtypeStruct((nb, c), F32)],
        scratch_shapes=[pltpu.VMEM((nb, tc + 2 * BF16_ROWS, c), F32),
                        pltpu.VMEM((N_RNN_HEADS, nb * SCAN_STRIDE, RNN_HEAD), F32),
                        pltpu.VMEM((N_RNN_HEADS, nb * SCAN_STRIDE, RNN_HEAD), F32),
                        pltpu.VMEM((N_RNN_HEADS, nb, RNN_HEAD), F32)],
        compiler_params=_params("arbitrary"),
        name="scan_bwd" if reverse else "scan_fwd",
    )(*args)


DFT_ROWS = 64
DFT_COLS = 256


def _dftmm_kernel(zc_ref, zs_ref, c_ref, s_ref, o_ref, ar_s, ai_s, o_s):
    t, f = zc_ref.shape
    ns = t // RADIX
    rb = min(DFT_ROWS, ns)
    r2 = math.sqrt(0.5)

    def dft4(x):
        s0 = (x[0][0] + x[2][0], x[0][1] + x[2][1])
        s1 = (x[0][0] - x[2][0], x[0][1] - x[2][1])
        s2 = (x[1][0] + x[3][0], x[1][1] + x[3][1])
        s3 = (x[1][0] - x[3][0], x[1][1] - x[3][1])
        return [(s0[0] + s2[0], s0[1] + s2[1]), (s1[0] + s3[1], s1[1] - s3[0]),
                (s0[0] - s2[0], s0[1] - s2[1]), (s1[0] - s3[1], s1[1] + s3[0])]

    def butterfly(i, carry):
        r0 = pl.multiple_of(i * rb, rb)
        w = [(zc_ref[pl.ds(j * ns + r0, rb), :].astype(F32), -zs_ref[pl.ds(j * ns + r0, rb), :].astype(F32))
             for j in range(RADIX)]
        u = [(w[j][0] + w[j + 4][0], w[j][1] + w[j + 4][1]) for j in range(4)]
        v = [(w[j][0] - w[j + 4][0], w[j][1] - w[j + 4][1]) for j in range(4)]
        v = [v[0],
             ((v[1][0] + v[1][1]) * r2, (v[1][1] - v[1][0]) * r2),
             (v[2][1], -v[2][0]),
             ((v[3][1] - v[3][0]) * r2, (-v[3][0] - v[3][1]) * r2)]
        even = dft4(u)
        odd = dft4(v)
        for m in range(4):
            for k1, a in ((2 * m, even[m]), (2 * m + 1, odd[m])):
                ar_s[k1, pl.ds(r0, rb), :] = a[0].astype(BF16)
                ai_s[k1, pl.ds(r0, rb), :] = a[1].astype(BF16)
        return carry

    lax.fori_loop(0, ns // rb, butterfly, 0)

    for c0 in range(0, f, DFT_COLS):
        for k1 in range(RADIX):
            res = (_dot(c_ref[k1], ar_s[k1, :, c0:c0 + DFT_COLS])
                   + _dot(s_ref[k1], ai_s[k1, :, c0:c0 + DFT_COLS]))
            for j in range(DFT_COLS // LANES):
                o_s[j, pl.ds(k1, ns, stride=RADIX), :] = res[:, j * LANES:(j + 1) * LANES]
        for j in range(DFT_COLS // LANES):
            o_ref[:, c0 + j * LANES:c0 + (j + 1) * LANES] = o_s[j].astype(BF16)


def _dft_apply(ct, st, zfc, zfs):
    b, t, f = zfc.shape
    ns = t // RADIX
    table = pl.BlockSpec((RADIX, ns, ns), lambda bb: (0, 0, 0), pipeline_mode=pl.Buffered(1))
    return pl.pallas_call(
        _dftmm_kernel,
        grid=(b,),
        in_specs=[pl.BlockSpec((None, t, f), lambda bb: (bb, 0, 0)),
                  pl.BlockSpec((None, t, f), lambda bb: (bb, 0, 0)),
                  table, table],
        out_specs=pl.BlockSpec((None, t, f), lambda bb: (bb, 0, 0)),
        out_shape=jax.ShapeDtypeStruct((b, t, f), BF16),
        scratch_shapes=[pltpu.VMEM((RADIX, ns, f), BF16),
                        pltpu.VMEM((RADIX, ns, f), BF16),
                        pltpu.VMEM((DFT_COLS // LANES, t, LANES), F32)],
        compiler_params=_params("parallel"),
        name="dft_apply",
    )(zfc, zfs, ct, st)


def _merge_kernel(f_ref, gr_ref, za_ref, zb_ref, x_ref, pos_ref, mod_ref, g2_ref,
                  wf_ref, wl_ref, wo_ref, wr_ref, x1_ref, h2_ref, sc_ref):
    y_four = _dot(f_ref[...], wf_ref[...])
    y_rnn = _dot(gr_ref[...], wl_ref[...])
    m = _sigmoid(za_ref[...].astype(F32)) * y_four + _sigmoid(zb_ref[...].astype(F32)) * y_rnn
    y = _dot(m.astype(BF16), wo_ref[...])
    x1 = (x_ref[...] + pos_ref[...]) + mod_ref[2:3, :] * y
    x1_ref[...] = x1
    h2 = (_rmsnorm(x1, g2_ref[...]) * (1.0 + mod_ref[4:5, :]) + mod_ref[3:4, :]).astype(BF16)
    h2_ref[...] = h2
    logits = lax.dot_general(wr_ref[...], h2, (((1,), (1,)), ((), ())), preferred_element_type=F32)
    e = jnp.exp(logits - jnp.max(logits, axis=0, keepdims=True))
    sc_ref[...] = e / jnp.sum(e, axis=0, keepdims=True)


def _merge(f, gr, za, zb, x, pos, mods, g2, w_four, w_lru, w_out, w_rt):
    b, t, d = x.shape
    tm = min(TOKEN_BLOCK, t)
    tok = lambda wd: pl.BlockSpec((None, tm, wd), lambda i, bb: (bb, i, 0))
    full = lambda a: pl.BlockSpec(a.shape, lambda i, bb: (0,) * a.ndim)
    return pl.pallas_call(
        _merge_kernel,
        grid=(t // tm, b),
        in_specs=[tok(D_FOURIER), tok(d), tok(d), tok(d), tok(d),
                  pl.BlockSpec((tm, d), lambda i, bb: (i, 0)),
                  pl.BlockSpec((None, SUBLANES, d), lambda i, bb: (bb, 0, 0)),
                  full(g2), full(w_four), full(w_lru), full(w_out), full(w_rt)],
        out_specs=[tok(d), tok(d), pl.BlockSpec((None, N_EXPERTS, tm), lambda i, bb: (bb, 0, i))],
        out_shape=[jax.ShapeDtypeStruct((b, t, d), F32),
                   jax.ShapeDtypeStruct((b, t, d), BF16),
                   jax.ShapeDtypeStruct((b, N_EXPERTS, t), F32)],
        compiler_params=_params("parallel", "parallel"),
        name="merge",
    )(f, gr, za, zb, x, pos, mods, g2, w_four, w_lru, w_out, w_rt)


MAX_EXP = 127
EXP_STEPS = 7
VAL_STEPS = 56


OFF_CHUNK = 256


def _select_kernel(s_ref, slot_ref, offs_ref, *, cap):
    rows, t = s_ref.shape

    def count(mask):
        return jnp.sum(jnp.where(mask, 1.0, 0.0), axis=1, keepdims=True)

    def exp_step(_, carry):
        k_lo, k_hi = carry
        k_mid = jnp.floor(0.5 * (k_lo + k_hi))
        ok = count(s_ref[...] >= jnp.exp2(-k_mid)) >= cap
        return jnp.where(ok, k_lo, k_mid), jnp.where(ok, k_mid, k_hi)

    k_lo, k_hi = lax.fori_loop(
        0, EXP_STEPS, exp_step,
        (jnp.full((rows, 1), -1.0, F32), jnp.full((rows, 1), float(MAX_EXP), F32)))
    lo0 = jnp.where(k_hi >= MAX_EXP, 0.0, jnp.exp2(-k_hi))
    hi0 = jnp.exp2(-k_lo)

    def val_step(_, carry):
        lo, hi = carry
        mid = 0.5 * (lo + hi)
        ok = count(s_ref[...] >= mid) >= cap
        return jnp.where(ok, mid, lo), jnp.where(ok, hi, mid)

    lo, hi = lax.fori_loop(0, VAL_STEPS, val_step, (lo0, hi0))
    s = s_ref[...]
    gt = s >= hi
    eq = (s >= lo) & (s < hi)
    need = cap - count(gt)

    p = lax.broadcasted_iota(jnp.int32, (LANES, LANES), 0)
    q = lax.broadcasted_iota(jnp.int32, (LANES, LANES), 1)
    tri = (p < q).astype(BF16)
    run_gt = jnp.zeros((rows, 1), F32)
    run_eq = jnp.zeros((rows, 1), F32)
    off_lane = lax.broadcasted_iota(jnp.int32, (rows, LANES), 1)
    offs = jnp.zeros((rows, LANES), F32)
    for j in range(t // LANES):
        if (j * LANES) % OFF_CHUNK == 0:
            offs = jnp.where(off_lane == (j * LANES) // OFF_CHUNK, run_gt + jnp.minimum(run_eq, need), offs)
        sl = slice(j * LANES, (j + 1) * LANES)
        g = gt[:, sl]
        q_ = eq[:, sl]
        pg = _dot(g.astype(BF16), tri) + run_gt
        pe = _dot(q_.astype(BF16), tri) + run_eq
        sel = g | (q_ & (pe < need))
        pos = pg + jnp.minimum(pe, need)
        slot_ref[:, sl] = jnp.where(sel, pos, -1.0).astype(jnp.int32)
        run_gt = run_gt + count(g)
        run_eq = run_eq + count(q_)
    offs = jnp.where(off_lane == t // OFF_CHUNK, run_gt + jnp.minimum(run_eq, need), offs)
    offs_ref[...] = offs.astype(jnp.int32)


def _select(scores, cap):
    rows, t = scores.shape
    assert t % OFF_CHUNK == 0 and t // OFF_CHUNK < LANES
    return pl.pallas_call(
        functools.partial(_select_kernel, cap=cap),
        out_shape=[jax.ShapeDtypeStruct((rows, t), jnp.int32),
                   jax.ShapeDtypeStruct((rows, LANES), jnp.int32)],
        compiler_params=_params(),
        name="select",
    )(scores)


GATHER_BLOCK = 512
GATHER_WINDOW = 128


def _expert_kernel(offs_ref, slot_ref, sc_ref, h_ref, wg_ref, wu_ref, wd_ref, o_ref, xg_s, gs_s, *, cap, n_off):
    t = h_ref.shape[0]
    tk = min(GATHER_BLOCK, t)
    win = min(GATHER_WINDOW, cap)
    per_block = tk // OFF_CHUNK
    slot_iota = lax.broadcasted_iota(jnp.int32, (win, tk), 0)
    xg_s[...] = jnp.zeros_like(xg_s)
    gs_s[...] = jnp.zeros_like(gs_s)
    row0 = (pl.program_id(0) * N_EXPERTS + pl.program_id(1)) * n_off
    for j in range(t // tk):
        sl = slice(j * tk, (j + 1) * tk)
        start0 = (offs_ref[row0 + j * per_block] // SUBLANES) * SUBLANES
        last = offs_ref[row0 + (j + 1) * per_block]

        def window(c, carry, sl=sl, start0=start0):
            lo = start0 + c * win
            st = pl.multiple_of(jnp.minimum(lo, cap - win), SUBLANES)
            onehot = (slot_iota + st == slot_ref[:, sl]) & (slot_ref[:, sl] >= lo)
            xg_s[pl.ds(st, win), :] += _dot(onehot.astype(BF16), h_ref[sl, :])
            gs_s[pl.ds(st, win), :] += jnp.sum(jnp.where(onehot, sc_ref[:, sl], 0.0), axis=1, keepdims=True)
            return carry

        window(0, 0)
        lax.fori_loop(1, (last - start0 + win - 1) // win, window, 0)
    xb = xg_s[...].astype(BF16)
    gate = _dot(xb, wg_ref[...])
    hid = (gate * _sigmoid(gate)) * _dot(xb, wu_ref[...])
    y = _dot(hid.astype(BF16), wd_ref[...])
    o_ref[...] = (y * gs_s[...]).astype(BF16)


def _experts(offs, slot, scores, h2, wg, wu, wd, cap):
    b, t, d = h2.shape
    n_off = offs.shape[0] // (b * N_EXPERTS)
    row = pl.BlockSpec((None, 1, t), lambda bb, e, o: (bb * N_EXPERTS + e, 0, 0))
    grid_spec = pltpu.PrefetchScalarGridSpec(
        num_scalar_prefetch=1,
        grid=(b, N_EXPERTS),
        in_specs=[row, row,
                  pl.BlockSpec((None, t, d), lambda bb, e, o: (bb, 0, 0)),
                  pl.BlockSpec((None, d, D_EXPERT), lambda bb, e, o: (e, 0, 0)),
                  pl.BlockSpec((None, d, D_EXPERT), lambda bb, e, o: (e, 0, 0)),
                  pl.BlockSpec((None, D_EXPERT, d), lambda bb, e, o: (e, 0, 0))],
        out_specs=pl.BlockSpec((None, cap, d), lambda bb, e, o: (bb * N_EXPERTS + e, 0, 0)),
        scratch_shapes=[pltpu.VMEM((cap, d), F32), pltpu.VMEM((cap, 1), F32)],
    )
    return pl.pallas_call(
        functools.partial(_expert_kernel, cap=cap, n_off=n_off),
        grid_spec=grid_spec,
        out_shape=jax.ShapeDtypeStruct((b * N_EXPERTS, cap, d), BF16),
        compiler_params=_params("parallel", "parallel"),
        name="experts",
    )(offs, slot.reshape(b * N_EXPERTS, 1, t), scores.reshape(b * N_EXPERTS, 1, t), h2, wg, wu, wd)


COMBINE_BLOCK = 1024
SLOT_WINDOW = 256


def _combine_kernel(offs_ref, slot_ref, y_ref, x1_ref, mod_ref, g_ref, o_ref, acc_s, *, cap, n_off):
    tm = x1_ref.shape[0]
    kc = min(SLOT_WINDOW, cap)
    per_block = tm // OFF_CHUNK
    lane_iota = lax.broadcasted_iota(jnp.int32, (tm, kc), 1)

    def window(e, start0, c):
        lo = start0 + c * kc
        st = pl.multiple_of(jnp.minimum(lo, cap - kc), BF16_ROWS)
        onehot = (slot_ref[:, e:e + 1] == st + lane_iota) & (slot_ref[:, e:e + 1] >= lo)
        return _dot(onehot.astype(BF16), y_ref[pl.ds(e * cap + st, kc), :])

    starts, extras = [], []
    for e in range(N_EXPERTS):
        o0 = (pl.program_id(0) * N_EXPERTS + e) * n_off + pl.program_id(1) * per_block
        start0 = (offs_ref[o0] // BF16_ROWS) * BF16_ROWS
        starts.append(start0)
        extras.append(jnp.maximum((offs_ref[o0 + per_block] - start0 + kc - 1) // kc - 1, 0))

    acc = jnp.zeros((tm, D_MODEL), F32)
    for e in range(N_EXPERTS):
        acc = acc + window(e, starts[e], 0)

    def finish(total):
        o_ref[...] = _rmsnorm(x1_ref[...] + mod_ref[5:6, :] * total, g_ref[...])

    finish(acc)

    @pl.when(sum(extras) > 0)
    def _():
        acc_s[...] = acc
        for e in range(N_EXPERTS):
            def more(c, carry, e=e):
                acc_s[...] += window(e, starts[e], c)
                return carry
            lax.fori_loop(1, extras[e] + 1, more, 0)
        finish(acc_s[...])


def _combine(offs, slot_t, y, x1, mods, g, cap):
    b, t, d = x1.shape
    tm = min(COMBINE_BLOCK, t)
    n_off = offs.shape[0] // (b * N_EXPERTS)
    grid_spec = pltpu.PrefetchScalarGridSpec(
        num_scalar_prefetch=1,
        grid=(b, t // tm),
        in_specs=[pl.BlockSpec((None, tm, N_EXPERTS), lambda bb, i, o: (bb, i, 0)),
                  pl.BlockSpec((None, N_EXPERTS * cap, d), lambda bb, i, o: (bb, 0, 0),
                               pipeline_mode=pl.Buffered(1)),
                  pl.BlockSpec((None, tm, d), lambda bb, i, o: (bb, i, 0)),
                  pl.BlockSpec((None, SUBLANES, d), lambda bb, i, o: (bb, 0, 0)),
                  pl.BlockSpec((1, d), lambda bb, i, o: (0, 0))],
        out_specs=pl.BlockSpec((None, tm, d), lambda bb, i, o: (bb, i, 0)),
        scratch_shapes=[pltpu.VMEM((tm, d), F32)],
    )
    return pl.pallas_call(
        functools.partial(_combine_kernel, cap=cap, n_off=n_off),
        grid_spec=grid_spec,
        out_shape=jax.ShapeDtypeStruct((b, t, d), F32),
        compiler_params=_params("parallel", "parallel"),
        name="combine",
    )(offs, slot_t, y, x1, mods, g)


def kernel(x, c, ctx, c_ctx, w_ada, b_ada, norm1_g, norm2_g, w_in, w_four, conv_w, conv_b, lru_lambda, lru_wa, lru_ba, lru_wi, lru_bi, w_lru, w_out, w_router, w_gate_e, w_up_e, w_down_e, final_g):
    b, t, d = x.shape
    assert d == D_MODEL and b == SUBLANES and t % TOKEN_BLOCK == 0 and ctx.shape[1] % SCAN_CHUNK == 0
    assert w_ada.shape[0] == 1, "single-layer problem: the context stream is only read by the recurrence"
    cap = CAPACITY_FACTOR * t // N_EXPERTS
    col_x = D_FOURIER + D_RNN
    col_g = col_x + D_RNN
    col_a = col_g + D_MODEL

    pos = _pos_table(t)
    ct, st = _dft_tables(t)

    cc = jnp.concatenate([c, c_ctx[None], jnp.zeros((SUBLANES - 1, d), F32)], axis=0)
    mods = _ada(cc, w_ada[0], b_ada[0]).reshape(cc.shape[0], N_MOD, d)
    mods = jnp.pad(mods, ((0, 0), (0, SUBLANES - N_MOD), (0, 0)))

    w = w_in[0]
    wfc, wfs = _fold_channel_dft(w[:, :D_FOURIER])
    w_x = w[:, D_FOURIER:col_x].astype(BF16)
    w_cat = jnp.concatenate([w_x, w[:, col_x:].astype(BF16), wfc, wfs], axis=1)
    g1 = norm1_g[0].reshape(1, d)
    zx, zg, za, zb, zfc, zfs = _proj(x, pos, mods, None, g1, w_cat,
                                     (D_RNN, D_RNN, D_MODEL, D_MODEL, D_FOURIER, D_FOURIER))
    (zx_ctx,) = _proj(ctx, None, mods, b, g1, w_x, (D_RNN,))

    for dr in range(2):
        vecs = jnp.concatenate([lru_lambda[0, dr][None], lru_ba[0, dr][None], lru_bi[0, dr][None],
                                conv_b[0][None], conv_w[0]], axis=0)
        wg = jnp.concatenate([lru_wa[0, dr], lru_wi[0, dr]], axis=-1).astype(BF16)
        _, h_ctx = _scan(zx_ctx, vecs, wg, jnp.zeros((b, D_RNN), F32), reverse=bool(dr))
        if dr == 0:
            h_fwd, _ = _scan(zx, vecs, wg, h_ctx, reverse=False)
        else:
            gr, _ = _scan(zx, vecs, wg, h_ctx, reverse=True, hf=h_fwd, zg=zg)

    f = _dft_apply(ct, st, zfc, zfs)

    x1, h2, scores = _merge(f, gr, za, zb, x, pos, mods, norm2_g[0].reshape(1, d),
                            w_four[0].astype(BF16), w_lru[0].astype(BF16), w_out[0].astype(BF16),
                            w_router[0].T.astype(BF16))

    slot, offs = _select(scores.reshape(b * N_EXPERTS, t), cap)
    offs = offs[:, :t // OFF_CHUNK + 1].reshape(-1)
    y = _experts(offs, slot, scores, h2, w_gate_e[0].astype(BF16), w_up_e[0].astype(BF16),
                 w_down_e[0].astype(BF16), cap)
    slot_t = jnp.swapaxes(slot.reshape(b, N_EXPERTS, t), 1, 2)
    return _combine(offs, slot_t, y.reshape(b, N_EXPERTS * cap, d), x1, mods, final_g.reshape(1, d), cap)
```

```python
import functools
import math

import numpy as np
import jax
import jax.numpy as jnp
from jax import lax
from jax.experimental import pallas as pl
from jax.experimental.pallas import tpu as pltpu

F32 = jnp.float32
BF16 = jnp.bfloat16

D_MODEL = 1024
GRID_W = 64
N_FOURIER_GROUPS = 4
FOURIER_GROUP = 128
D_FOURIER = N_FOURIER_GROUPS * FOURIER_GROUP
D_RNN = D_MODEL
N_RNN_HEADS = 8
RNN_HEAD = D_RNN // N_RNN_HEADS
CONV_W = 4
CONV_LEFT = 2
LRU_C = 8.0
N_EXPERTS = 16
CAPACITY_FACTOR = 2
D_EXPERT = 1536
N_MOD = 6
EPS = 1e-6
POS_MAX_PERIOD = 10000.0

LANES = 128
SUBLANES = 8
BF16_ROWS = 16
VMEM_LIMIT = 56 * 1024 * 1024

TOKEN_BLOCK = 512
SCAN_CHUNK = 128
SCAN_STRIDE = SCAN_CHUNK + SUBLANES


def _params(*sem):
    return pltpu.CompilerParams(dimension_semantics=sem, vmem_limit_bytes=VMEM_LIMIT)


def _sigmoid(x):
    return 0.5 * jnp.tanh(0.5 * x) + 0.5


def _rmsnorm(x, g):
    return x * lax.rsqrt(jnp.mean(x * x, axis=-1, keepdims=True) + EPS) * g


def _dot(a, b):
    return jnp.dot(a, b, preferred_element_type=F32)


def _pos_kernel(o_ref):
    rows = o_ref.shape[0]
    q = D_MODEL // 4
    k = lax.broadcasted_iota(jnp.int32, (1, q), 1).astype(F32)
    freqs = jnp.exp(-math.log(POS_MAX_PERIOD) * k / q)
    ang_r = lax.broadcasted_iota(jnp.int32, (rows, q), 0).astype(F32) * freqs
    ang_c = lax.broadcasted_iota(jnp.int32, (GRID_W, q), 0).astype(F32) * freqs
    shape = (rows, GRID_W, q)
    o_ref[:, :, 0 * q:1 * q] = jnp.broadcast_to(jnp.sin(ang_r)[:, None, :], shape)
    o_ref[:, :, 1 * q:2 * q] = jnp.broadcast_to(jnp.cos(ang_r)[:, None, :], shape)
    o_ref[:, :, 2 * q:3 * q] = jnp.broadcast_to(jnp.sin(ang_c)[None, :, :], shape)
    o_ref[:, :, 3 * q:4 * q] = jnp.broadcast_to(jnp.cos(ang_c)[None, :, :], shape)


def _pos_table(n_tokens):
    rows = n_tokens // GRID_W
    out = pl.pallas_call(
        _pos_kernel,
        out_shape=jax.ShapeDtypeStruct((rows, GRID_W, D_MODEL), F32),
        name="pos_table",
    )()
    return out.reshape(n_tokens, D_MODEL)


RADIX = 8


def _dft_kernel(c_ref, s_ref, *, n):
    ns = c_ref.shape[0]
    k = RADIX * lax.broadcasted_iota(jnp.int32, (ns, ns), 0) + pl.program_id(0)
    t2 = lax.broadcasted_iota(jnp.int32, (ns, ns), 1)
    ang = ((k * t2) & (n - 1)).astype(F32) * (2.0 * math.pi / n)
    scale = 1.0 / math.sqrt(n)
    c_ref[...] = (jnp.cos(ang) * scale).astype(BF16)
    s_ref[...] = (jnp.sin(ang) * scale).astype(BF16)


def _dft_tables(n):
    assert n & (n - 1) == 0 and n % (RADIX * BF16_ROWS) == 0
    ns = n // RADIX
    return pl.pallas_call(
        functools.partial(_dft_kernel, n=n),
        grid=(RADIX,),
        out_specs=[pl.BlockSpec((None, ns, ns), lambda i: (i, 0, 0))] * 2,
        out_shape=[jax.ShapeDtypeStruct((RADIX, ns, ns), BF16)] * 2,
        compiler_params=_params("parallel"),
        name="dft_tables",
    )()


def _fold_kernel(w_ref, c_ref, s_ref, oc_ref, os_ref):
    for g in range(N_FOURIER_GROUPS):
        sl = slice(g * FOURIER_GROUP, (g + 1) * FOURIER_GROUP)
        w = w_ref[:, sl]
        oc_ref[:, sl] = jnp.dot(w, c_ref[...], preferred_element_type=F32,
                                precision=lax.Precision.HIGHEST).astype(BF16)
        os_ref[:, sl] = jnp.dot(w, s_ref[...], preferred_element_type=F32,
                                precision=lax.Precision.HIGHEST).astype(BF16)


def _fold_channel_dft(w_f):
    p = np.arange(FOURIER_GROUP)
    ang = 2.0 * np.pi * ((p[:, None] * p[None, :]) % FOURIER_GROUP) / FOURIER_GROUP
    cp = jnp.asarray(np.cos(ang) / math.sqrt(FOURIER_GROUP), F32)
    sp = jnp.asarray(np.sin(ang) / math.sqrt(FOURIER_GROUP), F32)
    return pl.pallas_call(
        _fold_kernel,
        out_shape=[jax.ShapeDtypeStruct(w_f.shape, BF16)] * 2,
        compiler_params=_params(),
        name="fold_channel_dft",
    )(w_f, cp, sp)


def _ada_kernel(c_ref, w_ref, b_ref, o_ref):
    c = c_ref[...]
    o_ref[...] = jnp.dot(c * _sigmoid(c), w_ref[...], preferred_element_type=F32,
                         precision=lax.Precision.HIGHEST) + b_ref[...]


def _ada(cc, w_ada, b_ada):
    n = w_ada.shape[1]
    tn = D_MODEL
    return pl.pallas_call(
        _ada_kernel,
        grid=(n // tn,),
        in_specs=[pl.BlockSpec(cc.shape, lambda j: (0, 0)),
                  pl.BlockSpec((D_MODEL, tn), lambda j: (0, j)),
                  pl.BlockSpec((1, tn), lambda j: (0, j))],
        out_specs=pl.BlockSpec((cc.shape[0], tn), lambda j: (0, j)),
        out_shape=jax.ShapeDtypeStruct((cc.shape[0], n), F32),
        compiler_params=_params("parallel"),
        name="ada",
    )(cc, w_ada, b_ada.reshape(1, n))


def _proj_kernel(*refs, widths, has_pos):
    if has_pos:
        x_ref, pos_ref, mod_ref, g_ref, w_ref = refs[:5]
        outs = refs[5:]
        x = x_ref[...] + pos_ref[...]
    else:
        x_ref, mod_ref, g_ref, w_ref = refs[:4]
        outs = refs[4:]
        x = x_ref[...]
    h = _rmsnorm(x, g_ref[...]) * (1.0 + mod_ref[1:2, :]) + mod_ref[0:1, :]
    hb = h.astype(BF16)
    c0 = 0
    for o_ref, wd in zip(outs, widths):
        o_ref[...] = _dot(hb, w_ref[:, c0:c0 + wd]).astype(BF16)
        c0 += wd


def _proj(x, pos, mods, mod_row, g, w, widths):
    b, t, d = x.shape
    tm = min(TOKEN_BLOCK, t)
    has_pos = pos is not None
    in_specs = [pl.BlockSpec((None, tm, d), lambda i, bb: (bb, i, 0))]
    args = [x]
    if has_pos:
        in_specs.append(pl.BlockSpec((tm, d), lambda i, bb: (i, 0)))
        args.append(pos)
    if mod_row is None:
        mod_map = lambda i, bb: (bb, 0, 0)
    else:
        mod_map = lambda i, bb: (mod_row, 0, 0)
    in_specs += [pl.BlockSpec((None, SUBLANES, d), mod_map),
                 pl.BlockSpec((1, d), lambda i, bb: (0, 0)),
                 pl.BlockSpec(w.shape, lambda i, bb: (0, 0))]
    args += [mods, g, w]
    return pl.pallas_call(
        functools.partial(_proj_kernel, widths=widths, has_pos=has_pos),
        grid=(t // tm, b),
        in_specs=in_specs,
        out_specs=[pl.BlockSpec((None, tm, wd), lambda i, bb: (bb, i, 0)) for wd in widths],
        out_shape=[jax.ShapeDtypeStruct((b, t, wd), BF16) for wd in widths],
        compiler_params=_params("parallel", "parallel"),
        name="proj",
    )(*args)


def _gelu(x):
    return 0.5 * x * (1.0 + jnp.tanh(math.sqrt(2.0 / math.pi) * (x + 0.044715 * x * x * x)))


def _conv_kernel(x_ref, xp_ref, xn_ref, vec_ref, o_ref):
    tm = x_ref.shape[0]
    edge = BF16_ROWS
    i = pl.program_id(0)
    first = i == 0
    last = i == pl.num_programs(0) - 1
    w = [vec_ref[4 + kk:5 + kk, :] for kk in range(CONV_W)]
    bias = vec_ref[3:4, :]

    def taps(xm2, xm1, x0, xp1):
        return bias + w[0] * xm2 + w[1] * xm1 + w[2] * x0 + w[3] * xp1

    x = x_ref[...].astype(F32)
    o_ref[...] = taps(pltpu.roll(x, 2, 0), pltpu.roll(x, 1, 0), x, pltpu.roll(x, tm - 1, 0)).astype(BF16)

    row = lax.broadcasted_iota(jnp.int32, (edge, x.shape[1]), 0)

    def shifted(before, cur, after, s):
        if s < 0:
            return jnp.where(row < -s, pltpu.roll(before, -s, 0), pltpu.roll(cur, -s, 0))
        return jnp.where(row < edge - s, pltpu.roll(cur, edge - s, 0), pltpu.roll(after, edge - s, 0))

    prev = jnp.where(first, 0.0, xp_ref[...].astype(F32))
    nxt = jnp.where(last, 0.0, xn_ref[...].astype(F32))
    head, head2 = x[0:edge], x[edge:2 * edge]
    tail, tail2 = x[tm - edge:tm], x[tm - 2 * edge:tm - edge]
    o_ref[0:edge, :] = taps(shifted(prev, head, head2, -2), shifted(prev, head, head2, -1), head,
                            shifted(prev, head, head2, 1)).astype(BF16)
    o_ref[tm - edge:tm, :] = taps(shifted(tail2, tail, nxt, -2), shifted(tail2, tail, nxt, -1), tail,
                                  shifted(tail2, tail, nxt, 1)).astype(BF16)


def _conv(zx, vecs):
    b, t, c = zx.shape
    tm = min(TOKEN_BLOCK, t)
    assert tm >= 2 * BF16_ROWS
    hpb = tm // BF16_ROWS
    n_halo = t // BF16_ROWS
    return pl.pallas_call(
        _conv_kernel,
        grid=(t // tm, b),
        in_specs=[pl.BlockSpec((None, tm, c), lambda i, bb: (bb, i, 0)),
                  pl.BlockSpec((None, BF16_ROWS, c), lambda i, bb: (bb, jnp.maximum(i * hpb - 1, 0), 0)),
                  pl.BlockSpec((None, BF16_ROWS, c), lambda i, bb: (bb, jnp.minimum((i + 1) * hpb, n_halo - 1), 0)),
                  pl.BlockSpec(vecs.shape, lambda i, bb: (0, 0))],
        out_specs=pl.BlockSpec((None, tm, c), lambda i, bb: (bb, i, 0)),
        out_shape=jax.ShapeDtypeStruct((b, t, c), BF16),
        compiler_params=_params("parallel", "parallel"),
        name="conv",
    )(zx, zx, zx, vecs)


def _scan_kernel(*refs, reverse, fuse):
    if fuse:
        u_ref, vec_ref, wg_ref, h0_ref, hf_ref, zg_ref, o_ref, hT_ref, a_s, b_s, hc_s = refs
    else:
        u_ref, vec_ref, wg_ref, h0_ref, o_ref, hT_ref, a_s, b_s, hc_s = refs
    nb, tc, c = u_ref.shape

    @pl.when(pl.program_id(0) == 0)
    def _():
        for hh in range(N_RNN_HEADS):
            hc_s[hh] = h0_ref[:, hh * RNN_HEAD:(hh + 1) * RNN_HEAD]

    lam = vec_ref[0:1, :]
    b_a = vec_ref[1:2, :]
    b_i = vec_ref[2:3, :]
    neg_c_sp = -LRU_C * (jnp.maximum(-lam, 0.0) + jnp.log1p(jnp.exp(-jnp.abs(lam))))

    for bb in range(nb):
        ub = u_ref[bb]
        u = ub.astype(F32)
        row0 = bb * SCAN_STRIDE
        for hh in range(N_RNN_HEADS):
            sl = slice(hh * RNN_HEAD, (hh + 1) * RNN_HEAD)
            gates = _dot(ub[:, sl], wg_ref[hh])
            r = _sigmoid(gates[:, :RNN_HEAD] + b_a[:, sl])
            ig = _sigmoid(gates[:, RNN_HEAD:] + b_i[:, sl])
            a = jnp.exp(r * neg_c_sp[:, sl])
            a_s[hh, row0:row0 + tc, :] = a
            b_s[hh, row0:row0 + tc, :] = jnp.sqrt(1.0 - a * a) * (ig * u[:, sl])

    def body(j, hs):
        t = (tc - 1 - j) if reverse else j
        idx = pl.ds(t, nb, stride=SCAN_STRIDE)
        new = []
        for hh in range(N_RNN_HEADS):
            h = a_s[hh, idx, :] * hs[hh] + b_s[hh, idx, :]
            b_s[hh, idx, :] = h
            new.append(h)
        return tuple(new)

    hs = lax.fori_loop(0, tc, body, tuple(hc_s[hh] for hh in range(N_RNN_HEADS)), unroll=8)
    for hh in range(N_RNN_HEADS):
        hc_s[hh] = hs[hh]
        hT_ref[:, hh * RNN_HEAD:(hh + 1) * RNN_HEAD] = hs[hh]

    for bb in range(nb):
        row0 = bb * SCAN_STRIDE
        for hh in range(N_RNN_HEADS):
            sl = slice(hh * RNN_HEAD, (hh + 1) * RNN_HEAD)
            hb = b_s[hh, row0:row0 + tc, :]
            if fuse:
                hb = _gelu(zg_ref[bb, :, sl].astype(F32)) * (hf_ref[bb, :, sl].astype(F32) + hb)
            o_ref[bb, :, sl] = hb.astype(BF16)


def _scan(u, vecs, wg, h0, *, reverse, hf=None, zg=None):
    nb, t, c = u.shape
    tc = SCAN_CHUNK
    n_chunks = t // tc
    fuse = hf is not None

    def cidx(i):
        return (n_chunks - 1 - i) if reverse else i

    main_spec = pl.BlockSpec((nb, tc, c), lambda i: (0, cidx(i), 0))
    in_specs = [
        main_spec,
        pl.BlockSpec(vecs.shape, lambda i: (0, 0)),
        pl.BlockSpec(wg.shape, lambda i: (0, 0, 0)),
        pl.BlockSpec(h0.shape, lambda i: (0, 0)),
    ]
    args = [u, vecs, wg, h0]
    if fuse:
        in_specs += [main_spec, main_spec]
        args += [hf, zg]
    return pl.pallas_call(
        functools.partial(_scan_kernel, reverse=reverse, fuse=fuse),
        grid=(n_chunks,),
        in_specs=in_specs,
        out_specs=[main_spec, pl.BlockSpec((nb, c), lambda i: (0, 0))],
        out_shape=[jax.ShapeDtypeStruct((nb, t, c), BF16), jax.ShapeDtypeStruct((nb, c), F32)],
        scratch_shapes=[pltpu.VMEM((N_RNN_HEADS, nb * SCAN_STRIDE, RNN_HEAD), F32),
                        pltpu.VMEM((N_RNN_HEADS, nb * SCAN_STRIDE, RNN_HEAD), F32),
                        pltpu.VMEM((N_RNN_HEADS, nb, RNN_HEAD), F32)],
        compiler_params=_params("arbitrary"),
        name="scan_bwd" if reverse else "scan_fwd",
    )(*args)


DFT_ROWS = 64
DFT_COLS = 256


def _dftmm_kernel(zc_ref, zs_ref, c_ref, s_ref, o_ref, ar_s, ai_s, o_s):
    t, f = zc_ref.shape
    ns = t // RADIX
    rb = min(DFT_ROWS, ns)
    r2 = math.sqrt(0.5)

    def dft4(x):
        s0 = (x[0][0] + x[2][0], x[0][1] + x[2][1])
        s1 = (x[0][0] - x[2][0], x[0][1] - x[2][1])
        s2 = (x[1][0] + x[3][0], x[1][1] + x[3][1])
        s3 = (x[1][0] - x[3][0], x[1][1] - x[3][1])
        return [(s0[0] + s2[0], s0[1] + s2[1]), (s1[0] + s3[1], s1[1] - s3[0]),
                (s0[0] - s2[0], s0[1] - s2[1]), (s1[0] - s3[1], s1[1] + s3[0])]

    def butterfly(i, carry):
        r0 = pl.multiple_of(i * rb, rb)
        w = [(zc_ref[pl.ds(j * ns + r0, rb), :].astype(F32), -zs_ref[pl.ds(j * ns + r0, rb), :].astype(F32))
             for j in range(RADIX)]
        u = [(w[j][0] + w[j + 4][0], w[j][1] + w[j + 4][1]) for j in range(4)]
        v = [(w[j][0] - w[j + 4][0], w[j][1] - w[j + 4][1]) for j in range(4)]
        v = [v[0],
             ((v[1][0] + v[1][1]) * r2, (v[1][1] - v[1][0]) * r2),
             (v[2][1], -v[2][0]),
             ((v[3][1] - v[3][0]) * r2, (-v[3][0] - v[3][1]) * r2)]
        even = dft4(u)
        odd = dft4(v)
        for m in range(4):
            for k1, a in ((2 * m, even[m]), (2 * m + 1, odd[m])):
                ar_s[k1, pl.ds(r0, rb), :] = a[0].astype(BF16)
                ai_s[k1, pl.ds(r0, rb), :] = a[1].astype(BF16)
        return carry

    lax.fori_loop(0, ns // rb, butterfly, 0)

    for c0 in range(0, f, DFT_COLS):
        for k1 in range(RADIX):
            res = (_dot(c_ref[k1], ar_s[k1, :, c0:c0 + DFT_COLS])
                   + _dot(s_ref[k1], ai_s[k1, :, c0:c0 + DFT_COLS]))
            for j in range(DFT_COLS // LANES):
                o_s[j, pl.ds(k1, ns, stride=RADIX), :] = res[:, j * LANES:(j + 1) * LANES]
        for j in range(DFT_COLS // LANES):
            o_ref[:, c0 + j * LANES:c0 + (j + 1) * LANES] = o_s[j].astype(BF16)


def _dft_apply(ct, st, zfc, zfs):
    b, t, f = zfc.shape
    ns = t // RADIX
    table = pl.BlockSpec((RADIX, ns, ns), lambda bb: (0, 0, 0), pipeline_mode=pl.Buffered(1))
    return pl.pallas_call(
        _dftmm_kernel,
        grid=(b,),
        in_specs=[pl.BlockSpec((None, t, f), lambda bb: (bb, 0, 0)),
                  pl.BlockSpec((None, t, f), lambda bb: (bb, 0, 0)),
                  table, table],
        out_specs=pl.BlockSpec((None, t, f), lambda bb: (bb, 0, 0)),
        out_shape=jax.ShapeDtypeStruct((b, t, f), BF16),
        scratch_shapes=[pltpu.VMEM((RADIX, ns, f), BF16),
                        pltpu.VMEM((RADIX, ns, f), BF16),
                        pltpu.VMEM((DFT_COLS // LANES, t, LANES), F32)],
        compiler_params=_params("parallel"),
        name="dft_apply",
    )(zfc, zfs, ct, st)


def _merge_kernel(f_ref, gr_ref, za_ref, zb_ref, x_ref, pos_ref, mod_ref, g2_ref,
                  wf_ref, wl_ref, wo_ref, wr_ref, x1_ref, h2_ref, sc_ref, scr_ref):
    y_four = _dot(f_ref[...], wf_ref[...])
    y_rnn = _dot(gr_ref[...], wl_ref[...])
    m = _sigmoid(za_ref[...].astype(F32)) * y_four + _sigmoid(zb_ref[...].astype(F32)) * y_rnn
    y = _dot(m.astype(BF16), wo_ref[...])
    x1 = (x_ref[...] + pos_ref[...]) + mod_ref[2:3, :] * y
    x1_ref[...] = x1
    h2 = (_rmsnorm(x1, g2_ref[...]) * (1.0 + mod_ref[4:5, :]) + mod_ref[3:4, :]).astype(BF16)
    h2_ref[...] = h2
    logits = lax.dot_general(wr_ref[...], h2, (((1,), (1,)), ((), ())), preferred_element_type=F32)
    e = jnp.exp(logits - jnp.max(logits, axis=0, keepdims=True))
    sc = e / jnp.sum(e, axis=0, keepdims=True)
    sc_ref[...] = sc
    hi = sc.astype(BF16).astype(F32)
    lo = sc - hi
    tm = sc.shape[1]
    row = lax.broadcasted_iota(jnp.int32, (BF16_ROWS, tm), 0)
    for ex in range(N_EXPERTS):
        tile = jnp.where(row == 0, hi[ex:ex + 1, :], jnp.where(row == 1, lo[ex:ex + 1, :], 0.0))
        scr_ref[ex] = tile.astype(BF16)


def _merge(f, gr, za, zb, x, pos, mods, g2, w_four, w_lru, w_out, w_rt):
    b, t, d = x.shape
    tm = min(TOKEN_BLOCK, t)
    tok = lambda wd: pl.BlockSpec((None, tm, wd), lambda i, bb: (bb, i, 0))
    full = lambda a: pl.BlockSpec(a.shape, lambda i, bb: (0,) * a.ndim)
    return pl.pallas_call(
        _merge_kernel,
        grid=(t // tm, b),
        in_specs=[tok(D_FOURIER), tok(d), tok(d), tok(d), tok(d),
                  pl.BlockSpec((tm, d), lambda i, bb: (i, 0)),
                  pl.BlockSpec((None, SUBLANES, d), lambda i, bb: (bb, 0, 0)),
                  full(g2), full(w_four), full(w_lru), full(w_out), full(w_rt)],
        out_specs=[tok(d), tok(d), pl.BlockSpec((None, N_EXPERTS, tm), lambda i, bb: (bb, 0, i)),
                   pl.BlockSpec((None, N_EXPERTS, BF16_ROWS, tm), lambda i, bb: (bb, 0, 0, i))],
        out_shape=[jax.ShapeDtypeStruct((b, t, d), F32),
                   jax.ShapeDtypeStruct((b, t, d), BF16),
                   jax.ShapeDtypeStruct((b, N_EXPERTS, t), F32),
                   jax.ShapeDtypeStruct((b, N_EXPERTS, BF16_ROWS, t), BF16)],
        compiler_params=_params("parallel", "parallel"),
        name="merge",
    )(f, gr, za, zb, x, pos, mods, g2, w_four, w_lru, w_out, w_rt)


MAX_EXP = 127
EXP_STEPS = 7
VAL_STEPS = 56


OFF_CHUNK = 256


def _select_kernel(s_ref, slot_ref, offs_ref, *, cap):
    rows, t = s_ref.shape

    def count(mask):
        return jnp.sum(jnp.where(mask, 1.0, 0.0), axis=1, keepdims=True)

    def exp_step(_, carry):
        k_lo, k_hi = carry
        k_mid = jnp.floor(0.5 * (k_lo + k_hi))
        ok = count(s_ref[...] >= jnp.exp2(-k_mid)) >= cap
        return jnp.where(ok, k_lo, k_mid), jnp.where(ok, k_mid, k_hi)

    k_lo, k_hi = lax.fori_loop(
        0, EXP_STEPS, exp_step,
        (jnp.full((rows, 1), -1.0, F32), jnp.full((rows, 1), float(MAX_EXP), F32)))
    lo0 = jnp.where(k_hi >= MAX_EXP, 0.0, jnp.exp2(-k_hi))
    hi0 = jnp.exp2(-k_lo)

    def val_step(_, carry):
        lo, hi = carry
        mid = 0.5 * (lo + hi)
        ok = count(s_ref[...] >= mid) >= cap
        return jnp.where(ok, mid, lo), jnp.where(ok, hi, mid)

    lo, hi = lax.fori_loop(0, VAL_STEPS, val_step, (lo0, hi0))
    s = s_ref[...]
    gt = s >= hi
    eq = (s >= lo) & (s < hi)
    need = cap - count(gt)

    p = lax.broadcasted_iota(jnp.int32, (LANES, LANES), 0)
    q = lax.broadcasted_iota(jnp.int32, (LANES, LANES), 1)
    tri = (p < q).astype(BF16)
    run_gt = jnp.zeros((rows, 1), F32)
    run_eq = jnp.zeros((rows, 1), F32)
    off_lane = lax.broadcasted_iota(jnp.int32, (rows, LANES), 1)
    offs = jnp.zeros((rows, LANES), F32)
    for j in range(t // LANES):
        if (j * LANES) % OFF_CHUNK == 0:
            offs = jnp.where(off_lane == (j * LANES) // OFF_CHUNK, run_gt + jnp.minimum(run_eq, need), offs)
        sl = slice(j * LANES, (j + 1) * LANES)
        g = gt[:, sl]
        q_ = eq[:, sl]
        pg = _dot(g.astype(BF16), tri) + run_gt
        pe = _dot(q_.astype(BF16), tri) + run_eq
        sel = g | (q_ & (pe < need))
        pos = pg + jnp.minimum(pe, need)
        slot_ref[:, sl] = jnp.where(sel, pos, -1.0).astype(jnp.int32)
        run_gt = run_gt + count(g)
        run_eq = run_eq + count(q_)
    offs = jnp.where(off_lane == t // OFF_CHUNK, run_gt + jnp.minimum(run_eq, need), offs)
    offs_ref[...] = offs.astype(jnp.int32)


def _select(scores, cap):
    rows, t = scores.shape
    assert t % OFF_CHUNK == 0 and t // OFF_CHUNK < LANES
    return pl.pallas_call(
        functools.partial(_select_kernel, cap=cap),
        out_shape=[jax.ShapeDtypeStruct((rows, t), jnp.int32),
                   jax.ShapeDtypeStruct((rows, LANES), jnp.int32)],
        compiler_params=_params(),
        name="select",
    )(scores)


GATHER_BLOCK = 512
GATHER_WINDOW = 128


def _expert_kernel(offs_ref, slot_ref, sc_ref, h_ref, wg_ref, wu_ref, wd_ref, o_ref, xg_s, gs_s, *, cap, n_off):
    t = h_ref.shape[0]
    tk = min(GATHER_BLOCK, t)
    win = min(GATHER_WINDOW, cap)
    per_block = tk // OFF_CHUNK
    slot_iota = lax.broadcasted_iota(jnp.int32, (win, tk), 0)
    xg_s[...] = jnp.zeros_like(xg_s)
    gs_s[...] = jnp.zeros_like(gs_s)
    row0 = (pl.program_id(0) * N_EXPERTS + pl.program_id(1)) * n_off

    def gather(sl, lo):
        onehot = (slot_iota + lo == slot_ref[:, sl]).astype(BF16)
        rows = _dot(onehot, h_ref[sl, :])
        gates = lax.dot_general(onehot, sc_ref[:, sl], (((1,), (1,)), ((), ())), preferred_element_type=F32)
        return rows, gates

    for j in range(t // tk):
        sl = slice(j * tk, (j + 1) * tk)
        st = pl.multiple_of((offs_ref[row0 + j * per_block] // SUBLANES) * SUBLANES, SUBLANES)
        last = offs_ref[row0 + (j + 1) * per_block]
        rows, gates = gather(sl, st)
        xg_s[pl.ds(st, SUBLANES), :] += rows[:SUBLANES]
        xg_s[pl.ds(st + SUBLANES, win - SUBLANES), :] = rows[SUBLANES:]
        gs_s[pl.ds(st, win), :] += gates

        def more(c, carry, sl=sl, st=st):
            lo = pl.multiple_of(st + c * win, SUBLANES)
            rows, gates = gather(sl, lo)
            xg_s[pl.ds(lo, win), :] = rows
            gs_s[pl.ds(lo, win), :] += gates
            return carry

        lax.fori_loop(1, (last - st + win - 1) // win, more, 0)
    xb = xg_s[0:cap, :].astype(BF16)
    gate = _dot(xb, wg_ref[...])
    hid = (gate * _sigmoid(gate)) * _dot(xb, wu_ref[...])
    y = _dot(hid.astype(BF16), wd_ref[...])
    o_ref[...] = (y * (gs_s[0:cap, 0:1] + gs_s[0:cap, 1:2])).astype(BF16)


def _experts(offs, slot, score_rows, h2, wg, wu, wd, cap):
    b, t, d = h2.shape
    n_off = offs.shape[0] // (b * N_EXPERTS)
    win = min(GATHER_WINDOW, cap)
    grid_spec = pltpu.PrefetchScalarGridSpec(
        num_scalar_prefetch=1,
        grid=(b, N_EXPERTS),
        in_specs=[pl.BlockSpec((None, 1, t), lambda bb, e, o: (bb * N_EXPERTS + e, 0, 0)),
                  pl.BlockSpec((None, None, BF16_ROWS, t), lambda bb, e, o: (bb, e, 0, 0)),
                  pl.BlockSpec((None, t, d), lambda bb, e, o: (bb, 0, 0)),
                  pl.BlockSpec((None, d, D_EXPERT), lambda bb, e, o: (e, 0, 0)),
                  pl.BlockSpec((None, d, D_EXPERT), lambda bb, e, o: (e, 0, 0)),
                  pl.BlockSpec((None, D_EXPERT, d), lambda bb, e, o: (e, 0, 0))],
        out_specs=pl.BlockSpec((None, cap, d), lambda bb, e, o: (bb * N_EXPERTS + e, 0, 0)),
        scratch_shapes=[pltpu.VMEM((cap + win, d), F32), pltpu.VMEM((cap + win, BF16_ROWS), F32)],
    )
    return pl.pallas_call(
        functools.partial(_expert_kernel, cap=cap, n_off=n_off),
        grid_spec=grid_spec,
        out_shape=jax.ShapeDtypeStruct((b * N_EXPERTS, cap, d), BF16),
        compiler_params=_params("parallel", "parallel"),
        name="experts",
    )(offs, slot.reshape(b * N_EXPERTS, 1, t), score_rows, h2, wg, wu, wd)


COMBINE_BLOCK = 1024
SLOT_WINDOW = 256


def _combine_kernel(offs_ref, slot_ref, y_ref, x1_ref, mod_ref, g_ref, o_ref, acc_s, *, cap, n_off):
    tm = x1_ref.shape[0]
    kc = min(SLOT_WINDOW, cap)
    per_block = tm // OFF_CHUNK
    lane_iota = lax.broadcasted_iota(jnp.int32, (tm, kc), 1)

    def window(e, start0, c):
        lo = start0 + c * kc
        st = pl.multiple_of(jnp.minimum(lo, cap - kc), BF16_ROWS)
        onehot = (slot_ref[:, e:e + 1] == st + lane_iota) & (slot_ref[:, e:e + 1] >= lo)
        return _dot(onehot.astype(BF16), y_ref[pl.ds(e * cap + st, kc), :])

    starts, extras = [], []
    for e in range(N_EXPERTS):
        o0 = (pl.program_id(0) * N_EXPERTS + e) * n_off + pl.program_id(1) * per_block
        start0 = (offs_ref[o0] // BF16_ROWS) * BF16_ROWS
        starts.append(start0)
        extras.append(jnp.maximum((offs_ref[o0 + per_block] - start0 + kc - 1) // kc - 1, 0))

    acc = jnp.zeros((tm, D_MODEL), F32)
    for e in range(N_EXPERTS):
        acc = acc + window(e, starts[e], 0)

    def finish(total):
        o_ref[...] = _rmsnorm(x1_ref[...] + mod_ref[5:6, :] * total, g_ref[...])

    finish(acc)

    @pl.when(sum(extras) > 0)
    def _():
        acc_s[...] = acc
        for e in range(N_EXPERTS):
            def more(c, carry, e=e):
                acc_s[...] += window(e, starts[e], c)
                return carry
            lax.fori_loop(1, extras[e] + 1, more, 0)
        finish(acc_s[...])


def _combine(offs, slot_t, y, x1, mods, g, cap):
    b, t, d = x1.shape
    tm = min(COMBINE_BLOCK, t)
    n_off = offs.shape[0] // (b * N_EXPERTS)
    grid_spec = pltpu.PrefetchScalarGridSpec(
        num_scalar_prefetch=1,
        grid=(b, t // tm),
        in_specs=[pl.BlockSpec((None, tm, N_EXPERTS), lambda bb, i, o: (bb, i, 0)),
                  pl.BlockSpec((None, N_EXPERTS * cap, d), lambda bb, i, o: (bb, 0, 0),
                               pipeline_mode=pl.Buffered(1)),
                  pl.BlockSpec((None, tm, d), lambda bb, i, o: (bb, i, 0)),
                  pl.BlockSpec((None, SUBLANES, d), lambda bb, i, o: (bb, 0, 0)),
                  pl.BlockSpec((1, d), lambda bb, i, o: (0, 0))],
        out_specs=pl.BlockSpec((None, tm, d), lambda bb, i, o: (bb, i, 0)),
        scratch_shapes=[pltpu.VMEM((tm, d), F32)],
    )
    return pl.pallas_call(
        functools.partial(_combine_kernel, cap=cap, n_off=n_off),
        grid_spec=grid_spec,
        out_shape=jax.ShapeDtypeStruct((b, t, d), F32),
        compiler_params=_params("parallel", "parallel"),
        name="combine",
    )(offs, slot_t, y, x1, mods, g)


def kernel(x, c, ctx, c_ctx, w_ada, b_ada, norm1_g, norm2_g, w_in, w_four, conv_w, conv_b, lru_lambda, lru_wa, lru_ba, lru_wi, lru_bi, w_lru, w_out, w_router, w_gate_e, w_up_e, w_down_e, final_g):
    b, t, d = x.shape
    assert d == D_MODEL and b == SUBLANES and t % TOKEN_BLOCK == 0 and ctx.shape[1] % SCAN_CHUNK == 0
    assert w_ada.shape[0] == 1, "single-layer problem: the context stream is only read by the recurrence"
    cap = CAPACITY_FACTOR * t // N_EXPERTS
    col_x = D_FOURIER + D_RNN
    col_g = col_x + D_RNN
    col_a = col_g + D_MODEL

    pos = _pos_table(t)
    ct, st = _dft_tables(t)

    cc = jnp.concatenate([c, c_ctx[None], jnp.zeros((SUBLANES - 1, d), F32)], axis=0)
    mods = _ada(cc, w_ada[0], b_ada[0]).reshape(cc.shape[0], N_MOD, d)
    mods = jnp.pad(mods, ((0, 0), (0, SUBLANES - N_MOD), (0, 0)))

    w = w_in[0]
    wfc, wfs = _fold_channel_dft(w[:, :D_FOURIER])
    w_x = w[:, D_FOURIER:col_x].astype(BF16)
    w_cat = jnp.concatenate([w_x, w[:, col_x:].astype(BF16), wfc, wfs], axis=1)
    g1 = norm1_g[0].reshape(1, d)
    zx, zg, za, zb, zfc, zfs = _proj(x, pos, mods, None, g1, w_cat,
                                     (D_RNN, D_RNN, D_MODEL, D_MODEL, D_FOURIER, D_FOURIER))
    (zx_ctx,) = _proj(ctx, None, mods, b, g1, w_x, (D_RNN,))

    for dr in range(2):
        vecs = jnp.concatenate([lru_lambda[0, dr][None], lru_ba[0, dr][None], lru_bi[0, dr][None],
                                conv_b[0][None], conv_w[0]], axis=0)
        wg = jnp.concatenate([lru_wa[0, dr], lru_wi[0, dr]], axis=-1).astype(BF16)
        if dr == 0:
            u_lat = _conv(zx, vecs)
            u_ctx = _conv(zx_ctx, vecs)
        _, h_ctx = _scan(u_ctx, vecs, wg, jnp.zeros((b, D_RNN), F32), reverse=bool(dr))
        if dr == 0:
            h_fwd, _ = _scan(u_lat, vecs, wg, h_ctx, reverse=False)
        else:
            gr, _ = _scan(u_lat, vecs, wg, h_ctx, reverse=True, hf=h_fwd, zg=zg)

    f = _dft_apply(ct, st, zfc, zfs)

    x1, h2, scores, score_rows = _merge(f, gr, za, zb, x, pos, mods, norm2_g[0].reshape(1, d),
                                        w_four[0].astype(BF16), w_lru[0].astype(BF16), w_out[0].astype(BF16),
                                        w_router[0].T.astype(BF16))

    slot, offs = _select(scores.reshape(b * N_EXPERTS, t), cap)
    offs = offs[:, :t // OFF_CHUNK + 1].reshape(-1)
    y = _experts(offs, slot, score_rows, h2, w_gate_e[0].astype(BF16), w_up_e[0].astype(BF16),
                 w_down_e[0].astype(BF16), cap)
    slot_t = jnp.swapaxes(slot.reshape(b, N_EXPERTS, t), 1, 2)
    return _combine(offs, slot_t, y.reshape(b, N_EXPERTS * cap, d), x1, mods, final_g.reshape(1, d), cap)
```

```python
import functools
import math

import numpy as np
import jax
import jax.numpy as jnp
from jax import lax
from jax.experimental import pallas as pl
from jax.experimental.pallas import tpu as pltpu

F32 = jnp.float32
BF16 = jnp.bfloat16

D_MODEL = 1024
GRID_W = 64
N_FOURIER_GROUPS = 4
FOURIER_GROUP = 128
D_FOURIER = N_FOURIER_GROUPS * FOURIER_GROUP
D_RNN = D_MODEL
N_RNN_HEADS = 8
RNN_HEAD = D_RNN // N_RNN_HEADS
CONV_W = 4
CONV_LEFT = 2
LRU_C = 8.0
N_EXPERTS = 16
CAPACITY_FACTOR = 2
D_EXPERT = 1536
N_MOD = 6
EPS = 1e-6
POS_MAX_PERIOD = 10000.0

LANES = 128
SUBLANES = 8
BF16_ROWS = 16
VMEM_LIMIT = 56 * 1024 * 1024

TOKEN_BLOCK = 512
SCAN_CHUNK = 128
SCAN_STRIDE = SCAN_CHUNK + SUBLANES


def _params(*sem):
    return pltpu.CompilerParams(dimension_semantics=sem, vmem_limit_bytes=VMEM_LIMIT)


def _sigmoid(x):
    return 0.5 * jnp.tanh(0.5 * x) + 0.5


def _rmsnorm(x, g):
    return x * lax.rsqrt(jnp.mean(x * x, axis=-1, keepdims=True) + EPS) * g


def _dot(a, b):
    return jnp.dot(a, b, preferred_element_type=F32)


def _pos_kernel(o_ref):
    rows = o_ref.shape[0]
    q = D_MODEL // 4
    k = lax.broadcasted_iota(jnp.int32, (1, q), 1).astype(F32)
    freqs = jnp.exp(-math.log(POS_MAX_PERIOD) * k / q)
    ang_r = lax.broadcasted_iota(jnp.int32, (rows, q), 0).astype(F32) * freqs
    ang_c = lax.broadcasted_iota(jnp.int32, (GRID_W, q), 0).astype(F32) * freqs
    shape = (rows, GRID_W, q)
    o_ref[:, :, 0 * q:1 * q] = jnp.broadcast_to(jnp.sin(ang_r)[:, None, :], shape)
    o_ref[:, :, 1 * q:2 * q] = jnp.broadcast_to(jnp.cos(ang_r)[:, None, :], shape)
    o_ref[:, :, 2 * q:3 * q] = jnp.broadcast_to(jnp.sin(ang_c)[None, :, :], shape)
    o_ref[:, :, 3 * q:4 * q] = jnp.broadcast_to(jnp.cos(ang_c)[None, :, :], shape)


def _pos_table(n_tokens):
    rows = n_tokens // GRID_W
    out = pl.pallas_call(
        _pos_kernel,
        out_shape=jax.ShapeDtypeStruct((rows, GRID_W, D_MODEL), F32),
        name="pos_table",
    )()
    return out.reshape(n_tokens, D_MODEL)


RADIX = 8


def _dft_kernel(c_ref, s_ref, *, n):
    ns = c_ref.shape[0]
    k = RADIX * lax.broadcasted_iota(jnp.int32, (ns, ns), 0) + pl.program_id(0)
    t2 = lax.broadcasted_iota(jnp.int32, (ns, ns), 1)
    ang = ((k * t2) & (n - 1)).astype(F32) * (2.0 * math.pi / n)
    scale = 1.0 / math.sqrt(n)
    c_ref[...] = (jnp.cos(ang) * scale).astype(BF16)
    s_ref[...] = (jnp.sin(ang) * scale).astype(BF16)


def _dft_tables(n):
    assert n & (n - 1) == 0 and n % (RADIX * BF16_ROWS) == 0
    ns = n // RADIX
    return pl.pallas_call(
        functools.partial(_dft_kernel, n=n),
        grid=(RADIX,),
        out_specs=[pl.BlockSpec((None, ns, ns), lambda i: (i, 0, 0))] * 2,
        out_shape=[jax.ShapeDtypeStruct((RADIX, ns, ns), BF16)] * 2,
        compiler_params=_params("parallel"),
        name="dft_tables",
    )()


def _fold_kernel(w_ref, c_ref, s_ref, oc_ref, os_ref):
    for g in range(N_FOURIER_GROUPS):
        sl = slice(g * FOURIER_GROUP, (g + 1) * FOURIER_GROUP)
        w = w_ref[:, sl]
        oc_ref[:, sl] = jnp.dot(w, c_ref[...], preferred_element_type=F32,
                                precision=lax.Precision.HIGHEST).astype(BF16)
        os_ref[:, sl] = jnp.dot(w, s_ref[...], preferred_element_type=F32,
                                precision=lax.Precision.HIGHEST).astype(BF16)


def _fold_channel_dft(w_f):
    p = np.arange(FOURIER_GROUP)
    ang = 2.0 * np.pi * ((p[:, None] * p[None, :]) % FOURIER_GROUP) / FOURIER_GROUP
    cp = jnp.asarray(np.cos(ang) / math.sqrt(FOURIER_GROUP), F32)
    sp = jnp.asarray(np.sin(ang) / math.sqrt(FOURIER_GROUP), F32)
    return pl.pallas_call(
        _fold_kernel,
        out_shape=[jax.ShapeDtypeStruct(w_f.shape, BF16)] * 2,
        compiler_params=_params(),
        name="fold_channel_dft",
    )(w_f, cp, sp)


def _ada_kernel(c_ref, w_ref, b_ref, o_ref):
    c = c_ref[...]
    o_ref[...] = jnp.dot(c * _sigmoid(c), w_ref[...], preferred_element_type=F32,
                         precision=lax.Precision.HIGHEST) + b_ref[...]


def _ada(cc, w_ada, b_ada):
    n = w_ada.shape[1]
    tn = D_MODEL
    return pl.pallas_call(
        _ada_kernel,
        grid=(n // tn,),
        in_specs=[pl.BlockSpec(cc.shape, lambda j: (0, 0)),
                  pl.BlockSpec((D_MODEL, tn), lambda j: (0, j)),
                  pl.BlockSpec((1, tn), lambda j: (0, j))],
        out_specs=pl.BlockSpec((cc.shape[0], tn), lambda j: (0, j)),
        out_shape=jax.ShapeDtypeStruct((cc.shape[0], n), F32),
        compiler_params=_params("parallel"),
        name="ada",
    )(cc, w_ada, b_ada.reshape(1, n))


def _proj_kernel(*refs, widths, has_pos):
    if has_pos:
        x_ref, pos_ref, mod_ref, g_ref, w_ref = refs[:5]
        outs = refs[5:]
        x = x_ref[...] + pos_ref[...]
    else:
        x_ref, mod_ref, g_ref, w_ref = refs[:4]
        outs = refs[4:]
        x = x_ref[...]
    h = _rmsnorm(x, g_ref[...]) * (1.0 + mod_ref[1:2, :]) + mod_ref[0:1, :]
    hb = h.astype(BF16)
    c0 = 0
    for o_ref, wd in zip(outs, widths):
        o_ref[...] = _dot(hb, w_ref[:, c0:c0 + wd]).astype(BF16)
        c0 += wd


def _proj(x, pos, mods, mod_row, g, w, widths):
    b, t, d = x.shape
    tm = min(TOKEN_BLOCK, t)
    has_pos = pos is not None
    in_specs = [pl.BlockSpec((None, tm, d), lambda i, bb: (bb, i, 0))]
    args = [x]
    if has_pos:
        in_specs.append(pl.BlockSpec((tm, d), lambda i, bb: (i, 0)))
        args.append(pos)
    if mod_row is None:
        mod_map = lambda i, bb: (bb, 0, 0)
    else:
        mod_map = lambda i, bb: (mod_row, 0, 0)
    in_specs += [pl.BlockSpec((None, SUBLANES, d), mod_map),
                 pl.BlockSpec((1, d), lambda i, bb: (0, 0)),
                 pl.BlockSpec(w.shape, lambda i, bb: (0, 0))]
    args += [mods, g, w]
    return pl.pallas_call(
        functools.partial(_proj_kernel, widths=widths, has_pos=has_pos),
        grid=(t // tm, b),
        in_specs=in_specs,
        out_specs=[pl.BlockSpec((None, tm, wd), lambda i, bb: (bb, i, 0)) for wd in widths],
        out_shape=[jax.ShapeDtypeStruct((b, t, wd), BF16) for wd in widths],
        compiler_params=_params("parallel", "parallel"),
        name="proj",
    )(*args)


def _gelu(x):
    return 0.5 * x * (1.0 + jnp.tanh(math.sqrt(2.0 / math.pi) * (x + 0.044715 * x * x * x)))


def _conv_kernel(x_ref, xp_ref, xn_ref, vec_ref, o_ref):
    tm = x_ref.shape[0]
    edge = BF16_ROWS
    i = pl.program_id(0)
    first = i == 0
    last = i == pl.num_programs(0) - 1
    w = [vec_ref[4 + kk:5 + kk, :] for kk in range(CONV_W)]
    bias = vec_ref[3:4, :]

    def taps(xm2, xm1, x0, xp1):
        return bias + w[0] * xm2 + w[1] * xm1 + w[2] * x0 + w[3] * xp1

    x = x_ref[...].astype(F32)
    o_ref[...] = taps(pltpu.roll(x, 2, 0), pltpu.roll(x, 1, 0), x, pltpu.roll(x, tm - 1, 0)).astype(BF16)

    row = lax.broadcasted_iota(jnp.int32, (edge, x.shape[1]), 0)

    def shifted(before, cur, after, s):
        if s < 0:
            return jnp.where(row < -s, pltpu.roll(before, -s, 0), pltpu.roll(cur, -s, 0))
        return jnp.where(row < edge - s, pltpu.roll(cur, edge - s, 0), pltpu.roll(after, edge - s, 0))

    prev = jnp.where(first, 0.0, xp_ref[...].astype(F32))
    nxt = jnp.where(last, 0.0, xn_ref[...].astype(F32))
    head, head2 = x[0:edge], x[edge:2 * edge]
    tail, tail2 = x[tm - edge:tm], x[tm - 2 * edge:tm - edge]
    o_ref[0:edge, :] = taps(shifted(prev, head, head2, -2), shifted(prev, head, head2, -1), head,
                            shifted(prev, head, head2, 1)).astype(BF16)
    o_ref[tm - edge:tm, :] = taps(shifted(tail2, tail, nxt, -2), shifted(tail2, tail, nxt, -1), tail,
                                  shifted(tail2, tail, nxt, 1)).astype(BF16)


def _conv(zx, vecs):
    b, t, c = zx.shape
    tm = min(TOKEN_BLOCK, t)
    assert tm >= 2 * BF16_ROWS
    hpb = tm // BF16_ROWS
    n_halo = t // BF16_ROWS
    return pl.pallas_call(
        _conv_kernel,
        grid=(t // tm, b),
        in_specs=[pl.BlockSpec((None, tm, c), lambda i, bb: (bb, i, 0)),
                  pl.BlockSpec((None, BF16_ROWS, c), lambda i, bb: (bb, jnp.maximum(i * hpb - 1, 0), 0)),
                  pl.BlockSpec((None, BF16_ROWS, c), lambda i, bb: (bb, jnp.minimum((i + 1) * hpb, n_halo - 1), 0)),
                  pl.BlockSpec(vecs.shape, lambda i, bb: (0, 0))],
        out_specs=pl.BlockSpec((None, tm, c), lambda i, bb: (bb, i, 0)),
        out_shape=jax.ShapeDtypeStruct((b, t, c), BF16),
        compiler_params=_params("parallel", "parallel"),
        name="conv",
    )(zx, zx, zx, vecs)


def _scan_kernel(*refs, reverse, fuse):
    if fuse:
        u_ref, vec_ref, wg_ref, h0_ref, hf_ref, zg_ref, o_ref, hT_ref, a_s, b_s, hc_s = refs
    else:
        u_ref, vec_ref, wg_ref, h0_ref, o_ref, hT_ref, a_s, b_s, hc_s = refs
    nb, tc, c = u_ref.shape

    @pl.when(pl.program_id(0) == 0)
    def _():
        for hh in range(N_RNN_HEADS):
            hc_s[hh] = h0_ref[:, hh * RNN_HEAD:(hh + 1) * RNN_HEAD]

    lam = vec_ref[0:1, :]
    b_a = vec_ref[1:2, :]
    b_i = vec_ref[2:3, :]
    neg_c_sp = -LRU_C * (jnp.maximum(-lam, 0.0) + jnp.log1p(jnp.exp(-jnp.abs(lam))))

    for bb in range(nb):
        ub = u_ref[bb]
        u = ub.astype(F32)
        row0 = bb * SCAN_STRIDE
        for hh in range(N_RNN_HEADS):
            sl = slice(hh * RNN_HEAD, (hh + 1) * RNN_HEAD)
            gates = _dot(ub[:, sl], wg_ref[hh])
            r = _sigmoid(gates[:, :RNN_HEAD] + b_a[:, sl])
            ig = _sigmoid(gates[:, RNN_HEAD:] + b_i[:, sl])
            a = jnp.exp(r * neg_c_sp[:, sl])
            a_s[hh, row0:row0 + tc, :] = a
            b_s[hh, row0:row0 + tc, :] = jnp.exp(0.5 * jnp.log(1.0 - a * a)) * (ig * u[:, sl])

    def body(j, hs):
        t = (tc - 1 - j) if reverse else j
        idx = pl.ds(t, nb, stride=SCAN_STRIDE)
        new = []
        for hh in range(N_RNN_HEADS):
            h = a_s[hh, idx, :] * hs[hh] + b_s[hh, idx, :]
            b_s[hh, idx, :] = h
            new.append(h)
        return tuple(new)

    hs = lax.fori_loop(0, tc, body, tuple(hc_s[hh] for hh in range(N_RNN_HEADS)), unroll=8)
    for hh in range(N_RNN_HEADS):
        hc_s[hh] = hs[hh]
        hT_ref[:, hh * RNN_HEAD:(hh + 1) * RNN_HEAD] = hs[hh]

    for bb in range(nb):
        row0 = bb * SCAN_STRIDE
        for hh in range(N_RNN_HEADS):
            sl = slice(hh * RNN_HEAD, (hh + 1) * RNN_HEAD)
            hb = b_s[hh, row0:row0 + tc, :]
            if fuse:
                hb = _gelu(zg_ref[bb, :, sl].astype(F32)) * (hf_ref[bb, :, sl].astype(F32) + hb)
            o_ref[bb, :, sl] = hb.astype(BF16)


def _scan(u, vecs, wg, h0, *, reverse, hf=None, zg=None):
    nb, t, c = u.shape
    tc = SCAN_CHUNK
    n_chunks = t // tc
    fuse = hf is not None

    def cidx(i):
        return (n_chunks - 1 - i) if reverse else i

    main_spec = pl.BlockSpec((nb, tc, c), lambda i: (0, cidx(i), 0))
    in_specs = [
        main_spec,
        pl.BlockSpec(vecs.shape, lambda i: (0, 0)),
        pl.BlockSpec(wg.shape, lambda i: (0, 0, 0)),
        pl.BlockSpec(h0.shape, lambda i: (0, 0)),
    ]
    args = [u, vecs, wg, h0]
    if fuse:
        in_specs += [main_spec, main_spec]
        args += [hf, zg]
    return pl.pallas_call(
        functools.partial(_scan_kernel, reverse=reverse, fuse=fuse),
        grid=(n_chunks,),
        in_specs=in_specs,
        out_specs=[main_spec, pl.BlockSpec((nb, c), lambda i: (0, 0))],
        out_shape=[jax.ShapeDtypeStruct((nb, t, c), BF16), jax.ShapeDtypeStruct((nb, c), F32)],
        scratch_shapes=[pltpu.VMEM((N_RNN_HEADS, nb * SCAN_STRIDE, RNN_HEAD), F32),
                        pltpu.VMEM((N_RNN_HEADS, nb * SCAN_STRIDE, RNN_HEAD), F32),
                        pltpu.VMEM((N_RNN_HEADS, nb, RNN_HEAD), F32)],
        compiler_params=_params("arbitrary"),
        name="scan_bwd" if reverse else "scan_fwd",
    )(*args)


DFT_ROWS = 64
DFT_COLS = 256


def _dftmm_kernel(zc_ref, zs_ref, c_ref, s_ref, o_ref, ar_s, ai_s, o_s):
    t, f = zc_ref.shape
    ns = t // RADIX
    rb = min(DFT_ROWS, ns)
    r2 = math.sqrt(0.5)

    def dft4(x):
        s0 = (x[0][0] + x[2][0], x[0][1] + x[2][1])
        s1 = (x[0][0] - x[2][0], x[0][1] - x[2][1])
        s2 = (x[1][0] + x[3][0], x[1][1] + x[3][1])
        s3 = (x[1][0] - x[3][0], x[1][1] - x[3][1])
        return [(s0[0] + s2[0], s0[1] + s2[1]), (s1[0] + s3[1], s1[1] - s3[0]),
                (s0[0] - s2[0], s0[1] - s2[1]), (s1[0] - s3[1], s1[1] + s3[0])]

    def butterfly(i, carry):
        r0 = pl.multiple_of(i * rb, rb)
        w = [(zc_ref[pl.ds(j * ns + r0, rb), :].astype(F32), -zs_ref[pl.ds(j * ns + r0, rb), :].astype(F32))
             for j in range(RADIX)]
        u = [(w[j][0] + w[j + 4][0], w[j][1] + w[j + 4][1]) for j in range(4)]
        v = [(w[j][0] - w[j + 4][0], w[j][1] - w[j + 4][1]) for j in range(4)]
        v = [v[0],
             ((v[1][0] + v[1][1]) * r2, (v[1][1] - v[1][0]) * r2),
             (v[2][1], -v[2][0]),
             ((v[3][1] - v[3][0]) * r2, (-v[3][0] - v[3][1]) * r2)]
        even = dft4(u)
        odd = dft4(v)
        for m in range(4):
            for k1, a in ((2 * m, even[m]), (2 * m + 1, odd[m])):
                ar_s[k1, pl.ds(r0, rb), :] = a[0].astype(BF16)
                ai_s[k1, pl.ds(r0, rb), :] = a[1].astype(BF16)
        return carry

    lax.fori_loop(0, ns // rb, butterfly, 0)

    for c0 in range(0, f, DFT_COLS):
        for k1 in range(RADIX):
            res = (_dot(c_ref[k1], ar_s[k1, :, c0:c0 + DFT_COLS])
                   + _dot(s_ref[k1], ai_s[k1, :, c0:c0 + DFT_COLS]))
            for j in range(DFT_COLS // LANES):
                o_s[j, pl.ds(k1, ns, stride=RADIX), :] = res[:, j * LANES:(j + 1) * LANES]
        for j in range(DFT_COLS // LANES):
            o_ref[:, c0 + j * LANES:c0 + (j + 1) * LANES] = o_s[j].astype(BF16)


def _dft_apply(ct, st, zfc, zfs):
    b, t, f = zfc.shape
    ns = t // RADIX
    table = pl.BlockSpec((RADIX, ns, ns), lambda bb: (0, 0, 0), pipeline_mode=pl.Buffered(1))
    return pl.pallas_call(
        _dftmm_kernel,
        grid=(b,),
        in_specs=[pl.BlockSpec((None, t, f), lambda bb: (bb, 0, 0)),
                  pl.BlockSpec((None, t, f), lambda bb: (bb, 0, 0)),
                  table, table],
        out_specs=pl.BlockSpec((None, t, f), lambda bb: (bb, 0, 0)),
        out_shape=jax.ShapeDtypeStruct((b, t, f), BF16),
        scratch_shapes=[pltpu.VMEM((RADIX, ns, f), BF16),
                        pltpu.VMEM((RADIX, ns, f), BF16),
                        pltpu.VMEM((DFT_COLS // LANES, t, LANES), F32)],
        compiler_params=_params("parallel"),
        name="dft_apply",
    )(zfc, zfs, ct, st)


def _merge_kernel(f_ref, gr_ref, za_ref, zb_ref, x_ref, pos_ref, mod_ref, g2_ref,
                  wf_ref, wl_ref, wo_ref, wr_ref, x1_ref, h2_ref, sc_ref, scr_ref):
    y_four = _dot(f_ref[...], wf_ref[...])
    y_rnn = _dot(gr_ref[...], wl_ref[...])
    m = _sigmoid(za_ref[...].astype(F32)) * y_four + _sigmoid(zb_ref[...].astype(F32)) * y_rnn
    y = _dot(m.astype(BF16), wo_ref[...])
    x1 = (x_ref[...] + pos_ref[...]) + mod_ref[2:3, :] * y
    x1_ref[...] = x1
    h2 = (_rmsnorm(x1, g2_ref[...]) * (1.0 + mod_ref[4:5, :]) + mod_ref[3:4, :]).astype(BF16)
    h2_ref[...] = h2
    logits = lax.dot_general(wr_ref[...], h2, (((1,), (1,)), ((), ())), preferred_element_type=F32)
    e = jnp.exp(logits - jnp.max(logits, axis=0, keepdims=True))
    sc = e / jnp.sum(e, axis=0, keepdims=True)
    sc_ref[...] = sc
    hi = sc.astype(BF16).astype(F32)
    lo = sc - hi
    tm = sc.shape[1]
    row = lax.broadcasted_iota(jnp.int32, (BF16_ROWS, tm), 0)
    for ex in range(N_EXPERTS):
        tile = jnp.where(row == 0, hi[ex:ex + 1, :], jnp.where(row == 1, lo[ex:ex + 1, :], 0.0))
        scr_ref[ex] = tile.astype(BF16)


def _merge(f, gr, za, zb, x, pos, mods, g2, w_four, w_lru, w_out, w_rt):
    b, t, d = x.shape
    tm = min(TOKEN_BLOCK, t)
    tok = lambda wd: pl.BlockSpec((None, tm, wd), lambda i, bb: (bb, i, 0))
    full = lambda a: pl.BlockSpec(a.shape, lambda i, bb: (0,) * a.ndim)
    return pl.pallas_call(
        _merge_kernel,
        grid=(t // tm, b),
        in_specs=[tok(D_FOURIER), tok(d), tok(d), tok(d), tok(d),
                  pl.BlockSpec((tm, d), lambda i, bb: (i, 0)),
                  pl.BlockSpec((None, SUBLANES, d), lambda i, bb: (bb, 0, 0)),
                  full(g2), full(w_four), full(w_lru), full(w_out), full(w_rt)],
        out_specs=[tok(d), tok(d), pl.BlockSpec((None, N_EXPERTS, tm), lambda i, bb: (bb, 0, i)),
                   pl.BlockSpec((None, N_EXPERTS, BF16_ROWS, tm), lambda i, bb: (bb, 0, 0, i))],
        out_shape=[jax.ShapeDtypeStruct((b, t, d), F32),
                   jax.ShapeDtypeStruct((b, t, d), BF16),
                   jax.ShapeDtypeStruct((b, N_EXPERTS, t), F32),
                   jax.ShapeDtypeStruct((b, N_EXPERTS, BF16_ROWS, t), BF16)],
        compiler_params=_params("parallel", "parallel"),
        name="merge",
    )(f, gr, za, zb, x, pos, mods, g2, w_four, w_lru, w_out, w_rt)


MAX_EXP = 127
EXP_STEPS = 7
VAL_STEPS = 56


OFF_CHUNK = 256


def _select_kernel(s_ref, slot_ref, offs_ref, *, cap):
    rows, t = s_ref.shape

    def count(mask):
        return jnp.sum(jnp.where(mask, 1.0, 0.0), axis=1, keepdims=True)

    def exp_step(_, carry):
        k_lo, k_hi = carry
        k_mid = jnp.floor(0.5 * (k_lo + k_hi))
        ok = count(s_ref[...] >= jnp.exp2(-k_mid)) >= cap
        return jnp.where(ok, k_lo, k_mid), jnp.where(ok, k_mid, k_hi)

    k_lo, k_hi = lax.fori_loop(
        0, EXP_STEPS, exp_step,
        (jnp.full((rows, 1), -1.0, F32), jnp.full((rows, 1), float(MAX_EXP), F32)))
    lo0 = jnp.where(k_hi >= MAX_EXP, 0.0, jnp.exp2(-k_hi))
    hi0 = jnp.exp2(-k_lo)

    def val_step(_, carry):
        lo, hi = carry
        mid = 0.5 * (lo + hi)
        ok = count(s_ref[...] >= mid) >= cap
        return jnp.where(ok, mid, lo), jnp.where(ok, hi, mid)

    lo, hi = lax.fori_loop(0, VAL_STEPS, val_step, (lo0, hi0))
    s = s_ref[...]
    gt = s >= hi
    eq = (s >= lo) & (s < hi)
    need = cap - count(gt)

    p = lax.broadcasted_iota(jnp.int32, (LANES, LANES), 0)
    q = lax.broadcasted_iota(jnp.int32, (LANES, LANES), 1)
    tri = (p < q).astype(BF16)
    run_gt = jnp.zeros((rows, 1), F32)
    run_eq = jnp.zeros((rows, 1), F32)
    off_lane = lax.broadcasted_iota(jnp.int32, (rows, LANES), 1)
    offs = jnp.zeros((rows, LANES), F32)
    for j in range(t // LANES):
        if (j * LANES) % OFF_CHUNK == 0:
            offs = jnp.where(off_lane == (j * LANES) // OFF_CHUNK, run_gt + jnp.minimum(run_eq, need), offs)
        sl = slice(j * LANES, (j + 1) * LANES)
        g = gt[:, sl]
        q_ = eq[:, sl]
        pg = _dot(g.astype(BF16), tri) + run_gt
        pe = _dot(q_.astype(BF16), tri) + run_eq
        sel = g | (q_ & (pe < need))
        pos = pg + jnp.minimum(pe, need)
        slot_ref[:, sl] = jnp.where(sel, pos, -1.0).astype(jnp.int32)
        run_gt = run_gt + count(g)
        run_eq = run_eq + count(q_)
    offs = jnp.where(off_lane == t // OFF_CHUNK, run_gt + jnp.minimum(run_eq, need), offs)
    offs_ref[...] = offs.astype(jnp.int32)


def _select(scores, cap):
    rows, t = scores.shape
    assert t % OFF_CHUNK == 0 and t // OFF_CHUNK < LANES
    return pl.pallas_call(
        functools.partial(_select_kernel, cap=cap),
        out_shape=[jax.ShapeDtypeStruct((rows, t), jnp.int32),
                   jax.ShapeDtypeStruct((rows, LANES), jnp.int32)],
        compiler_params=_params(),
        name="select",
    )(scores)


GATHER_BLOCK = 512
GATHER_WINDOW = 128


def _expert_kernel(offs_ref, slot_ref, sc_ref, h_ref, wg_ref, wu_ref, wd_ref, o_ref,
                   xg_s, gs_s, rows_s, gates_s, *, cap, n_off):
    t = h_ref.shape[0]
    tk = min(GATHER_BLOCK, t)
    win = min(GATHER_WINDOW, cap)
    per_block = tk // OFF_CHUNK
    slot_iota = lax.broadcasted_iota(jnp.int32, (win, tk), 0)
    xg_s[...] = jnp.zeros_like(xg_s)
    gs_s[...] = jnp.zeros_like(gs_s)
    row0 = (pl.program_id(0) * N_EXPERTS + pl.program_id(1)) * n_off

    def gather(sl, lo):
        onehot = (slot_iota + lo == slot_ref[:, sl]).astype(BF16)
        rows = _dot(onehot, h_ref[sl, :])
        gates = lax.dot_general(onehot, sc_ref[:, sl], (((1,), (1,)), ((), ())), preferred_element_type=F32)
        return rows, gates

    n_blocks = t // tk
    starts = [pl.multiple_of((offs_ref[row0 + j * per_block] // SUBLANES) * SUBLANES, SUBLANES)
              for j in range(n_blocks)]
    for j in range(n_blocks):
        rows_s[j], gates_s[j] = gather(slice(j * tk, (j + 1) * tk), starts[j])
    for j in range(n_blocks):
        sl = slice(j * tk, (j + 1) * tk)
        st = starts[j]
        last = offs_ref[row0 + (j + 1) * per_block]
        xg_s[pl.ds(st, SUBLANES), :] += rows_s[j, 0:SUBLANES, :]
        xg_s[pl.ds(st + SUBLANES, win - SUBLANES), :] = rows_s[j, SUBLANES:win, :]
        gs_s[pl.ds(st, win), :] += gates_s[j]

        def more(c, carry, sl=sl, st=st):
            lo = pl.multiple_of(st + c * win, SUBLANES)
            rows, gates = gather(sl, lo)
            xg_s[pl.ds(lo, win), :] = rows
            gs_s[pl.ds(lo, win), :] += gates
            return carry

        lax.fori_loop(1, (last - st + win - 1) // win, more, 0)
    xb = xg_s[0:cap, :].astype(BF16)
    gate = _dot(xb, wg_ref[...])
    hid = (gate * _sigmoid(gate)) * _dot(xb, wu_ref[...])
    y = _dot(hid.astype(BF16), wd_ref[...])
    o_ref[...] = (y * (gs_s[0:cap, 0:1] + gs_s[0:cap, 1:2])).astype(BF16)


def _experts(offs, slot, score_rows, h2, wg, wu, wd, cap):
    b, t, d = h2.shape
    n_off = offs.shape[0] // (b * N_EXPERTS)
    win = min(GATHER_WINDOW, cap)
    grid_spec = pltpu.PrefetchScalarGridSpec(
        num_scalar_prefetch=1,
        grid=(b, N_EXPERTS),
        in_specs=[pl.BlockSpec((None, 1, t), lambda bb, e, o: (bb * N_EXPERTS + e, 0, 0)),
                  pl.BlockSpec((None, None, BF16_ROWS, t), lambda bb, e, o: (bb, e, 0, 0)),
                  pl.BlockSpec((None, t, d), lambda bb, e, o: (bb, 0, 0)),
                  pl.BlockSpec((None, d, D_EXPERT), lambda bb, e, o: (e, 0, 0)),
                  pl.BlockSpec((None, d, D_EXPERT), lambda bb, e, o: (e, 0, 0)),
                  pl.BlockSpec((None, D_EXPERT, d), lambda bb, e, o: (e, 0, 0))],
        out_specs=pl.BlockSpec((None, cap, d), lambda bb, e, o: (bb * N_EXPERTS + e, 0, 0)),
        scratch_shapes=[pltpu.VMEM((cap + win, d), F32), pltpu.VMEM((cap + win, BF16_ROWS), F32),
                        pltpu.VMEM((t // min(GATHER_BLOCK, t), win, d), F32),
                        pltpu.VMEM((t // min(GATHER_BLOCK, t), win, BF16_ROWS), F32)],
    )
    return pl.pallas_call(
        functools.partial(_expert_kernel, cap=cap, n_off=n_off),
        grid_spec=grid_spec,
        out_shape=jax.ShapeDtypeStruct((b * N_EXPERTS, cap, d), BF16),
        compiler_params=_params("parallel", "parallel"),
        name="experts",
    )(offs, slot.reshape(b * N_EXPERTS, 1, t), score_rows, h2, wg, wu, wd)


COMBINE_BLOCK = 1024
SLOT_WINDOW = 256


def _combine_kernel(offs_ref, slot_ref, y_ref, x1_ref, mod_ref, g_ref, o_ref, acc_s, *, cap, n_off):
    tm = x1_ref.shape[0]
    kc = min(SLOT_WINDOW, cap)
    per_block = tm // OFF_CHUNK
    lane_iota = lax.broadcasted_iota(jnp.int32, (tm, kc), 1)

    def window(e, start0, c):
        lo = start0 + c * kc
        st = pl.multiple_of(jnp.minimum(lo, cap - kc), BF16_ROWS)
        onehot = slot_ref[:, e:e + 1] == st + lane_iota
        if not isinstance(c, int) or c > 0:
            onehot = onehot & (slot_ref[:, e:e + 1] >= lo)
        return _dot(onehot.astype(BF16), y_ref[pl.ds(e * cap + st, kc), :])

    starts, extras = [], []
    for e in range(N_EXPERTS):
        o0 = (pl.program_id(0) * N_EXPERTS + e) * n_off + pl.program_id(1) * per_block
        start0 = (offs_ref[o0] // BF16_ROWS) * BF16_ROWS
        starts.append(start0)
        extras.append(jnp.maximum((offs_ref[o0 + per_block] - start0 + kc - 1) // kc - 1, 0))

    acc = jnp.zeros((tm, D_MODEL), F32)
    for e in range(N_EXPERTS):
        acc = acc + window(e, starts[e], 0)

    def finish(total):
        o_ref[...] = _rmsnorm(x1_ref[...] + mod_ref[5:6, :] * total, g_ref[...])

    finish(acc)

    @pl.when(sum(extras) > 0)
    def _():
        acc_s[...] = acc
        for e in range(N_EXPERTS):
            def more(c, carry, e=e):
                acc_s[...] += window(e, starts[e], c)
                return carry
            lax.fori_loop(1, extras[e] + 1, more, 0)
        finish(acc_s[...])


def _combine(offs, slot_t, y, x1, mods, g, cap):
    b, t, d = x1.shape
    tm = min(COMBINE_BLOCK, t)
    n_off = offs.shape[0] // (b * N_EXPERTS)
    grid_spec = pltpu.PrefetchScalarGridSpec(
        num_scalar_prefetch=1,
        grid=(b, t // tm),
        in_specs=[pl.BlockSpec((None, tm, N_EXPERTS), lambda bb, i, o: (bb, i, 0)),
                  pl.BlockSpec((None, N_EXPERTS * cap, d), lambda bb, i, o: (bb, 0, 0),
                               pipeline_mode=pl.Buffered(1)),
                  pl.BlockSpec((None, tm, d), lambda bb, i, o: (bb, i, 0)),
                  pl.BlockSpec((None, SUBLANES, d), lambda bb, i, o: (bb, 0, 0)),
                  pl.BlockSpec((1, d), lambda bb, i, o: (0, 0))],
        out_specs=pl.BlockSpec((None, tm, d), lambda bb, i, o: (bb, i, 0)),
        scratch_shapes=[pltpu.VMEM((tm, d), F32)],
    )
    return pl.pallas_call(
        functools.partial(_combine_kernel, cap=cap, n_off=n_off),
        grid_spec=grid_spec,
        out_shape=jax.ShapeDtypeStruct((b, t, d), F32),
        compiler_params=_params("parallel", "parallel"),
        name="combine",
    )(offs, slot_t, y, x1, mods, g)


def kernel(x, c, ctx, c_ctx, w_ada, b_ada, norm1_g, norm2_g, w_in, w_four, conv_w, conv_b, lru_lambda, lru_wa, lru_ba, lru_wi, lru_bi, w_lru, w_out, w_router, w_gate_e, w_up_e, w_down_e, final_g):
    b, t, d = x.shape
    assert d == D_MODEL and b == SUBLANES and t % TOKEN_BLOCK == 0 and ctx.shape[1] % SCAN_CHUNK == 0
    assert w_ada.shape[0] == 1, "single-layer problem: the context stream is only read by the recurrence"
    cap = CAPACITY_FACTOR * t // N_EXPERTS
    col_x = D_FOURIER + D_RNN
    col_g = col_x + D_RNN
    col_a = col_g + D_MODEL

    pos = _pos_table(t)
    ct, st = _dft_tables(t)

    cc = jnp.concatenate([c, c_ctx[None], jnp.zeros((SUBLANES - 1, d), F32)], axis=0)
    mods = _ada(cc, w_ada[0], b_ada[0]).reshape(cc.shape[0], N_MOD, d)
    mods = jnp.pad(mods, ((0, 0), (0, SUBLANES - N_MOD), (0, 0)))

    w = w_in[0]
    wfc, wfs = _fold_channel_dft(w[:, :D_FOURIER])
    w_x = w[:, D_FOURIER:col_x].astype(BF16)
    w_cat = jnp.concatenate([w_x, w[:, col_x:].astype(BF16), wfc, wfs], axis=1)
    g1 = norm1_g[0].reshape(1, d)
    zx, zg, za, zb, zfc, zfs = _proj(x, pos, mods, None, g1, w_cat,
                                     (D_RNN, D_RNN, D_MODEL, D_MODEL, D_FOURIER, D_FOURIER))
    (zx_ctx,) = _proj(ctx, None, mods, b, g1, w_x, (D_RNN,))

    for dr in range(2):
        vecs = jnp.concatenate([lru_lambda[0, dr][None], lru_ba[0, dr][None], lru_bi[0, dr][None],
                                conv_b[0][None], conv_w[0]], axis=0)
        wg = jnp.concatenate([lru_wa[0, dr], lru_wi[0, dr]], axis=-1).astype(BF16)
        if dr == 0:
            u_lat = _conv(zx, vecs)
            u_ctx = _conv(zx_ctx, vecs)
        _, h_ctx = _scan(u_ctx, vecs, wg, jnp.zeros((b, D_RNN), F32), reverse=bool(dr))
        if dr == 0:
            h_fwd, _ = _scan(u_lat, vecs, wg, h_ctx, reverse=False)
        else:
            gr, _ = _scan(u_lat, vecs, wg, h_ctx, reverse=True, hf=h_fwd, zg=zg)

    f = _dft_apply(ct, st, zfc, zfs)

    x1, h2, scores, score_rows = _merge(f, gr, za, zb, x, pos, mods, norm2_g[0].reshape(1, d),
                                        w_four[0].astype(BF16), w_lru[0].astype(BF16), w_out[0].astype(BF16),
                                        w_router[0].T.astype(BF16))

    slot, offs = _select(scores.reshape(b * N_EXPERTS, t), cap)
    offs = offs[:, :t // OFF_CHUNK + 1].reshape(-1)
    y = _experts(offs, slot, score_rows, h2, w_gate_e[0].astype(BF16), w_up_e[0].astype(BF16),
                 w_down_e[0].astype(BF16), cap)
    slot_t = jnp.swapaxes(slot.reshape(b, N_EXPERTS, t), 1, 2)
    return _combine(offs, slot_t, y.reshape(b, N_EXPERTS * cap, d), x1, mods, final_g.reshape(1, d), cap)
```

```python
import functools
import math

import numpy as np
import jax
import jax.numpy as jnp
from jax import lax
from jax.experimental import pallas as pl
from jax.experimental.pallas import tpu as pltpu

F32 = jnp.float32
BF16 = jnp.bfloat16

D_MODEL = 1024
GRID_W = 64
N_FOURIER_GROUPS = 4
FOURIER_GROUP = 128
D_FOURIER = N_FOURIER_GROUPS * FOURIER_GROUP
D_RNN = D_MODEL
N_RNN_HEADS = 8
RNN_HEAD = D_RNN // N_RNN_HEADS
CONV_W = 4
CONV_LEFT = 2
LRU_C = 8.0
LOG2E = math.log2(math.e)
N_EXPERTS = 16
CAPACITY_FACTOR = 2
D_EXPERT = 1536
N_MOD = 6
EPS = 1e-6
POS_MAX_PERIOD = 10000.0

LANES = 128
SUBLANES = 8
BF16_ROWS = 16
VMEM_LIMIT = 56 * 1024 * 1024

TOKEN_BLOCK = 512
SCAN_CHUNK = 128
SCAN_STRIDE = SCAN_CHUNK + SUBLANES


def _params(*sem):
    return pltpu.CompilerParams(dimension_semantics=sem, vmem_limit_bytes=VMEM_LIMIT)


def _sigmoid(x):
    return 0.5 * jnp.tanh(0.5 * x) + 0.5


def _rmsnorm(x, g):
    return x * lax.rsqrt(jnp.mean(x * x, axis=-1, keepdims=True) + EPS) * g


def _dot(a, b):
    return jnp.dot(a, b, preferred_element_type=F32)


def _pos_kernel(o_ref):
    rows = o_ref.shape[0]
    q = D_MODEL // 4
    k = lax.broadcasted_iota(jnp.int32, (1, q), 1).astype(F32)
    freqs = jnp.exp(-math.log(POS_MAX_PERIOD) * k / q)
    ang_r = lax.broadcasted_iota(jnp.int32, (rows, q), 0).astype(F32) * freqs
    ang_c = lax.broadcasted_iota(jnp.int32, (GRID_W, q), 0).astype(F32) * freqs
    shape = (rows, GRID_W, q)
    o_ref[:, :, 0 * q:1 * q] = jnp.broadcast_to(jnp.sin(ang_r)[:, None, :], shape)
    o_ref[:, :, 1 * q:2 * q] = jnp.broadcast_to(jnp.cos(ang_r)[:, None, :], shape)
    o_ref[:, :, 2 * q:3 * q] = jnp.broadcast_to(jnp.sin(ang_c)[None, :, :], shape)
    o_ref[:, :, 3 * q:4 * q] = jnp.broadcast_to(jnp.cos(ang_c)[None, :, :], shape)


def _pos_table(n_tokens):
    rows = n_tokens // GRID_W
    out = pl.pallas_call(
        _pos_kernel,
        out_shape=jax.ShapeDtypeStruct((rows, GRID_W, D_MODEL), F32),
        name="pos_table",
    )()
    return out.reshape(n_tokens, D_MODEL)


RADIX = 8


def _dft_kernel(c_ref, s_ref, *, n):
    ns = c_ref.shape[0]
    k = RADIX * lax.broadcasted_iota(jnp.int32, (ns, ns), 0) + pl.program_id(0)
    t2 = lax.broadcasted_iota(jnp.int32, (ns, ns), 1)
    ang = ((k * t2) & (n - 1)).astype(F32) * (2.0 * math.pi / n)
    scale = 1.0 / math.sqrt(n)
    c_ref[...] = (jnp.cos(ang) * scale).astype(BF16)
    s_ref[...] = (jnp.sin(ang) * scale).astype(BF16)


def _dft_tables(n):
    assert n & (n - 1) == 0 and n % (RADIX * BF16_ROWS) == 0
    ns = n // RADIX
    return pl.pallas_call(
        functools.partial(_dft_kernel, n=n),
        grid=(RADIX,),
        out_specs=[pl.BlockSpec((None, ns, ns), lambda i: (i, 0, 0))] * 2,
        out_shape=[jax.ShapeDtypeStruct((RADIX, ns, ns), BF16)] * 2,
        compiler_params=_params("parallel"),
        name="dft_tables",
    )()


def _fold_kernel(w_ref, c_ref, s_ref, oc_ref, os_ref):
    for g in range(N_FOURIER_GROUPS):
        sl = slice(g * FOURIER_GROUP, (g + 1) * FOURIER_GROUP)
        w = w_ref[:, sl]
        oc_ref[:, sl] = jnp.dot(w, c_ref[...], preferred_element_type=F32,
                                precision=lax.Precision.HIGHEST).astype(BF16)
        os_ref[:, sl] = jnp.dot(w, s_ref[...], preferred_element_type=F32,
                                precision=lax.Precision.HIGHEST).astype(BF16)


def _fold_channel_dft(w_f):
    p = np.arange(FOURIER_GROUP)
    ang = 2.0 * np.pi * ((p[:, None] * p[None, :]) % FOURIER_GROUP) / FOURIER_GROUP
    cp = jnp.asarray(np.cos(ang) / math.sqrt(FOURIER_GROUP), F32)
    sp = jnp.asarray(np.sin(ang) / math.sqrt(FOURIER_GROUP), F32)
    return pl.pallas_call(
        _fold_kernel,
        out_shape=[jax.ShapeDtypeStruct(w_f.shape, BF16)] * 2,
        compiler_params=_params(),
        name="fold_channel_dft",
    )(w_f, cp, sp)


def _ada_kernel(c_ref, w_ref, b_ref, o_ref):
    c = c_ref[...]
    o_ref[...] = jnp.dot(c * _sigmoid(c), w_ref[...], preferred_element_type=F32,
                         precision=lax.Precision.HIGHEST) + b_ref[...]


def _ada(cc, w_ada, b_ada):
    n = w_ada.shape[1]
    tn = D_MODEL
    return pl.pallas_call(
        _ada_kernel,
        grid=(n // tn,),
        in_specs=[pl.BlockSpec(cc.shape, lambda j: (0, 0)),
                  pl.BlockSpec((D_MODEL, tn), lambda j: (0, j)),
                  pl.BlockSpec((1, tn), lambda j: (0, j))],
        out_specs=pl.BlockSpec((cc.shape[0], tn), lambda j: (0, j)),
        out_shape=jax.ShapeDtypeStruct((cc.shape[0], n), F32),
        compiler_params=_params("parallel"),
        name="ada",
    )(cc, w_ada, b_ada.reshape(1, n))


def _proj_kernel(*refs, widths, has_pos):
    if has_pos:
        x_ref, pos_ref, mod_ref, g_ref, w_ref = refs[:5]
        outs = refs[5:]
        x = x_ref[...] + pos_ref[...]
    else:
        x_ref, mod_ref, g_ref, w_ref = refs[:4]
        outs = refs[4:]
        x = x_ref[...]
    h = _rmsnorm(x, g_ref[...]) * (1.0 + mod_ref[1:2, :]) + mod_ref[0:1, :]
    hb = h.astype(BF16)
    c0 = 0
    for o_ref, wd in zip(outs, widths):
        o_ref[...] = _dot(hb, w_ref[:, c0:c0 + wd]).astype(BF16)
        c0 += wd


def _proj(x, pos, mods, mod_row, g, w, widths):
    b, t, d = x.shape
    tm = min(TOKEN_BLOCK, t)
    has_pos = pos is not None
    in_specs = [pl.BlockSpec((None, tm, d), lambda i, bb: (bb, i, 0))]
    args = [x]
    if has_pos:
        in_specs.append(pl.BlockSpec((tm, d), lambda i, bb: (i, 0)))
        args.append(pos)
    if mod_row is None:
        mod_map = lambda i, bb: (bb, 0, 0)
    else:
        mod_map = lambda i, bb: (mod_row, 0, 0)
    in_specs += [pl.BlockSpec((None, SUBLANES, d), mod_map),
                 pl.BlockSpec((1, d), lambda i, bb: (0, 0)),
                 pl.BlockSpec(w.shape, lambda i, bb: (0, 0))]
    args += [mods, g, w]
    return pl.pallas_call(
        functools.partial(_proj_kernel, widths=widths, has_pos=has_pos),
        grid=(t // tm, b),
        in_specs=in_specs,
        out_specs=[pl.BlockSpec((None, tm, wd), lambda i, bb: (bb, i, 0)) for wd in widths],
        out_shape=[jax.ShapeDtypeStruct((b, t, wd), BF16) for wd in widths],
        compiler_params=_params("parallel", "parallel"),
        name="proj",
    )(*args)


def _gelu(x):
    return 0.5 * x * (1.0 + jnp.tanh(math.sqrt(2.0 / math.pi) * (x + 0.044715 * x * x * x)))


def _conv_kernel(x_ref, xp_ref, xn_ref, vec_ref, o_ref):
    tm = x_ref.shape[0]
    edge = BF16_ROWS
    i = pl.program_id(0)
    first = i == 0
    last = i == pl.num_programs(0) - 1
    w = [vec_ref[4 + kk:5 + kk, :] for kk in range(CONV_W)]
    bias = vec_ref[3:4, :]

    def taps(xm2, xm1, x0, xp1):
        return bias + w[0] * xm2 + w[1] * xm1 + w[2] * x0 + w[3] * xp1

    x = x_ref[...].astype(F32)
    o_ref[...] = taps(pltpu.roll(x, 2, 0), pltpu.roll(x, 1, 0), x, pltpu.roll(x, tm - 1, 0)).astype(BF16)

    row = lax.broadcasted_iota(jnp.int32, (edge, x.shape[1]), 0)

    def shifted(before, cur, after, s):
        if s < 0:
            return jnp.where(row < -s, pltpu.roll(before, -s, 0), pltpu.roll(cur, -s, 0))
        return jnp.where(row < edge - s, pltpu.roll(cur, edge - s, 0), pltpu.roll(after, edge - s, 0))

    prev = jnp.where(first, 0.0, xp_ref[...].astype(F32))
    nxt = jnp.where(last, 0.0, xn_ref[...].astype(F32))
    head, head2 = x[0:edge], x[edge:2 * edge]
    tail, tail2 = x[tm - edge:tm], x[tm - 2 * edge:tm - edge]
    o_ref[0:edge, :] = taps(shifted(prev, head, head2, -2), shifted(prev, head, head2, -1), head,
                            shifted(prev, head, head2, 1)).astype(BF16)
    o_ref[tm - edge:tm, :] = taps(shifted(tail2, tail, nxt, -2), shifted(tail2, tail, nxt, -1), tail,
                                  shifted(tail2, tail, nxt, 1)).astype(BF16)


def _conv(zx, vecs):
    b, t, c = zx.shape
    tm = min(TOKEN_BLOCK, t)
    assert tm >= 2 * BF16_ROWS
    hpb = tm // BF16_ROWS
    n_halo = t // BF16_ROWS
    return pl.pallas_call(
        _conv_kernel,
        grid=(t // tm, b),
        in_specs=[pl.BlockSpec((None, tm, c), lambda i, bb: (bb, i, 0)),
                  pl.BlockSpec((None, BF16_ROWS, c), lambda i, bb: (bb, jnp.maximum(i * hpb - 1, 0), 0)),
                  pl.BlockSpec((None, BF16_ROWS, c), lambda i, bb: (bb, jnp.minimum((i + 1) * hpb, n_halo - 1), 0)),
                  pl.BlockSpec(vecs.shape, lambda i, bb: (0, 0))],
        out_specs=pl.BlockSpec((None, tm, c), lambda i, bb: (bb, i, 0)),
        out_shape=jax.ShapeDtypeStruct((b, t, c), BF16),
        compiler_params=_params("parallel", "parallel"),
        name="conv",
    )(zx, zx, zx, vecs)


def _scan_kernel(*refs, reverse, fuse):
    if fuse:
        u_ref, vec_ref, wg_ref, h0_ref, hf_ref, zg_ref, o_ref, hT_ref, a_s, b_s, h_s, hc_s = refs
    else:
        u_ref, vec_ref, wg_ref, h0_ref, o_ref, hT_ref, a_s, b_s, h_s, hc_s = refs
    nb, tc, c = u_ref.shape

    @pl.when(pl.program_id(0) == 0)
    def _():
        for hh in range(N_RNN_HEADS):
            hc_s[hh] = h0_ref[:, hh * RNN_HEAD:(hh + 1) * RNN_HEAD]

    lam = vec_ref[0:1, :]
    half_b_a = vec_ref[1:2, :]
    half_b_i = vec_ref[2:3, :]
    k_row = (-0.5 * LRU_C * LOG2E) * (jnp.maximum(-lam, 0.0) + jnp.log1p(jnp.exp(-jnp.abs(lam))))

    for bb in range(nb):
        ub = u_ref[bb]
        u = ub.astype(F32)
        row0 = bb * SCAN_STRIDE
        for hh in range(N_RNN_HEADS):
            sl = slice(hh * RNN_HEAD, (hh + 1) * RNN_HEAD)
            gates = _dot(ub[:, sl], wg_ref[hh])
            t_r = jnp.tanh(gates[:, :RNN_HEAD] + half_b_a[:, sl])
            t_i = jnp.tanh(gates[:, RNN_HEAD:] + half_b_i[:, sl])
            a = jnp.exp2(k_row[:, sl] * t_r + k_row[:, sl])
            a_s[hh, row0:row0 + tc, :] = a
            scale = jnp.exp2((0.5 * LOG2E) * jnp.log(1.0 - a * a)) * u[:, sl]
            b_s[hh, row0:row0 + tc, :] = scale * (0.5 * t_i + 0.5)

    def body(j, hs):
        t = (tc - 1 - j) if reverse else j
        idx = pl.ds(t, nb, stride=SCAN_STRIDE)
        new = []
        for hh in range(N_RNN_HEADS):
            h = a_s[hh, idx, :] * hs[hh] + b_s[hh, idx, :]
            h_s[hh, idx, :] = h
            new.append(h)
        return tuple(new)

    hs = lax.fori_loop(0, tc, body, tuple(hc_s[hh] for hh in range(N_RNN_HEADS)), unroll=8)
    for hh in range(N_RNN_HEADS):
        hc_s[hh] = hs[hh]
        hT_ref[:, hh * RNN_HEAD:(hh + 1) * RNN_HEAD] = hs[hh]

    for bb in range(nb):
        row0 = bb * SCAN_STRIDE
        for hh in range(N_RNN_HEADS):
            sl = slice(hh * RNN_HEAD, (hh + 1) * RNN_HEAD)
            hb = h_s[hh, row0:row0 + tc, :]
            if fuse:
                hb = _gelu(zg_ref[bb, :, sl].astype(F32)) * (hf_ref[bb, :, sl].astype(F32) + hb)
            o_ref[bb, :, sl] = hb.astype(BF16)


def _scan(u, vecs, wg, h0, *, reverse, hf=None, zg=None):
    nb, t, c = u.shape
    tc = SCAN_CHUNK
    n_chunks = t // tc
    fuse = hf is not None

    def cidx(i):
        return (n_chunks - 1 - i) if reverse else i

    main_spec = pl.BlockSpec((nb, tc, c), lambda i: (0, cidx(i), 0))
    in_specs = [
        main_spec,
        pl.BlockSpec(vecs.shape, lambda i: (0, 0)),
        pl.BlockSpec(wg.shape, lambda i: (0, 0, 0)),
        pl.BlockSpec(h0.shape, lambda i: (0, 0)),
    ]
    args = [u, vecs, wg, h0]
    if fuse:
        in_specs += [main_spec, main_spec]
        args += [hf, zg]
    return pl.pallas_call(
        functools.partial(_scan_kernel, reverse=reverse, fuse=fuse),
        grid=(n_chunks,),
        in_specs=in_specs,
        out_specs=[main_spec, pl.BlockSpec((nb, c), lambda i: (0, 0))],
        out_shape=[jax.ShapeDtypeStruct((nb, t, c), BF16), jax.ShapeDtypeStruct((nb, c), F32)],
        scratch_shapes=[pltpu.VMEM((N_RNN_HEADS, nb * SCAN_STRIDE, RNN_HEAD), F32),
                        pltpu.VMEM((N_RNN_HEADS, nb * SCAN_STRIDE, RNN_HEAD), F32),
                        pltpu.VMEM((N_RNN_HEADS, nb * SCAN_STRIDE, RNN_HEAD), F32),
                        pltpu.VMEM((N_RNN_HEADS, nb, RNN_HEAD), F32)],
        compiler_params=_params("arbitrary"),
        name="scan_bwd" if reverse else "scan_fwd",
    )(*args)


DFT_ROWS = 64
DFT_COLS = 256


def _dftmm_kernel(zc_ref, zs_ref, c_ref, s_ref, o_ref, ar_s, ai_s, o_s):
    t, f = zc_ref.shape
    ns = t // RADIX
    rb = min(DFT_ROWS, ns)
    r2 = math.sqrt(0.5)

    def dft4(x):
        s0 = (x[0][0] + x[2][0], x[0][1] + x[2][1])
        s1 = (x[0][0] - x[2][0], x[0][1] - x[2][1])
        s2 = (x[1][0] + x[3][0], x[1][1] + x[3][1])
        s3 = (x[1][0] - x[3][0], x[1][1] - x[3][1])
        return [(s0[0] + s2[0], s0[1] + s2[1]), (s1[0] + s3[1], s1[1] - s3[0]),
                (s0[0] - s2[0], s0[1] - s2[1]), (s1[0] - s3[1], s1[1] + s3[0])]

    def butterfly(i, carry):
        r0 = pl.multiple_of(i * rb, rb)
        w = [(zc_ref[pl.ds(j * ns + r0, rb), :].astype(F32), -zs_ref[pl.ds(j * ns + r0, rb), :].astype(F32))
             for j in range(RADIX)]
        u = [(w[j][0] + w[j + 4][0], w[j][1] + w[j + 4][1]) for j in range(4)]
        v = [(w[j][0] - w[j + 4][0], w[j][1] - w[j + 4][1]) for j in range(4)]
        v = [v[0],
             ((v[1][0] + v[1][1]) * r2, (v[1][1] - v[1][0]) * r2),
             (v[2][1], -v[2][0]),
             ((v[3][1] - v[3][0]) * r2, (-v[3][0] - v[3][1]) * r2)]
        even = dft4(u)
        odd = dft4(v)
        for m in range(4):
            for k1, a in ((2 * m, even[m]), (2 * m + 1, odd[m])):
                ar_s[k1, pl.ds(r0, rb), :] = a[0].astype(BF16)
                ai_s[k1, pl.ds(r0, rb), :] = a[1].astype(BF16)
        return carry

    lax.fori_loop(0, ns // rb, butterfly, 0)

    for c0 in range(0, f, DFT_COLS):
        for k1 in range(RADIX):
            res = (_dot(c_ref[k1], ar_s[k1, :, c0:c0 + DFT_COLS])
                   + _dot(s_ref[k1], ai_s[k1, :, c0:c0 + DFT_COLS]))
            for j in range(DFT_COLS // LANES):
                o_s[j, pl.ds(k1, ns, stride=RADIX), :] = res[:, j * LANES:(j + 1) * LANES]
        for j in range(DFT_COLS // LANES):
            o_ref[:, c0 + j * LANES:c0 + (j + 1) * LANES] = o_s[j].astype(BF16)


def _dft_apply(ct, st, zfc, zfs):
    b, t, f = zfc.shape
    ns = t // RADIX
    table = pl.BlockSpec((RADIX, ns, ns), lambda bb: (0, 0, 0), pipeline_mode=pl.Buffered(1))
    return pl.pallas_call(
        _dftmm_kernel,
        grid=(b,),
        in_specs=[pl.BlockSpec((None, t, f), lambda bb: (bb, 0, 0)),
                  pl.BlockSpec((None, t, f), lambda bb: (bb, 0, 0)),
                  table, table],
        out_specs=pl.BlockSpec((None, t, f), lambda bb: (bb, 0, 0)),
        out_shape=jax.ShapeDtypeStruct((b, t, f), BF16),
        scratch_shapes=[pltpu.VMEM((RADIX, ns, f), BF16),
                        pltpu.VMEM((RADIX, ns, f), BF16),
                        pltpu.VMEM((DFT_COLS // LANES, t, LANES), F32)],
        compiler_params=_params("parallel"),
        name="dft_apply",
    )(zfc, zfs, ct, st)


def _merge_kernel(f_ref, gr_ref, za_ref, zb_ref, x_ref, pos_ref, mod_ref, g2_ref,
                  wf_ref, wl_ref, wo_ref, wr_ref, x1_ref, h2_ref, sc_ref, scr_ref):
    y_four = _dot(f_ref[...], wf_ref[...])
    y_rnn = _dot(gr_ref[...], wl_ref[...])
    m = _sigmoid(za_ref[...].astype(F32)) * y_four + _sigmoid(zb_ref[...].astype(F32)) * y_rnn
    y = _dot(m.astype(BF16), wo_ref[...])
    x1 = (x_ref[...] + pos_ref[...]) + mod_ref[2:3, :] * y
    x1_ref[...] = x1
    h2 = (_rmsnorm(x1, g2_ref[...]) * (1.0 + mod_ref[4:5, :]) + mod_ref[3:4, :]).astype(BF16)
    h2_ref[...] = h2
    logits = lax.dot_general(wr_ref[...], h2, (((1,), (1,)), ((), ())), preferred_element_type=F32)
    e = jnp.exp(logits - jnp.max(logits, axis=0, keepdims=True))
    sc = e / jnp.sum(e, axis=0, keepdims=True)
    sc_ref[...] = sc
    hi = sc.astype(BF16).astype(F32)
    lo = sc - hi
    tm = sc.shape[1]
    row = lax.broadcasted_iota(jnp.int32, (BF16_ROWS, tm), 0)
    for ex in range(N_EXPERTS):
        tile = jnp.where(row == 0, hi[ex:ex + 1, :], jnp.where(row == 1, lo[ex:ex + 1, :], 0.0))
        scr_ref[ex] = tile.astype(BF16)


def _merge(f, gr, za, zb, x, pos, mods, g2, w_four, w_lru, w_out, w_rt):
    b, t, d = x.shape
    tm = min(TOKEN_BLOCK, t)
    tok = lambda wd: pl.BlockSpec((None, tm, wd), lambda i, bb: (bb, i, 0))
    full = lambda a: pl.BlockSpec(a.shape, lambda i, bb: (0,) * a.ndim)
    return pl.pallas_call(
        _merge_kernel,
        grid=(t // tm, b),
        in_specs=[tok(D_FOURIER), tok(d), tok(d), tok(d), tok(d),
                  pl.BlockSpec((tm, d), lambda i, bb: (i, 0)),
                  pl.BlockSpec((None, SUBLANES, d), lambda i, bb: (bb, 0, 0)),
                  full(g2), full(w_four), full(w_lru), full(w_out), full(w_rt)],
        out_specs=[tok(d), tok(d), pl.BlockSpec((None, N_EXPERTS, tm), lambda i, bb: (bb, 0, i)),
                   pl.BlockSpec((None, N_EXPERTS, BF16_ROWS, tm), lambda i, bb: (bb, 0, 0, i))],
        out_shape=[jax.ShapeDtypeStruct((b, t, d), F32),
                   jax.ShapeDtypeStruct((b, t, d), BF16),
                   jax.ShapeDtypeStruct((b, N_EXPERTS, t), F32),
                   jax.ShapeDtypeStruct((b, N_EXPERTS, BF16_ROWS, t), BF16)],
        compiler_params=_params("parallel", "parallel"),
        name="merge",
    )(f, gr, za, zb, x, pos, mods, g2, w_four, w_lru, w_out, w_rt)


MAX_EXP = 127
EXP_STEPS = 7
VAL_STEPS = 56


OFF_CHUNK = 256


def _select_kernel(s_ref, slot_ref, offs_ref, *, cap):
    rows, t = s_ref.shape

    def count(mask):
        return jnp.sum(jnp.where(mask, 1.0, 0.0), axis=1, keepdims=True)

    def exp_step(_, carry):
        k_lo, k_hi = carry
        k_mid = jnp.floor(0.5 * (k_lo + k_hi))
        ok = count(s_ref[...] >= jnp.exp2(-k_mid)) >= cap
        return jnp.where(ok, k_lo, k_mid), jnp.where(ok, k_mid, k_hi)

    k_lo, k_hi = lax.fori_loop(
        0, EXP_STEPS, exp_step,
        (jnp.full((rows, 1), -1.0, F32), jnp.full((rows, 1), float(MAX_EXP), F32)))
    lo0 = jnp.where(k_hi >= MAX_EXP, 0.0, jnp.exp2(-k_hi))
    hi0 = jnp.exp2(-k_lo)

    def val_step(_, carry):
        lo, hi = carry
        mid = 0.5 * (lo + hi)
        ok = count(s_ref[...] >= mid) >= cap
        return jnp.where(ok, mid, lo), jnp.where(ok, hi, mid)

    lo, hi = lax.fori_loop(0, VAL_STEPS, val_step, (lo0, hi0))
    s = s_ref[...]
    gt = s >= hi
    eq = (s >= lo) & (s < hi)
    need = cap - count(gt)

    p = lax.broadcasted_iota(jnp.int32, (LANES, LANES), 0)
    q = lax.broadcasted_iota(jnp.int32, (LANES, LANES), 1)
    tri = (p < q).astype(BF16)
    run_gt = jnp.zeros((rows, 1), F32)
    run_eq = jnp.zeros((rows, 1), F32)
    off_lane = lax.broadcasted_iota(jnp.int32, (rows, LANES), 1)
    offs = jnp.zeros((rows, LANES), F32)
    for j in range(t // LANES):
        if (j * LANES) % OFF_CHUNK == 0:
            offs = jnp.where(off_lane == (j * LANES) // OFF_CHUNK, run_gt + jnp.minimum(run_eq, need), offs)
        sl = slice(j * LANES, (j + 1) * LANES)
        g = gt[:, sl]
        q_ = eq[:, sl]
        pg = _dot(g.astype(BF16), tri) + run_gt
        pe = _dot(q_.astype(BF16), tri) + run_eq
        sel = g | (q_ & (pe < need))
        pos = pg + jnp.minimum(pe, need)
        slot_ref[:, sl] = jnp.where(sel, pos, -1.0).astype(jnp.int32)
        run_gt = run_gt + count(g)
        run_eq = run_eq + count(q_)
    offs = jnp.where(off_lane == t // OFF_CHUNK, run_gt + jnp.minimum(run_eq, need), offs)
    offs_ref[...] = offs.astype(jnp.int32)


def _select(scores, cap):
    rows, t = scores.shape
    assert t % OFF_CHUNK == 0 and t // OFF_CHUNK < LANES
    return pl.pallas_call(
        functools.partial(_select_kernel, cap=cap),
        out_shape=[jax.ShapeDtypeStruct((rows, t), jnp.int32),
                   jax.ShapeDtypeStruct((rows, LANES), jnp.int32)],
        compiler_params=_params(),
        name="select",
    )(scores)


GATHER_BLOCK = 512
GATHER_WINDOW = 128


def _expert_kernel(offs_ref, slot_ref, sc_ref, h_ref, wg_ref, wu_ref, wd_ref, o_ref,
                   xg_s, gs_s, rows_s, gates_s, *, cap, n_off):
    t = h_ref.shape[0]
    tk = min(GATHER_BLOCK, t)
    win = min(GATHER_WINDOW, cap)
    per_block = tk // OFF_CHUNK
    slot_iota = lax.broadcasted_iota(jnp.int32, (win, tk), 0)
    xg_s[...] = jnp.zeros_like(xg_s)
    gs_s[...] = jnp.zeros_like(gs_s)
    row0 = (pl.program_id(0) * N_EXPERTS + pl.program_id(1)) * n_off

    def gather(sl, lo):
        onehot = (slot_iota + lo == slot_ref[:, sl]).astype(BF16)
        rows = _dot(onehot, h_ref[sl, :])
        gates = lax.dot_general(onehot, sc_ref[:, sl], (((1,), (1,)), ((), ())), preferred_element_type=F32)
        return rows, gates

    n_blocks = t // tk
    starts = [pl.multiple_of((offs_ref[row0 + j * per_block] // SUBLANES) * SUBLANES, SUBLANES)
              for j in range(n_blocks)]
    for j in range(n_blocks):
        rows_s[j], gates_s[j] = gather(slice(j * tk, (j + 1) * tk), starts[j])
    for j in range(n_blocks):
        sl = slice(j * tk, (j + 1) * tk)
        st = starts[j]
        last = offs_ref[row0 + (j + 1) * per_block]
        xg_s[pl.ds(st, SUBLANES), :] += rows_s[j, 0:SUBLANES, :]
        xg_s[pl.ds(st + SUBLANES, win - SUBLANES), :] = rows_s[j, SUBLANES:win, :]
        gs_s[pl.ds(st, win), :] += gates_s[j]

        def more(c, carry, sl=sl, st=st):
            lo = pl.multiple_of(st + c * win, SUBLANES)
            rows, gates = gather(sl, lo)
            xg_s[pl.ds(lo, win), :] = rows
            gs_s[pl.ds(lo, win), :] += gates
            return carry

        lax.fori_loop(1, (last - st + win - 1) // win, more, 0)
    xb = xg_s[0:cap, :].astype(BF16)
    gate = _dot(xb, wg_ref[...])
    hid = (gate * _sigmoid(gate)) * _dot(xb, wu_ref[...])
    y = _dot(hid.astype(BF16), wd_ref[...])
    o_ref[...] = (y * (gs_s[0:cap, 0:1] + gs_s[0:cap, 1:2])).astype(BF16)


def _experts(offs, slot, score_rows, h2, wg, wu, wd, cap):
    b, t, d = h2.shape
    n_off = offs.shape[0] // (b * N_EXPERTS)
    win = min(GATHER_WINDOW, cap)
    grid_spec = pltpu.PrefetchScalarGridSpec(
        num_scalar_prefetch=1,
        grid=(b, N_EXPERTS),
        in_specs=[pl.BlockSpec((None, 1, t), lambda bb, e, o: (bb * N_EXPERTS + e, 0, 0)),
                  pl.BlockSpec((None, None, BF16_ROWS, t), lambda bb, e, o: (bb, e, 0, 0)),
                  pl.BlockSpec((None, t, d), lambda bb, e, o: (bb, 0, 0)),
                  pl.BlockSpec((None, d, D_EXPERT), lambda bb, e, o: (e, 0, 0)),
                  pl.BlockSpec((None, d, D_EXPERT), lambda bb, e, o: (e, 0, 0)),
                  pl.BlockSpec((None, D_EXPERT, d), lambda bb, e, o: (e, 0, 0))],
        out_specs=pl.BlockSpec((None, cap, d), lambda bb, e, o: (bb * N_EXPERTS + e, 0, 0)),
        scratch_shapes=[pltpu.VMEM((cap + win, d), F32), pltpu.VMEM((cap + win, BF16_ROWS), F32),
                        pltpu.VMEM((t // min(GATHER_BLOCK, t), win, d), F32),
                        pltpu.VMEM((t // min(GATHER_BLOCK, t), win, BF16_ROWS), F32)],
    )
    return pl.pallas_call(
        functools.partial(_expert_kernel, cap=cap, n_off=n_off),
        grid_spec=grid_spec,
        out_shape=jax.ShapeDtypeStruct((b * N_EXPERTS, cap, d), BF16),
        compiler_params=_params("parallel", "parallel"),
        name="experts",
    )(offs, slot.reshape(b * N_EXPERTS, 1, t), score_rows, h2, wg, wu, wd)


COMBINE_BLOCK = 1024
SLOT_WINDOW = 256


COMBINE_GROUP = 8


def _combine_kernel(offs_ref, slot_ref, y_ref, x1_ref, mod_ref, g_ref, o_ref, acc_s, oh_s, yw_s, *, cap, n_off):
    tm = x1_ref.shape[0]
    kc = min(SLOT_WINDOW, cap)
    per_block = tm // OFF_CHUNK
    lane_iota = lax.broadcasted_iota(jnp.int32, (tm, kc), 1)

    def place(e, start0, c):
        lo = start0 + c * kc
        st = pl.multiple_of(jnp.minimum(lo, cap - kc), BF16_ROWS)
        onehot = slot_ref[:, e:e + 1] == st + lane_iota
        if not isinstance(c, int) or c > 0:
            onehot = onehot & (slot_ref[:, e:e + 1] >= lo)
        return onehot.astype(BF16), st

    starts, extras = [], []
    for e in range(N_EXPERTS):
        o0 = (pl.program_id(0) * N_EXPERTS + e) * n_off + pl.program_id(1) * per_block
        start0 = (offs_ref[o0] // BF16_ROWS) * BF16_ROWS
        starts.append(start0)
        extras.append(jnp.maximum((offs_ref[o0 + per_block] - start0 + kc - 1) // kc - 1, 0))

    acc = None
    for g0 in range(0, N_EXPERTS, COMBINE_GROUP):
        for i in range(COMBINE_GROUP):
            onehot, st = place(g0 + i, starts[g0 + i], 0)
            oh_s[:, i * kc:(i + 1) * kc] = onehot
            yw_s[i * kc:(i + 1) * kc, :] = y_ref[pl.ds((g0 + i) * cap + st, kc), :]
        part = _dot(oh_s[...], yw_s[...])
        acc = part if acc is None else acc + part

    def finish(total):
        o_ref[...] = _rmsnorm(x1_ref[...] + mod_ref[5:6, :] * total, g_ref[...])

    finish(acc)

    @pl.when(sum(extras) > 0)
    def _():
        acc_s[...] = acc
        for e in range(N_EXPERTS):
            def more(c, carry, e=e):
                onehot, st = place(e, starts[e], c)
                acc_s[...] += _dot(onehot, y_ref[pl.ds(e * cap + st, kc), :])
                return carry
            lax.fori_loop(1, extras[e] + 1, more, 0)
        finish(acc_s[...])


def _combine(offs, slot_t, y, x1, mods, g, cap):
    b, t, d = x1.shape
    tm = min(COMBINE_BLOCK, t)
    n_off = offs.shape[0] // (b * N_EXPERTS)
    grid_spec = pltpu.PrefetchScalarGridSpec(
        num_scalar_prefetch=1,
        grid=(b, t // tm),
        in_specs=[pl.BlockSpec((None, tm, N_EXPERTS), lambda bb, i, o: (bb, i, 0)),
                  pl.BlockSpec((None, N_EXPERTS * cap, d), lambda bb, i, o: (bb, 0, 0),
                               pipeline_mode=pl.Buffered(1)),
                  pl.BlockSpec((None, tm, d), lambda bb, i, o: (bb, i, 0)),
                  pl.BlockSpec((None, SUBLANES, d), lambda bb, i, o: (bb, 0, 0)),
                  pl.BlockSpec((1, d), lambda bb, i, o: (0, 0))],
        out_specs=pl.BlockSpec((None, tm, d), lambda bb, i, o: (bb, i, 0)),
        scratch_shapes=[pltpu.VMEM((tm, d), F32),
                        pltpu.VMEM((tm, COMBINE_GROUP * min(SLOT_WINDOW, cap)), BF16),
                        pltpu.VMEM((COMBINE_GROUP * min(SLOT_WINDOW, cap), d), BF16)],
    )
    return pl.pallas_call(
        functools.partial(_combine_kernel, cap=cap, n_off=n_off),
        grid_spec=grid_spec,
        out_shape=jax.ShapeDtypeStruct((b, t, d), F32),
        compiler_params=_params("parallel", "parallel"),
        name="combine",
    )(offs, slot_t, y, x1, mods, g)


def kernel(x, c, ctx, c_ctx, w_ada, b_ada, norm1_g, norm2_g, w_in, w_four, conv_w, conv_b, lru_lambda, lru_wa, lru_ba, lru_wi, lru_bi, w_lru, w_out, w_router, w_gate_e, w_up_e, w_down_e, final_g):
    b, t, d = x.shape
    assert d == D_MODEL and b == SUBLANES and t % TOKEN_BLOCK == 0 and ctx.shape[1] % SCAN_CHUNK == 0
    assert w_ada.shape[0] == 1, "single-layer problem: the context stream is only read by the recurrence"
    cap = CAPACITY_FACTOR * t // N_EXPERTS
    col_x = D_FOURIER + D_RNN
    col_g = col_x + D_RNN
    col_a = col_g + D_MODEL

    pos = _pos_table(t)
    ct, st = _dft_tables(t)

    cc = jnp.concatenate([c, c_ctx[None], jnp.zeros((SUBLANES - 1, d), F32)], axis=0)
    mods = _ada(cc, w_ada[0], b_ada[0]).reshape(cc.shape[0], N_MOD, d)
    mods = jnp.pad(mods, ((0, 0), (0, SUBLANES - N_MOD), (0, 0)))

    w = w_in[0]
    wfc, wfs = _fold_channel_dft(w[:, :D_FOURIER])
    w_x = w[:, D_FOURIER:col_x].astype(BF16)
    w_cat = jnp.concatenate([w_x, w[:, col_x:].astype(BF16), wfc, wfs], axis=1)
    g1 = norm1_g[0].reshape(1, d)
    zx, zg, za, zb, zfc, zfs = _proj(x, pos, mods, None, g1, w_cat,
                                     (D_RNN, D_RNN, D_MODEL, D_MODEL, D_FOURIER, D_FOURIER))
    (zx_ctx,) = _proj(ctx, None, mods, b, g1, w_x, (D_RNN,))

    for dr in range(2):
        vecs = jnp.concatenate([lru_lambda[0, dr][None], 0.5 * lru_ba[0, dr][None], 0.5 * lru_bi[0, dr][None],
                                conv_b[0][None], conv_w[0]], axis=0)
        wg = (0.5 * jnp.concatenate([lru_wa[0, dr], lru_wi[0, dr]], axis=-1)).astype(BF16)
        if dr == 0:
            u_lat = _conv(zx, vecs)
            u_ctx = _conv(zx_ctx, vecs)
        _, h_ctx = _scan(u_ctx, vecs, wg, jnp.zeros((b, D_RNN), F32), reverse=bool(dr))
        if dr == 0:
            h_fwd, _ = _scan(u_lat, vecs, wg, h_ctx, reverse=False)
        else:
            gr, _ = _scan(u_lat, vecs, wg, h_ctx, reverse=True, hf=h_fwd, zg=zg)

    f = _dft_apply(ct, st, zfc, zfs)

    x1, h2, scores, score_rows = _merge(f, gr, za, zb, x, pos, mods, norm2_g[0].reshape(1, d),
                                        w_four[0].astype(BF16), w_lru[0].astype(BF16), w_out[0].astype(BF16),
                                        w_router[0].T.astype(BF16))

    slot, offs = _select(scores.reshape(b * N_EXPERTS, t), cap)
    offs = offs[:, :t // OFF_CHUNK + 1].reshape(-1)
    y = _experts(offs, slot, score_rows, h2, w_gate_e[0].astype(BF16), w_up_e[0].astype(BF16),
                 w_down_e[0].astype(BF16), cap)
    slot_t = jnp.swapaxes(slot.reshape(b, N_EXPERTS, t), 1, 2)
    return _combine(offs, slot_t, y.reshape(b, N_EXPERTS * cap, d), x1, mods, final_g.reshape(1, d), cap)
```

```python
import functools
import math

import numpy as np
import jax
import jax.numpy as jnp
from jax import lax
from jax.experimental import pallas as pl
from jax.experimental.pallas import tpu as pltpu

F32 = jnp.float32
BF16 = jnp.bfloat16

D_MODEL = 1024
GRID_W = 64
N_FOURIER_GROUPS = 4
FOURIER_GROUP = 128
D_FOURIER = N_FOURIER_GROUPS * FOURIER_GROUP
D_RNN = D_MODEL
N_RNN_HEADS = 8
RNN_HEAD = D_RNN // N_RNN_HEADS
CONV_W = 4
CONV_LEFT = 2
LRU_C = 8.0
LOG2E = math.log2(math.e)
N_EXPERTS = 16
CAPACITY_FACTOR = 2
D_EXPERT = 1536
N_MOD = 6
EPS = 1e-6
POS_MAX_PERIOD = 10000.0

LANES = 128
SUBLANES = 8
BF16_ROWS = 16
VMEM_LIMIT = 56 * 1024 * 1024

TOKEN_BLOCK = 512
SCAN_CHUNK = 128
SCAN_STRIDE = SCAN_CHUNK + SUBLANES


def _params(*sem):
    return pltpu.CompilerParams(dimension_semantics=sem, vmem_limit_bytes=VMEM_LIMIT)


def _sigmoid(x):
    return 0.5 * jnp.tanh(0.5 * x) + 0.5


def _rmsnorm(x, g):
    return x * lax.rsqrt(jnp.mean(x * x, axis=-1, keepdims=True) + EPS) * g


def _dot(a, b):
    return jnp.dot(a, b, preferred_element_type=F32)


def _pos_kernel(o_ref):
    rows = o_ref.shape[0]
    q = D_MODEL // 4
    k = lax.broadcasted_iota(jnp.int32, (1, q), 1).astype(F32)
    freqs = jnp.exp(-math.log(POS_MAX_PERIOD) * k / q)
    ang_r = lax.broadcasted_iota(jnp.int32, (rows, q), 0).astype(F32) * freqs
    ang_c = lax.broadcasted_iota(jnp.int32, (GRID_W, q), 0).astype(F32) * freqs
    shape = (rows, GRID_W, q)
    o_ref[:, :, 0 * q:1 * q] = jnp.broadcast_to(jnp.sin(ang_r)[:, None, :], shape)
    o_ref[:, :, 1 * q:2 * q] = jnp.broadcast_to(jnp.cos(ang_r)[:, None, :], shape)
    o_ref[:, :, 2 * q:3 * q] = jnp.broadcast_to(jnp.sin(ang_c)[None, :, :], shape)
    o_ref[:, :, 3 * q:4 * q] = jnp.broadcast_to(jnp.cos(ang_c)[None, :, :], shape)


def _pos_table(n_tokens):
    rows = n_tokens // GRID_W
    out = pl.pallas_call(
        _pos_kernel,
        out_shape=jax.ShapeDtypeStruct((rows, GRID_W, D_MODEL), F32),
        name="pos_table",
    )()
    return out.reshape(n_tokens, D_MODEL)


RADIX = 8


def _dft_kernel(c_ref, s_ref, *, n):
    ns = c_ref.shape[0]
    k = RADIX * lax.broadcasted_iota(jnp.int32, (ns, ns), 0) + pl.program_id(0)
    t2 = lax.broadcasted_iota(jnp.int32, (ns, ns), 1)
    ang = ((k * t2) & (n - 1)).astype(F32) * (2.0 * math.pi / n)
    scale = 1.0 / math.sqrt(n)
    c_ref[...] = (jnp.cos(ang) * scale).astype(BF16)
    s_ref[...] = (jnp.sin(ang) * scale).astype(BF16)


def _dft_tables(n):
    assert n & (n - 1) == 0 and n % (RADIX * BF16_ROWS) == 0
    ns = n // RADIX
    return pl.pallas_call(
        functools.partial(_dft_kernel, n=n),
        grid=(RADIX,),
        out_specs=[pl.BlockSpec((None, ns, ns), lambda i: (i, 0, 0))] * 2,
        out_shape=[jax.ShapeDtypeStruct((RADIX, ns, ns), BF16)] * 2,
        compiler_params=_params("parallel"),
        name="dft_tables",
    )()


def _fold_kernel(w_ref, c_ref, s_ref, oc_ref, os_ref):
    for g in range(N_FOURIER_GROUPS):
        sl = slice(g * FOURIER_GROUP, (g + 1) * FOURIER_GROUP)
        w = w_ref[:, sl]
        oc_ref[:, sl] = jnp.dot(w, c_ref[...], preferred_element_type=F32,
                                precision=lax.Precision.HIGHEST).astype(BF16)
        os_ref[:, sl] = jnp.dot(w, s_ref[...], preferred_element_type=F32,
                                precision=lax.Precision.HIGHEST).astype(BF16)


def _fold_channel_dft(w_f):
    p = np.arange(FOURIER_GROUP)
    ang = 2.0 * np.pi * ((p[:, None] * p[None, :]) % FOURIER_GROUP) / FOURIER_GROUP
    cp = jnp.asarray(np.cos(ang) / math.sqrt(FOURIER_GROUP), F32)
    sp = jnp.asarray(np.sin(ang) / math.sqrt(FOURIER_GROUP), F32)
    return pl.pallas_call(
        _fold_kernel,
        out_shape=[jax.ShapeDtypeStruct(w_f.shape, BF16)] * 2,
        compiler_params=_params(),
        name="fold_channel_dft",
    )(w_f, cp, sp)


def _ada_kernel(c_ref, w_ref, b_ref, o_ref):
    c = c_ref[...]
    o_ref[...] = jnp.dot(c * _sigmoid(c), w_ref[...], preferred_element_type=F32,
                         precision=lax.Precision.HIGHEST) + b_ref[...]


def _ada(cc, w_ada, b_ada):
    n = w_ada.shape[1]
    tn = D_MODEL
    return pl.pallas_call(
        _ada_kernel,
        grid=(n // tn,),
        in_specs=[pl.BlockSpec(cc.shape, lambda j: (0, 0)),
                  pl.BlockSpec((D_MODEL, tn), lambda j: (0, j)),
                  pl.BlockSpec((1, tn), lambda j: (0, j))],
        out_specs=pl.BlockSpec((cc.shape[0], tn), lambda j: (0, j)),
        out_shape=jax.ShapeDtypeStruct((cc.shape[0], n), F32),
        compiler_params=_params("parallel"),
        name="ada",
    )(cc, w_ada, b_ada.reshape(1, n))


def _proj_kernel(*refs, widths, has_pos):
    if has_pos:
        x_ref, pos_ref, mod_ref, g_ref, w_ref = refs[:5]
        outs = refs[5:]
        x = x_ref[...] + pos_ref[...]
    else:
        x_ref, mod_ref, g_ref, w_ref = refs[:4]
        outs = refs[4:]
        x = x_ref[...]
    h = _rmsnorm(x, g_ref[...]) * (1.0 + mod_ref[1:2, :]) + mod_ref[0:1, :]
    hb = h.astype(BF16)
    c0 = 0
    for o_ref, wd in zip(outs, widths):
        o_ref[...] = _dot(hb, w_ref[:, c0:c0 + wd]).astype(BF16)
        c0 += wd


def _proj(x, pos, mods, mod_row, g, w, widths):
    b, t, d = x.shape
    tm = min(TOKEN_BLOCK, t)
    has_pos = pos is not None
    in_specs = [pl.BlockSpec((None, tm, d), lambda i, bb: (bb, i, 0))]
    args = [x]
    if has_pos:
        in_specs.append(pl.BlockSpec((tm, d), lambda i, bb: (i, 0)))
        args.append(pos)
    if mod_row is None:
        mod_map = lambda i, bb: (bb, 0, 0)
    else:
        mod_map = lambda i, bb: (mod_row, 0, 0)
    in_specs += [pl.BlockSpec((None, SUBLANES, d), mod_map),
                 pl.BlockSpec((1, d), lambda i, bb: (0, 0)),
                 pl.BlockSpec(w.shape, lambda i, bb: (0, 0))]
    args += [mods, g, w]
    return pl.pallas_call(
        functools.partial(_proj_kernel, widths=widths, has_pos=has_pos),
        grid=(t // tm, b),
        in_specs=in_specs,
        out_specs=[pl.BlockSpec((None, tm, wd), lambda i, bb: (bb, i, 0)) for wd in widths],
        out_shape=[jax.ShapeDtypeStruct((b, t, wd), BF16) for wd in widths],
        compiler_params=_params("parallel", "parallel"),
        name="proj",
    )(*args)


def _gelu(x):
    return 0.5 * x * (1.0 + jnp.tanh(math.sqrt(2.0 / math.pi) * (x + 0.044715 * x * x * x)))


def _conv_kernel(x_ref, xp_ref, xn_ref, vec_ref, *rest):
    if len(rest) == 3:
        w_ref, o_ref, wb_ref = rest
        wb_ref[...] = w_ref[...].astype(BF16)
    else:
        (o_ref,) = rest
    tm = x_ref.shape[0]
    edge = BF16_ROWS
    i = pl.program_id(0)
    first = i == 0
    last = i == pl.num_programs(0) - 1
    w = [vec_ref[4 + kk:5 + kk, :] for kk in range(CONV_W)]
    bias = vec_ref[3:4, :]

    def taps(xm2, xm1, x0, xp1):
        return bias + w[0] * xm2 + w[1] * xm1 + w[2] * x0 + w[3] * xp1

    x = x_ref[...].astype(F32)
    o_ref[...] = taps(pltpu.roll(x, 2, 0), pltpu.roll(x, 1, 0), x, pltpu.roll(x, tm - 1, 0)).astype(BF16)

    row = lax.broadcasted_iota(jnp.int32, (edge, x.shape[1]), 0)

    def shifted(before, cur, after, s):
        if s < 0:
            return jnp.where(row < -s, pltpu.roll(before, -s, 0), pltpu.roll(cur, -s, 0))
        return jnp.where(row < edge - s, pltpu.roll(cur, edge - s, 0), pltpu.roll(after, edge - s, 0))

    prev = jnp.where(first, 0.0, xp_ref[...].astype(F32))
    nxt = jnp.where(last, 0.0, xn_ref[...].astype(F32))
    head, head2 = x[0:edge], x[edge:2 * edge]
    tail, tail2 = x[tm - edge:tm], x[tm - 2 * edge:tm - edge]
    o_ref[0:edge, :] = taps(shifted(prev, head, head2, -2), shifted(prev, head, head2, -1), head,
                            shifted(prev, head, head2, 1)).astype(BF16)
    o_ref[tm - edge:tm, :] = taps(shifted(tail2, tail, nxt, -2), shifted(tail2, tail, nxt, -1), tail,
                                  shifted(tail2, tail, nxt, 1)).astype(BF16)


def _conv(zx, vecs, cast=None):
    b, t, c = zx.shape
    tm = min(TOKEN_BLOCK, t)
    assert tm >= 2 * BF16_ROWS
    hpb = tm // BF16_ROWS
    n_halo = t // BF16_ROWS
    in_specs = [pl.BlockSpec((None, tm, c), lambda i, bb: (bb, i, 0)),
                pl.BlockSpec((None, BF16_ROWS, c), lambda i, bb: (bb, jnp.maximum(i * hpb - 1, 0), 0)),
                pl.BlockSpec((None, BF16_ROWS, c), lambda i, bb: (bb, jnp.minimum((i + 1) * hpb, n_halo - 1), 0)),
                pl.BlockSpec(vecs.shape, lambda i, bb: (0, 0))]
    args = [zx, zx, zx, vecs]
    out_specs = [pl.BlockSpec((None, tm, c), lambda i, bb: (bb, i, 0))]
    out_shape = [jax.ShapeDtypeStruct((b, t, c), BF16)]
    if cast is not None:
        spec, shape = _cast_specs(cast, (t // tm) * b, lambda i, bb: i * b + bb)
        in_specs.append(spec)
        args.append(cast)
        out_specs.append(spec)
        out_shape.append(shape)
    return pl.pallas_call(
        _conv_kernel,
        grid=(t // tm, b),
        in_specs=in_specs,
        out_specs=out_specs,
        out_shape=out_shape,
        compiler_params=_params("parallel", "parallel"),
        name="conv",
    )(*args)


def _scan_kernel(*refs, reverse, fuse, cast):
    refs = list(refs)
    u_ref, vec_ref, wg_ref, h0_ref = refs[:4]
    del refs[:4]
    if fuse:
        hf_ref, zg_ref = refs[:2]
        del refs[:2]
    if cast:
        w_ref = refs.pop(0)
        o_ref, hT_ref, wb_ref, a_s, b_s, h_s, hc_s = refs
        wb_ref[...] = w_ref[...].astype(BF16)
    else:
        o_ref, hT_ref, a_s, b_s, h_s, hc_s = refs
    nb, tc, c = u_ref.shape

    @pl.when(pl.program_id(0) == 0)
    def _():
        for hh in range(N_RNN_HEADS):
            hc_s[hh] = h0_ref[:, hh * RNN_HEAD:(hh + 1) * RNN_HEAD]

    lam = vec_ref[0:1, :]
    half_b_a = vec_ref[1:2, :]
    half_b_i = vec_ref[2:3, :]
    k_row = (-0.5 * LRU_C * LOG2E) * (jnp.maximum(-lam, 0.0) + jnp.log1p(jnp.exp(-jnp.abs(lam))))

    for bb in range(nb):
        ub = u_ref[bb]
        u = ub.astype(F32)
        row0 = bb * SCAN_STRIDE
        for hh in range(N_RNN_HEADS):
            sl = slice(hh * RNN_HEAD, (hh + 1) * RNN_HEAD)
            gates = _dot(ub[:, sl], wg_ref[hh])
            t_r = jnp.tanh(gates[:, :RNN_HEAD] + half_b_a[:, sl])
            t_i = jnp.tanh(gates[:, RNN_HEAD:] + half_b_i[:, sl])
            a = jnp.exp2(k_row[:, sl] * t_r + k_row[:, sl])
            a_s[hh, row0:row0 + tc, :] = a
            scale = jnp.exp2((0.5 * LOG2E) * jnp.log(1.0 - a * a)) * u[:, sl]
            b_s[hh, row0:row0 + tc, :] = scale * (0.5 * t_i + 0.5)

    def body(j, hs):
        t = (tc - 1 - j) if reverse else j
        idx = pl.ds(t, nb, stride=SCAN_STRIDE)
        new = []
        for hh in range(N_RNN_HEADS):
            h = a_s[hh, idx, :] * hs[hh] + b_s[hh, idx, :]
            h_s[hh, idx, :] = h
            new.append(h)
        return tuple(new)

    hs = lax.fori_loop(0, tc, body, tuple(hc_s[hh] for hh in range(N_RNN_HEADS)), unroll=8)
    for hh in range(N_RNN_HEADS):
        hc_s[hh] = hs[hh]
        hT_ref[:, hh * RNN_HEAD:(hh + 1) * RNN_HEAD] = hs[hh]

    for bb in range(nb):
        row0 = bb * SCAN_STRIDE
        for hh in range(N_RNN_HEADS):
            sl = slice(hh * RNN_HEAD, (hh + 1) * RNN_HEAD)
            hb = h_s[hh, row0:row0 + tc, :]
            if fuse:
                hb = _gelu(zg_ref[bb, :, sl].astype(F32)) * (hf_ref[bb, :, sl].astype(F32) + hb)
            o_ref[bb, :, sl] = hb.astype(BF16)


def _cast_specs(w, n_steps, step_index):
    rows, cols = w.shape
    assert rows % (n_steps * BF16_ROWS) == 0
    spec = pl.BlockSpec((rows // n_steps, cols), lambda *idx: (step_index(*idx), 0))
    return spec, jax.ShapeDtypeStruct((rows, cols), BF16)


def _scan(u, vecs, wg, h0, *, reverse, hf=None, zg=None, cast=None):
    nb, t, c = u.shape
    tc = SCAN_CHUNK
    n_chunks = t // tc
    fuse = hf is not None

    def cidx(i):
        return (n_chunks - 1 - i) if reverse else i

    main_spec = pl.BlockSpec((nb, tc, c), lambda i: (0, cidx(i), 0))
    in_specs = [
        main_spec,
        pl.BlockSpec(vecs.shape, lambda i: (0, 0)),
        pl.BlockSpec(wg.shape, lambda i: (0, 0, 0)),
        pl.BlockSpec(h0.shape, lambda i: (0, 0)),
    ]
    args = [u, vecs, wg, h0]
    if fuse:
        in_specs += [main_spec, main_spec]
        args += [hf, zg]
    out_specs = [main_spec, pl.BlockSpec((nb, c), lambda i: (0, 0))]
    out_shape = [jax.ShapeDtypeStruct((nb, t, c), BF16), jax.ShapeDtypeStruct((nb, c), F32)]
    if cast is not None:
        spec, shape = _cast_specs(cast, n_chunks, lambda i: i)
        in_specs.append(spec)
        args.append(cast)
        out_specs.append(spec)
        out_shape.append(shape)
    return pl.pallas_call(
        functools.partial(_scan_kernel, reverse=reverse, fuse=fuse, cast=cast is not None),
        grid=(n_chunks,),
        in_specs=in_specs,
        out_specs=out_specs,
        out_shape=out_shape,
        scratch_shapes=[pltpu.VMEM((N_RNN_HEADS, nb * SCAN_STRIDE, RNN_HEAD), F32),
                        pltpu.VMEM((N_RNN_HEADS, nb * SCAN_STRIDE, RNN_HEAD), F32),
                        pltpu.VMEM((N_RNN_HEADS, nb * SCAN_STRIDE, RNN_HEAD), F32),
                        pltpu.VMEM((N_RNN_HEADS, nb, RNN_HEAD), F32)],
        compiler_params=_params("arbitrary"),
        name="scan_bwd" if reverse else "scan_fwd",
    )(*args)


DFT_ROWS = 64
DFT_COLS = 256


def _dftmm_kernel(zc_ref, zs_ref, c_ref, s_ref, o_ref, ar_s, ai_s, o_s):
    t, f = zc_ref.shape
    ns = t // RADIX
    rb = min(DFT_ROWS, ns)
    r2 = math.sqrt(0.5)

    def dft4(x):
        s0 = (x[0][0] + x[2][0], x[0][1] + x[2][1])
        s1 = (x[0][0] - x[2][0], x[0][1] - x[2][1])
        s2 = (x[1][0] + x[3][0], x[1][1] + x[3][1])
        s3 = (x[1][0] - x[3][0], x[1][1] - x[3][1])
        return [(s0[0] + s2[0], s0[1] + s2[1]), (s1[0] + s3[1], s1[1] - s3[0]),
                (s0[0] - s2[0], s0[1] - s2[1]), (s1[0] - s3[1], s1[1] + s3[0])]

    def butterfly(i, carry):
        r0 = pl.multiple_of(i * rb, rb)
        w = [(zc_ref[pl.ds(j * ns + r0, rb), :].astype(F32), -zs_ref[pl.ds(j * ns + r0, rb), :].astype(F32))
             for j in range(RADIX)]
        u = [(w[j][0] + w[j + 4][0], w[j][1] + w[j + 4][1]) for j in range(4)]
        v = [(w[j][0] - w[j + 4][0], w[j][1] - w[j + 4][1]) for j in range(4)]
        v = [v[0],
             ((v[1][0] + v[1][1]) * r2, (v[1][1] - v[1][0]) * r2),
             (v[2][1], -v[2][0]),
             ((v[3][1] - v[3][0]) * r2, (-v[3][0] - v[3][1]) * r2)]
        even = dft4(u)
        odd = dft4(v)
        for m in range(4):
            for k1, a in ((2 * m, even[m]), (2 * m + 1, odd[m])):
                ar_s[k1, pl.ds(r0, rb), :] = a[0].astype(BF16)
                ai_s[k1, pl.ds(r0, rb), :] = a[1].astype(BF16)
        return carry

    lax.fori_loop(0, ns // rb, butterfly, 0)

    for c0 in range(0, f, DFT_COLS):
        for k1 in range(RADIX):
            res = (_dot(c_ref[k1], ar_s[k1, :, c0:c0 + DFT_COLS])
                   + _dot(s_ref[k1], ai_s[k1, :, c0:c0 + DFT_COLS]))
            for j in range(DFT_COLS // LANES):
                o_s[j, pl.ds(k1, ns, stride=RADIX), :] = res[:, j * LANES:(j + 1) * LANES]
        for j in range(DFT_COLS // LANES):
            o_ref[:, c0 + j * LANES:c0 + (j + 1) * LANES] = o_s[j].astype(BF16)


def _dft_apply(ct, st, zfc, zfs):
    b, t, f = zfc.shape
    ns = t // RADIX
    table = pl.BlockSpec((RADIX, ns, ns), lambda bb: (0, 0, 0), pipeline_mode=pl.Buffered(1))
    return pl.pallas_call(
        _dftmm_kernel,
        grid=(b,),
        in_specs=[pl.BlockSpec((None, t, f), lambda bb: (bb, 0, 0)),
                  pl.BlockSpec((None, t, f), lambda bb: (bb, 0, 0)),
                  table, table],
        out_specs=pl.BlockSpec((None, t, f), lambda bb: (bb, 0, 0)),
        out_shape=jax.ShapeDtypeStruct((b, t, f), BF16),
        scratch_shapes=[pltpu.VMEM((RADIX, ns, f), BF16),
                        pltpu.VMEM((RADIX, ns, f), BF16),
                        pltpu.VMEM((DFT_COLS // LANES, t, LANES), F32)],
        compiler_params=_params("parallel"),
        name="dft_apply",
    )(zfc, zfs, ct, st)


def _merge_kernel(f_ref, gr_ref, za_ref, zb_ref, x_ref, pos_ref, mod_ref, g2_ref,
                  wf_ref, wl_ref, wo_ref, wr_ref, x1_ref, h2_ref, sc_ref, scr_ref):
    y_four = _dot(f_ref[...], wf_ref[...])
    y_rnn = _dot(gr_ref[...], wl_ref[...])
    m = _sigmoid(za_ref[...].astype(F32)) * y_four + _sigmoid(zb_ref[...].astype(F32)) * y_rnn
    y = _dot(m.astype(BF16), wo_ref[...])
    x1 = (x_ref[...] + pos_ref[...]) + mod_ref[2:3, :] * y
    x1_ref[...] = x1
    h2 = (_rmsnorm(x1, g2_ref[...]) * (1.0 + mod_ref[4:5, :]) + mod_ref[3:4, :]).astype(BF16)
    h2_ref[...] = h2
    logits = lax.dot_general(wr_ref[...], h2, (((1,), (1,)), ((), ())), preferred_element_type=F32)
    e = jnp.exp(logits - jnp.max(logits, axis=0, keepdims=True))
    sc = e / jnp.sum(e, axis=0, keepdims=True)
    sc_ref[...] = sc
    hi = sc.astype(BF16).astype(F32)
    lo = sc - hi
    tm = sc.shape[1]
    row = lax.broadcasted_iota(jnp.int32, (BF16_ROWS, tm), 0)
    for ex in range(N_EXPERTS):
        tile = jnp.where(row == 0, hi[ex:ex + 1, :], jnp.where(row == 1, lo[ex:ex + 1, :], 0.0))
        scr_ref[ex] = tile.astype(BF16)


def _merge(f, gr, za, zb, x, pos, mods, g2, w_four, w_lru, w_out, w_rt):
    b, t, d = x.shape
    tm = min(TOKEN_BLOCK, t)
    tok = lambda wd: pl.BlockSpec((None, tm, wd), lambda i, bb: (bb, i, 0))
    full = lambda a: pl.BlockSpec(a.shape, lambda i, bb: (0,) * a.ndim)
    return pl.pallas_call(
        _merge_kernel,
        grid=(t // tm, b),
        in_specs=[tok(D_FOURIER), tok(d), tok(d), tok(d), tok(d),
                  pl.BlockSpec((tm, d), lambda i, bb: (i, 0)),
                  pl.BlockSpec((None, SUBLANES, d), lambda i, bb: (bb, 0, 0)),
                  full(g2), full(w_four), full(w_lru), full(w_out), full(w_rt)],
        out_specs=[tok(d), tok(d), pl.BlockSpec((None, N_EXPERTS, tm), lambda i, bb: (bb, 0, i)),
                   pl.BlockSpec((None, N_EXPERTS, BF16_ROWS, tm), lambda i, bb: (bb, 0, 0, i))],
        out_shape=[jax.ShapeDtypeStruct((b, t, d), F32),
                   jax.ShapeDtypeStruct((b, t, d), BF16),
                   jax.ShapeDtypeStruct((b, N_EXPERTS, t), F32),
                   jax.ShapeDtypeStruct((b, N_EXPERTS, BF16_ROWS, t), BF16)],
        compiler_params=_params("parallel", "parallel"),
        name="merge",
    )(f, gr, za, zb, x, pos, mods, g2, w_four, w_lru, w_out, w_rt)


MAX_EXP = 127
EXP_STEPS = 7
VAL_STEPS = 56


OFF_CHUNK = 256


def _select_kernel(s_ref, slot_ref, offs_ref, *, cap):
    rows, t = s_ref.shape

    def count(mask):
        return jnp.sum(jnp.where(mask, 1.0, 0.0), axis=1, keepdims=True)

    def exp_step(_, carry):
        k_lo, k_hi = carry
        k_mid = jnp.floor(0.5 * (k_lo + k_hi))
        ok = count(s_ref[...] >= jnp.exp2(-k_mid)) >= cap
        return jnp.where(ok, k_lo, k_mid), jnp.where(ok, k_mid, k_hi)

    k_lo, k_hi = lax.fori_loop(
        0, EXP_STEPS, exp_step,
        (jnp.full((rows, 1), -1.0, F32), jnp.full((rows, 1), float(MAX_EXP), F32)))
    lo0 = jnp.where(k_hi >= MAX_EXP, 0.0, jnp.exp2(-k_hi))
    hi0 = jnp.exp2(-k_lo)

    def val_step(_, carry):
        lo, hi = carry
        mid = 0.5 * (lo + hi)
        ok = count(s_ref[...] >= mid) >= cap
        return jnp.where(ok, mid, lo), jnp.where(ok, hi, mid)

    lo, hi = lax.fori_loop(0, VAL_STEPS, val_step, (lo0, hi0))
    s = s_ref[...]
    gt = s >= hi
    eq = (s >= lo) & (s < hi)
    need = cap - count(gt)

    p = lax.broadcasted_iota(jnp.int32, (LANES, LANES), 0)
    q = lax.broadcasted_iota(jnp.int32, (LANES, LANES), 1)
    tri = (p < q).astype(BF16)
    run_gt = jnp.zeros((rows, 1), F32)
    run_eq = jnp.zeros((rows, 1), F32)
    off_lane = lax.broadcasted_iota(jnp.int32, (rows, LANES), 1)
    offs = jnp.zeros((rows, LANES), F32)
    for j in range(t // LANES):
        if (j * LANES) % OFF_CHUNK == 0:
            offs = jnp.where(off_lane == (j * LANES) // OFF_CHUNK, run_gt + jnp.minimum(run_eq, need), offs)
        sl = slice(j * LANES, (j + 1) * LANES)
        g = gt[:, sl]
        q_ = eq[:, sl]
        pg = _dot(g.astype(BF16), tri) + run_gt
        pe = _dot(q_.astype(BF16), tri) + run_eq
        sel = g | (q_ & (pe < need))
        pos = pg + jnp.minimum(pe, need)
        slot_ref[:, sl] = jnp.where(sel, pos, -1.0).astype(jnp.int32)
        run_gt = run_gt + count(g)
        run_eq = run_eq + count(q_)
    offs = jnp.where(off_lane == t // OFF_CHUNK, run_gt + jnp.minimum(run_eq, need), offs)
    offs_ref[...] = offs.astype(jnp.int32)


def _select(scores, cap):
    rows, t = scores.shape
    assert t % OFF_CHUNK == 0 and t // OFF_CHUNK < LANES
    return pl.pallas_call(
        functools.partial(_select_kernel, cap=cap),
        out_shape=[jax.ShapeDtypeStruct((rows, t), jnp.int32),
                   jax.ShapeDtypeStruct((rows, LANES), jnp.int32)],
        compiler_params=_params(),
        name="select",
    )(scores)


GATHER_BLOCK = 512
GATHER_WINDOW = 128


def _expert_kernel(offs_ref, slot_ref, sc_ref, h_ref, wg_ref, wu_ref, wd_ref, o_ref,
                   xg_s, gs_s, rows_s, gates_s, *, cap, n_off):
    t = h_ref.shape[0]
    tk = min(GATHER_BLOCK, t)
    win = min(GATHER_WINDOW, cap)
    per_block = tk // OFF_CHUNK
    slot_iota = lax.broadcasted_iota(jnp.int32, (win, tk), 0)
    xg_s[...] = jnp.zeros_like(xg_s)
    gs_s[...] = jnp.zeros_like(gs_s)
    row0 = (pl.program_id(0) * N_EXPERTS + pl.program_id(1)) * n_off

    def gather(sl, lo):
        onehot = (slot_iota + lo == slot_ref[:, sl]).astype(BF16)
        rows = _dot(onehot, h_ref[sl, :])
        gates = lax.dot_general(onehot, sc_ref[:, sl], (((1,), (1,)), ((), ())), preferred_element_type=F32)
        return rows, gates

    n_blocks = t // tk
    starts = [pl.multiple_of((offs_ref[row0 + j * per_block] // SUBLANES) * SUBLANES, SUBLANES)
              for j in range(n_blocks)]
    for j in range(n_blocks):
        rows_s[j], gates_s[j] = gather(slice(j * tk, (j + 1) * tk), starts[j])
    for j in range(n_blocks):
        sl = slice(j * tk, (j + 1) * tk)
        st = starts[j]
        last = offs_ref[row0 + (j + 1) * per_block]
        xg_s[pl.ds(st, SUBLANES), :] += rows_s[j, 0:SUBLANES, :]
        xg_s[pl.ds(st + SUBLANES, win - SUBLANES), :] = rows_s[j, SUBLANES:win, :]
        gs_s[pl.ds(st, win), :] += gates_s[j]

        def more(c, carry, sl=sl, st=st):
            lo = pl.multiple_of(st + c * win, SUBLANES)
            rows, gates = gather(sl, lo)
            xg_s[pl.ds(lo, win), :] = rows
            gs_s[pl.ds(lo, win), :] += gates
            return carry

        lax.fori_loop(1, (last - st + win - 1) // win, more, 0)
    xb = xg_s[0:cap, :].astype(BF16)
    gate = _dot(xb, wg_ref[...])
    hid = (gate * _sigmoid(gate)) * _dot(xb, wu_ref[...])
    y = _dot(hid.astype(BF16), wd_ref[...])
    o_ref[...] = (y * (gs_s[0:cap, 0:1] + gs_s[0:cap, 1:2])).astype(BF16)


def _experts(offs, slot, score_rows, h2, wg, wu, wd, cap):
    b, t, d = h2.shape
    n_off = offs.shape[0] // (b * N_EXPERTS)
    win = min(GATHER_WINDOW, cap)
    grid_spec = pltpu.PrefetchScalarGridSpec(
        num_scalar_prefetch=1,
        grid=(b, N_EXPERTS),
        in_specs=[pl.BlockSpec((None, 1, t), lambda bb, e, o: (bb * N_EXPERTS + e, 0, 0)),
                  pl.BlockSpec((None, None, BF16_ROWS, t), lambda bb, e, o: (bb, e, 0, 0)),
                  pl.BlockSpec((None, t, d), lambda bb, e, o: (bb, 0, 0)),
                  pl.BlockSpec((None, d, D_EXPERT), lambda bb, e, o: (e, 0, 0)),
                  pl.BlockSpec((None, d, D_EXPERT), lambda bb, e, o: (e, 0, 0)),
                  pl.BlockSpec((None, D_EXPERT, d), lambda bb, e, o: (e, 0, 0))],
        out_specs=pl.BlockSpec((None, cap, d), lambda bb, e, o: (bb * N_EXPERTS + e, 0, 0)),
        scratch_shapes=[pltpu.VMEM((cap + win, d), F32), pltpu.VMEM((cap + win, BF16_ROWS), F32),
                        pltpu.VMEM((t // min(GATHER_BLOCK, t), win, d), F32),
                        pltpu.VMEM((t // min(GATHER_BLOCK, t), win, BF16_ROWS), F32)],
    )
    return pl.pallas_call(
        functools.partial(_expert_kernel, cap=cap, n_off=n_off),
        grid_spec=grid_spec,
        out_shape=jax.ShapeDtypeStruct((b * N_EXPERTS, cap, d), BF16),
        compiler_params=_params("parallel", "parallel"),
        name="experts",
    )(offs, slot.reshape(b * N_EXPERTS, 1, t), score_rows, h2, wg, wu, wd)


COMBINE_BLOCK = 1024
SLOT_WINDOW = 256


COMBINE_GROUP = 8


def _combine_kernel(offs_ref, slot_ref, y_ref, x1_ref, mod_ref, g_ref, o_ref, acc_s, oh_s, yw_s, *, cap, n_off):
    tm = x1_ref.shape[0]
    kc = min(SLOT_WINDOW, cap)
    per_block = tm // OFF_CHUNK
    lane_iota = lax.broadcasted_iota(jnp.int32, (tm, kc), 1)

    def place(e, start0, c):
        lo = start0 + c * kc
        st = pl.multiple_of(jnp.minimum(lo, cap - kc), BF16_ROWS)
        onehot = slot_ref[:, e:e + 1] == st + lane_iota
        if not isinstance(c, int) or c > 0:
            onehot = onehot & (slot_ref[:, e:e + 1] >= lo)
        return onehot.astype(BF16), st

    starts, extras = [], []
    for e in range(N_EXPERTS):
        o0 = (pl.program_id(0) * N_EXPERTS + e) * n_off + pl.program_id(1) * per_block
        start0 = (offs_ref[o0] // BF16_ROWS) * BF16_ROWS
        starts.append(start0)
        extras.append(jnp.maximum((offs_ref[o0 + per_block] - start0 + kc - 1) // kc - 1, 0))

    acc = None
    for g0 in range(0, N_EXPERTS, COMBINE_GROUP):
        for i in range(COMBINE_GROUP):
            onehot, st = place(g0 + i, starts[g0 + i], 0)
            oh_s[:, i * kc:(i + 1) * kc] = onehot
            yw_s[i * kc:(i + 1) * kc, :] = y_ref[pl.ds((g0 + i) * cap + st, kc), :]
        part = _dot(oh_s[...], yw_s[...])
        acc = part if acc is None else acc + part

    def finish(total):
        o_ref[...] = _rmsnorm(x1_ref[...] + mod_ref[5:6, :] * total, g_ref[...])

    finish(acc)

    @pl.when(sum(extras) > 0)
    def _():
        acc_s[...] = acc
        for e in range(N_EXPERTS):
            def more(c, carry, e=e):
                onehot, st = place(e, starts[e], c)
                acc_s[...] += _dot(onehot, y_ref[pl.ds(e * cap + st, kc), :])
                return carry
            lax.fori_loop(1, extras[e] + 1, more, 0)
        finish(acc_s[...])


def _combine(offs, slot_t, y, x1, mods, g, cap):
    b, t, d = x1.shape
    tm = min(COMBINE_BLOCK, t)
    n_off = offs.shape[0] // (b * N_EXPERTS)
    grid_spec = pltpu.PrefetchScalarGridSpec(
        num_scalar_prefetch=1,
        grid=(b, t // tm),
        in_specs=[pl.BlockSpec((None, tm, N_EXPERTS), lambda bb, i, o: (bb, i, 0)),
                  pl.BlockSpec((None, N_EXPERTS * cap, d), lambda bb, i, o: (bb, 0, 0),
                               pipeline_mode=pl.Buffered(1)),
                  pl.BlockSpec((None, tm, d), lambda bb, i, o: (bb, i, 0)),
                  pl.BlockSpec((None, SUBLANES, d), lambda bb, i, o: (bb, 0, 0)),
                  pl.BlockSpec((1, d), lambda bb, i, o: (0, 0))],
        out_specs=pl.BlockSpec((None, tm, d), lambda bb, i, o: (bb, i, 0)),
        scratch_shapes=[pltpu.VMEM((tm, d), F32),
                        pltpu.VMEM((tm, COMBINE_GROUP * min(SLOT_WINDOW, cap)), BF16),
                        pltpu.VMEM((COMBINE_GROUP * min(SLOT_WINDOW, cap), d), BF16)],
    )
    return pl.pallas_call(
        functools.partial(_combine_kernel, cap=cap, n_off=n_off),
        grid_spec=grid_spec,
        out_shape=jax.ShapeDtypeStruct((b, t, d), F32),
        compiler_params=_params("parallel", "parallel"),
        name="combine",
    )(offs, slot_t, y, x1, mods, g)


def kernel(x, c, ctx, c_ctx, w_ada, b_ada, norm1_g, norm2_g, w_in, w_four, conv_w, conv_b, lru_lambda, lru_wa, lru_ba, lru_wi, lru_bi, w_lru, w_out, w_router, w_gate_e, w_up_e, w_down_e, final_g):
    b, t, d = x.shape
    assert d == D_MODEL and b == SUBLANES and t % TOKEN_BLOCK == 0 and ctx.shape[1] % SCAN_CHUNK == 0
    assert w_ada.shape[0] == 1, "single-layer problem: the context stream is only read by the recurrence"
    cap = CAPACITY_FACTOR * t // N_EXPERTS
    col_x = D_FOURIER + D_RNN
    col_g = col_x + D_RNN
    col_a = col_g + D_MODEL

    pos = _pos_table(t)
    ct, st = _dft_tables(t)

    cc = jnp.concatenate([c, c_ctx[None], jnp.zeros((SUBLANES - 1, d), F32)], axis=0)
    mods = _ada(cc, w_ada[0], b_ada[0]).reshape(cc.shape[0], N_MOD, d)
    mods = jnp.pad(mods, ((0, 0), (0, SUBLANES - N_MOD), (0, 0)))

    w = w_in[0]
    wfc, wfs = _fold_channel_dft(w[:, :D_FOURIER])
    w_x = w[:, D_FOURIER:col_x].astype(BF16)
    w_cat = jnp.concatenate([w_x, w[:, col_x:].astype(BF16), wfc, wfs], axis=1)
    g1 = norm1_g[0].reshape(1, d)
    zx, zg, za, zb, zfc, zfs = _proj(x, pos, mods, None, g1, w_cat,
                                     (D_RNN, D_RNN, D_MODEL, D_MODEL, D_FOURIER, D_FOURIER))
    (zx_ctx,) = _proj(ctx, None, mods, b, g1, w_x, (D_RNN,))

    for dr in range(2):
        vecs = jnp.concatenate([lru_lambda[0, dr][None], 0.5 * lru_ba[0, dr][None], 0.5 * lru_bi[0, dr][None],
                                conv_b[0][None], conv_w[0]], axis=0)
        wg = (0.5 * jnp.concatenate([lru_wa[0, dr], lru_wi[0, dr]], axis=-1)).astype(BF16)
        if dr == 0:
            u_lat, wd_b = _conv(zx, vecs, cast=w_down_e[0].reshape(N_EXPERTS * D_EXPERT, d))
            (u_ctx,) = _conv(zx_ctx, vecs)
        _, h_ctx = _scan(u_ctx, vecs, wg, jnp.zeros((b, D_RNN), F32), reverse=bool(dr))
        if dr == 0:
            h_fwd, _, wg_b = _scan(u_lat, vecs, wg, h_ctx, reverse=False,
                                   cast=w_gate_e[0].reshape(N_EXPERTS * d, D_EXPERT))
        else:
            gr, _, wu_b = _scan(u_lat, vecs, wg, h_ctx, reverse=True, hf=h_fwd, zg=zg,
                                cast=w_up_e[0].reshape(N_EXPERTS * d, D_EXPERT))

    f = _dft_apply(ct, st, zfc, zfs)

    x1, h2, scores, score_rows = _merge(f, gr, za, zb, x, pos, mods, norm2_g[0].reshape(1, d),
                                        w_four[0].astype(BF16), w_lru[0].astype(BF16), w_out[0].astype(BF16),
                                        w_router[0].T.astype(BF16))

    slot, offs = _select(scores.reshape(b * N_EXPERTS, t), cap)
    offs = offs[:, :t // OFF_CHUNK + 1].reshape(-1)
    y = _experts(offs, slot, score_rows, h2, wg_b.reshape(N_EXPERTS, d, D_EXPERT),
                 wu_b.reshape(N_EXPERTS, d, D_EXPERT), wd_b.reshape(N_EXPERTS, D_EXPERT, d), cap)
    slot_t = jnp.swapaxes(slot.reshape(b, N_EXPERTS, t), 1, 2)
    return _combine(offs, slot_t, y.reshape(b, N_EXPERTS * cap, d), x1, mods, final_g.reshape(1, d), cap)
```

```python
import functools
import math

import numpy as np
import jax
import jax.numpy as jnp
from jax import lax
from jax.experimental import pallas as pl
from jax.experimental.pallas import tpu as pltpu

F32 = jnp.float32
BF16 = jnp.bfloat16

D_MODEL = 1024
GRID_W = 64
N_FOURIER_GROUPS = 4
FOURIER_GROUP = 128
D_FOURIER = N_FOURIER_GROUPS * FOURIER_GROUP
D_RNN = D_MODEL
N_RNN_HEADS = 8
RNN_HEAD = D_RNN // N_RNN_HEADS
CONV_W = 4
CONV_LEFT = 2
LRU_C = 8.0
LOG2E = math.log2(math.e)
N_EXPERTS = 16
CAPACITY_FACTOR = 2
D_EXPERT = 1536
N_MOD = 6
EPS = 1e-6
POS_MAX_PERIOD = 10000.0

LANES = 128
SUBLANES = 8
BF16_ROWS = 16
VMEM_LIMIT = 56 * 1024 * 1024

TOKEN_BLOCK = 512
SCAN_CHUNK = 128


def _params(*sem):
    return pltpu.CompilerParams(dimension_semantics=sem, vmem_limit_bytes=VMEM_LIMIT)


def _sigmoid(x):
    return 0.5 * jnp.tanh(0.5 * x) + 0.5


def _rmsnorm(x, g):
    return x * lax.rsqrt(jnp.mean(x * x, axis=-1, keepdims=True) + EPS) * g


def _dot(a, b):
    return jnp.dot(a, b, preferred_element_type=F32)


def _pos_kernel(o_ref):
    rows = o_ref.shape[0]
    q = D_MODEL // 4
    k = lax.broadcasted_iota(jnp.int32, (1, q), 1).astype(F32)
    freqs = jnp.exp(-math.log(POS_MAX_PERIOD) * k / q)
    ang_r = lax.broadcasted_iota(jnp.int32, (rows, q), 0).astype(F32) * freqs
    ang_c = lax.broadcasted_iota(jnp.int32, (GRID_W, q), 0).astype(F32) * freqs
    shape = (rows, GRID_W, q)
    o_ref[:, :, 0 * q:1 * q] = jnp.broadcast_to(jnp.sin(ang_r)[:, None, :], shape)
    o_ref[:, :, 1 * q:2 * q] = jnp.broadcast_to(jnp.cos(ang_r)[:, None, :], shape)
    o_ref[:, :, 2 * q:3 * q] = jnp.broadcast_to(jnp.sin(ang_c)[None, :, :], shape)
    o_ref[:, :, 3 * q:4 * q] = jnp.broadcast_to(jnp.cos(ang_c)[None, :, :], shape)


def _pos_table(n_tokens):
    rows = n_tokens // GRID_W
    out = pl.pallas_call(
        _pos_kernel,
        out_shape=jax.ShapeDtypeStruct((rows, GRID_W, D_MODEL), F32),
        name="pos_table",
    )()
    return out.reshape(n_tokens, D_MODEL)


RADIX = 8


def _dft_kernel(c_ref, s_ref, *, n):
    ns = c_ref.shape[1]
    k2 = lax.broadcasted_iota(jnp.int32, (ns, ns), 0)
    t2 = lax.broadcasted_iota(jnp.int32, (ns, ns), 1)
    common = ((k2 * t2) & (ns - 1)).astype(F32) * (2.0 * math.pi / ns)
    cos_c = jnp.cos(common)
    sin_c = jnp.sin(common)
    scale = 1.0 / math.sqrt(n)
    t_row = lax.broadcasted_iota(jnp.int32, (1, ns), 1)
    for k1 in range(RADIX):
        slab = (k1 * t_row).astype(F32) * (2.0 * math.pi / n)
        cos_s = jnp.cos(slab) * scale
        sin_s = jnp.sin(slab) * scale
        c_ref[k1] = (cos_c * cos_s - sin_c * sin_s).astype(BF16)
        s_ref[k1] = (sin_c * cos_s + cos_c * sin_s).astype(BF16)


def _dft_tables(n):
    assert n & (n - 1) == 0 and n % (RADIX * BF16_ROWS) == 0
    ns = n // RADIX
    return pl.pallas_call(
        functools.partial(_dft_kernel, n=n),
        out_shape=[jax.ShapeDtypeStruct((RADIX, ns, ns), BF16)] * 2,
        compiler_params=_params(),
        name="dft_tables",
    )()


def _ada_kernel(c_ref, w_ref, b_ref, o_ref):
    c = c_ref[...]
    o_ref[...] = jnp.dot(c * _sigmoid(c), w_ref[...], preferred_element_type=F32,
                         precision=lax.Precision.HIGHEST) + b_ref[...]


def _ada(cc, w_ada, b_ada):
    n = w_ada.shape[1]
    tn = D_MODEL
    return pl.pallas_call(
        _ada_kernel,
        grid=(n // tn,),
        in_specs=[pl.BlockSpec(cc.shape, lambda j: (0, 0)),
                  pl.BlockSpec((D_MODEL, tn), lambda j: (0, j)),
                  pl.BlockSpec((1, tn), lambda j: (0, j))],
        out_specs=pl.BlockSpec((cc.shape[0], tn), lambda j: (0, j)),
        out_shape=jax.ShapeDtypeStruct((cc.shape[0], n), F32),
        compiler_params=_params("parallel"),
        name="ada",
    )(cc, w_ada, b_ada.reshape(1, n))


def _proj_kernel(*refs, widths, has_pos, n_cast):
    refs = list(refs)
    if n_cast:
        outs_cast = refs[-n_cast:]
        del refs[-n_cast:]
        n_in = len(refs) - len(widths)
        for w32_ref, w16_ref in zip(refs[n_in - n_cast:n_in], outs_cast):
            w16_ref[...] = w32_ref[...].astype(BF16)
        del refs[n_in - n_cast:n_in]
    if has_pos:
        x_ref, pos_ref, mod_ref, g_ref, w_ref = refs[:5]
        outs = refs[5:]
        x = x_ref[...] + pos_ref[...]
    else:
        x_ref, mod_ref, g_ref, w_ref = refs[:4]
        outs = refs[4:]
        x = x_ref[...]
    h = _rmsnorm(x, g_ref[...]) * (1.0 + mod_ref[1:2, :]) + mod_ref[0:1, :]
    hb = h.astype(BF16)
    c0 = 0
    for o_ref, wd in zip(outs, widths):
        o_ref[...] = _dot(hb, w_ref[:, c0:c0 + wd]).astype(BF16)
        c0 += wd


def _proj(x, pos, mods, mod_row, g, w, widths, casts=()):
    b, t, d = x.shape
    tm = min(TOKEN_BLOCK, t)
    has_pos = pos is not None
    in_specs = [pl.BlockSpec((None, tm, d), lambda i, bb: (bb, i, 0))]
    args = [x]
    if has_pos:
        in_specs.append(pl.BlockSpec((tm, d), lambda i, bb: (i, 0)))
        args.append(pos)
    if mod_row is None:
        mod_map = lambda i, bb: (bb, 0, 0)
    else:
        mod_map = lambda i, bb: (mod_row, 0, 0)
    in_specs += [pl.BlockSpec((None, SUBLANES, d), mod_map),
                 pl.BlockSpec((1, d), lambda i, bb: (0, 0)),
                 pl.BlockSpec(w.shape, lambda i, bb: (0, 0))]
    args += [mods, g, w]
    out_specs = [pl.BlockSpec((None, tm, wd), lambda i, bb: (bb, i, 0)) for wd in widths]
    out_shape = [jax.ShapeDtypeStruct((b, t, wd), BF16) for wd in widths]
    for w32 in casts:
        spec, shape = _cast_specs(w32, (t // tm) * b, lambda i, bb: i * b + bb)
        in_specs.append(spec)
        args.append(w32)
        out_specs.append(spec)
        out_shape.append(shape)
    return pl.pallas_call(
        functools.partial(_proj_kernel, widths=widths, has_pos=has_pos, n_cast=len(casts)),
        grid=(t // tm, b),
        in_specs=in_specs,
        out_specs=out_specs,
        out_shape=out_shape,
        compiler_params=_params("parallel", "parallel"),
        name="proj",
    )(*args)


def _gelu(x):
    return 0.5 * x * (1.0 + jnp.tanh(math.sqrt(2.0 / math.pi) * (x + 0.044715 * x * x * x)))


def _conv_kernel(x_ref, xp_ref, xn_ref, vec_ref, o_ref):
    tm = x_ref.shape[0]
    edge = BF16_ROWS
    i = pl.program_id(0)
    first = i == 0
    last = i == pl.num_programs(0) - 1
    w = [vec_ref[4 + kk:5 + kk, :] for kk in range(CONV_W)]
    bias = vec_ref[3:4, :]

    def taps(xm2, xm1, x0, xp1):
        return bias + w[0] * xm2 + w[1] * xm1 + w[2] * x0 + w[3] * xp1

    x = x_ref[...].astype(F32)
    o_ref[...] = taps(pltpu.roll(x, 2, 0), pltpu.roll(x, 1, 0), x, pltpu.roll(x, tm - 1, 0)).astype(BF16)

    row = lax.broadcasted_iota(jnp.int32, (edge, x.shape[1]), 0)

    def shifted(before, cur, after, s):
        if s < 0:
            return jnp.where(row < -s, pltpu.roll(before, -s, 0), pltpu.roll(cur, -s, 0))
        return jnp.where(row < edge - s, pltpu.roll(cur, edge - s, 0), pltpu.roll(after, edge - s, 0))

    prev = jnp.where(first, 0.0, xp_ref[...].astype(F32))
    nxt = jnp.where(last, 0.0, xn_ref[...].astype(F32))
    head, head2 = x[0:edge], x[edge:2 * edge]
    tail, tail2 = x[tm - edge:tm], x[tm - 2 * edge:tm - edge]
    o_ref[0:edge, :] = taps(shifted(prev, head, head2, -2), shifted(prev, head, head2, -1), head,
                            shifted(prev, head, head2, 1)).astype(BF16)
    o_ref[tm - edge:tm, :] = taps(shifted(tail2, tail, nxt, -2), shifted(tail2, tail, nxt, -1), tail,
                                  shifted(tail2, tail, nxt, 1)).astype(BF16)


def _conv(zx, vecs):
    b, t, c = zx.shape
    tm = min(TOKEN_BLOCK, t)
    assert tm >= 2 * BF16_ROWS
    hpb = tm // BF16_ROWS
    n_halo = t // BF16_ROWS
    return pl.pallas_call(
        _conv_kernel,
        grid=(t // tm, b),
        in_specs=[pl.BlockSpec((None, tm, c), lambda i, bb: (bb, i, 0)),
                  pl.BlockSpec((None, BF16_ROWS, c), lambda i, bb: (bb, jnp.maximum(i * hpb - 1, 0), 0)),
                  pl.BlockSpec((None, BF16_ROWS, c), lambda i, bb: (bb, jnp.minimum((i + 1) * hpb, n_halo - 1), 0)),
                  pl.BlockSpec(vecs.shape, lambda i, bb: (0, 0))],
        out_specs=pl.BlockSpec((None, tm, c), lambda i, bb: (bb, i, 0)),
        out_shape=jax.ShapeDtypeStruct((b, t, c), BF16),
        compiler_params=_params("parallel", "parallel"),
        name="conv",
    )(zx, zx, zx, vecs)


def _scan_kernel(*refs, reverse, fuse):
    if fuse:
        u_ref, vec_ref, wg_ref, h0_ref, hf_ref, zg_ref, o_ref, hT_ref, a_s, b_s, h_s, hc_s = refs
    else:
        u_ref, vec_ref, wg_ref, h0_ref, o_ref, hT_ref, a_s, b_s, h_s, hc_s = refs
    nb, tc, c = u_ref.shape

    @pl.when(pl.program_id(0) == 0)
    def _():
        for hh in range(N_RNN_HEADS):
            hc_s[hh] = h0_ref[:, hh * RNN_HEAD:(hh + 1) * RNN_HEAD]

    lam = vec_ref[0:1, :]
    half_b_a = vec_ref[1:2, :]
    half_b_i = vec_ref[2:3, :]
    k_row = (-0.5 * LRU_C * LOG2E) * (jnp.maximum(-lam, 0.0) + jnp.log1p(jnp.exp(-jnp.abs(lam))))

    for bb in range(nb):
        ub = u_ref[bb]
        u = ub.astype(F32)
        rows = pl.ds(bb, tc, stride=nb)
        for hh in range(N_RNN_HEADS):
            sl = slice(hh * RNN_HEAD, (hh + 1) * RNN_HEAD)
            gates = _dot(ub[:, sl], wg_ref[hh])
            t_r = jnp.tanh(gates[:, :RNN_HEAD] + half_b_a[:, sl])
            t_i = jnp.tanh(gates[:, RNN_HEAD:] + half_b_i[:, sl])
            a = jnp.exp2(k_row[:, sl] * t_r + k_row[:, sl])
            a_s[hh, rows, :] = a
            scale = jnp.exp2((0.5 * LOG2E) * jnp.log(1.0 - a * a)) * u[:, sl]
            b_s[hh, rows, :] = scale * (0.5 * t_i + 0.5)

    def body(j, hs):
        t = (tc - 1 - j) if reverse else j
        idx = pl.ds(pl.multiple_of(t * nb, nb), nb)
        new = []
        for hh in range(N_RNN_HEADS):
            h = a_s[hh, idx, :] * hs[hh] + b_s[hh, idx, :]
            h_s[hh, idx, :] = h
            new.append(h)
        return tuple(new)

    hs = lax.fori_loop(0, tc, body, tuple(hc_s[hh] for hh in range(N_RNN_HEADS)), unroll=8)
    for hh in range(N_RNN_HEADS):
        hc_s[hh] = hs[hh]
        hT_ref[:, hh * RNN_HEAD:(hh + 1) * RNN_HEAD] = hs[hh]

    for bb in range(nb):
        for hh in range(N_RNN_HEADS):
            sl = slice(hh * RNN_HEAD, (hh + 1) * RNN_HEAD)
            hb = h_s[hh, pl.ds(bb, tc, stride=nb), :]
            if fuse:
                hb = _gelu(zg_ref[bb, :, sl].astype(F32)) * (hf_ref[bb, :, sl].astype(F32) + hb)
            o_ref[bb, :, sl] = hb.astype(BF16)


def _cast_specs(w, n_steps, step_index):
    rows, cols = w.shape
    assert rows % (n_steps * BF16_ROWS) == 0
    spec = pl.BlockSpec((rows // n_steps, cols), lambda *idx: (step_index(*idx), 0))
    return spec, jax.ShapeDtypeStruct((rows, cols), BF16)


def _scan(u, vecs, wg, h0, *, reverse, hf=None, zg=None):
    nb, t, c = u.shape
    tc = SCAN_CHUNK
    n_chunks = t // tc
    fuse = hf is not None

    def cidx(i):
        return (n_chunks - 1 - i) if reverse else i

    main_spec = pl.BlockSpec((nb, tc, c), lambda i: (0, cidx(i), 0))
    in_specs = [
        main_spec,
        pl.BlockSpec(vecs.shape, lambda i: (0, 0)),
        pl.BlockSpec(wg.shape, lambda i: (0, 0, 0)),
        pl.BlockSpec(h0.shape, lambda i: (0, 0)),
    ]
    args = [u, vecs, wg, h0]
    if fuse:
        in_specs += [main_spec, main_spec]
        args += [hf, zg]
    return pl.pallas_call(
        functools.partial(_scan_kernel, reverse=reverse, fuse=fuse),
        grid=(n_chunks,),
        in_specs=in_specs,
        out_specs=[main_spec, pl.BlockSpec((nb, c), lambda i: (0, 0))],
        out_shape=[jax.ShapeDtypeStruct((nb, t, c), BF16), jax.ShapeDtypeStruct((nb, c), F32)],
        scratch_shapes=[pltpu.VMEM((N_RNN_HEADS, nb * tc, RNN_HEAD), F32),
                        pltpu.VMEM((N_RNN_HEADS, nb * tc, RNN_HEAD), F32),
                        pltpu.VMEM((N_RNN_HEADS, nb * tc, RNN_HEAD), F32),
                        pltpu.VMEM((N_RNN_HEADS, nb, RNN_HEAD), F32)],
        compiler_params=_params("arbitrary"),
        name="scan_bwd" if reverse else "scan_fwd",
    )(*args)


DFT_ROWS = 64
DFT_COLS = 256


def _dftmm_kernel(zf_ref, cs_ref, c_ref, s_ref, o_ref, zc_s, zs_s, ar_s, ai_s, o_s):
    t, f = zf_ref.shape
    ns = t // RADIX
    rb = min(DFT_ROWS, ns)
    r2 = math.sqrt(0.5)

    cs_table = cs_ref[...].astype(BF16)
    for j in range(RADIX):
        rows = slice(j * ns, (j + 1) * ns)
        for g in range(N_FOURIER_GROUPS):
            sl = slice(g * FOURIER_GROUP, (g + 1) * FOURIER_GROUP)
            cs = _dot(zf_ref[rows, sl], cs_table)
            zc_s[rows, sl] = cs[:, :FOURIER_GROUP].astype(BF16)
            zs_s[rows, sl] = cs[:, FOURIER_GROUP:].astype(BF16)

    def dft4(x):
        s0 = (x[0][0] + x[2][0], x[0][1] + x[2][1])
        s1 = (x[0][0] - x[2][0], x[0][1] - x[2][1])
        s2 = (x[1][0] + x[3][0], x[1][1] + x[3][1])
        s3 = (x[1][0] - x[3][0], x[1][1] - x[3][1])
        return [(s0[0] + s2[0], s0[1] + s2[1]), (s1[0] + s3[1], s1[1] - s3[0]),
                (s0[0] - s2[0], s0[1] - s2[1]), (s1[0] - s3[1], s1[1] + s3[0])]

    def butterfly(i, carry):
        r0 = pl.multiple_of(i * rb, rb)
        w = [(zc_s[pl.ds(j * ns + r0, rb), :].astype(F32), -zs_s[pl.ds(j * ns + r0, rb), :].astype(F32))
             for j in range(RADIX)]
        u = [(w[j][0] + w[j + 4][0], w[j][1] + w[j + 4][1]) for j in range(4)]
        v = [(w[j][0] - w[j + 4][0], w[j][1] - w[j + 4][1]) for j in range(4)]
        v = [v[0],
             ((v[1][0] + v[1][1]) * r2, (v[1][1] - v[1][0]) * r2),
             (v[2][1], -v[2][0]),
             ((v[3][1] - v[3][0]) * r2, (-v[3][0] - v[3][1]) * r2)]
        even = dft4(u)
        odd = dft4(v)
        for m in range(4):
            for k1, a in ((2 * m, even[m]), (2 * m + 1, odd[m])):
                ar_s[k1, pl.ds(r0, rb), :] = a[0].astype(BF16)
                ai_s[k1, pl.ds(r0, rb), :] = a[1].astype(BF16)
        return carry

    lax.fori_loop(0, ns // rb, butterfly, 0)

    for c0 in range(0, f, DFT_COLS):
        for k1 in range(RADIX):
            res = (_dot(c_ref[k1], ar_s[k1, :, c0:c0 + DFT_COLS])
                   + _dot(s_ref[k1], ai_s[k1, :, c0:c0 + DFT_COLS]))
            for j in range(DFT_COLS // LANES):
                o_s[j, pl.ds(k1, ns, stride=RADIX), :] = res[:, j * LANES:(j + 1) * LANES]
        for j in range(DFT_COLS // LANES):
            o_ref[:, c0 + j * LANES:c0 + (j + 1) * LANES] = o_s[j].astype(BF16)


def _channel_dft_table():
    p = np.arange(FOURIER_GROUP)
    ang = 2.0 * np.pi * ((p[:, None] * p[None, :]) % FOURIER_GROUP) / FOURIER_GROUP
    cs = np.concatenate([np.cos(ang), np.sin(ang)], axis=1) / math.sqrt(FOURIER_GROUP)
    return jnp.asarray(cs, F32)


def _dft_apply(ct, st, zf):
    b, t, f = zf.shape
    ns = t // RADIX
    table = pl.BlockSpec((RADIX, ns, ns), lambda bb: (0, 0, 0), pipeline_mode=pl.Buffered(1))
    cs = _channel_dft_table()
    return pl.pallas_call(
        _dftmm_kernel,
        grid=(b,),
        in_specs=[pl.BlockSpec((None, t, f), lambda bb: (bb, 0, 0)),
                  pl.BlockSpec(cs.shape, lambda bb: (0, 0)),
                  table, table],
        out_specs=pl.BlockSpec((None, t, f), lambda bb: (bb, 0, 0)),
        out_shape=jax.ShapeDtypeStruct((b, t, f), BF16),
        scratch_shapes=[pltpu.VMEM((t, f), BF16),
                        pltpu.VMEM((t, f), BF16),
                        pltpu.VMEM((RADIX, ns, f), BF16),
                        pltpu.VMEM((RADIX, ns, f), BF16),
                        pltpu.VMEM((DFT_COLS // LANES, t, LANES), F32)],
        compiler_params=_params("parallel"),
        name="dft_apply",
    )(zf, cs, ct, st)


def _merge_kernel(f_ref, gr_ref, za_ref, zb_ref, x_ref, pos_ref, mod_ref, g2_ref,
                  wf_ref, wl_ref, wo_ref, wr_ref, w32_ref, x1_ref, h2_ref, sc_ref, scr_ref, w16_ref):
    w16_ref[...] = w32_ref[...].astype(BF16)
    y_four = _dot(f_ref[...], wf_ref[...])
    y_rnn = _dot(gr_ref[...], wl_ref[...])
    m = _sigmoid(za_ref[...].astype(F32)) * y_four + _sigmoid(zb_ref[...].astype(F32)) * y_rnn
    y = _dot(m.astype(BF16), wo_ref[...])
    x1 = (x_ref[...] + pos_ref[...]) + mod_ref[2:3, :] * y
    x1_ref[...] = x1
    h2 = (_rmsnorm(x1, g2_ref[...]) * (1.0 + mod_ref[4:5, :]) + mod_ref[3:4, :]).astype(BF16)
    h2_ref[...] = h2
    logits = lax.dot_general(wr_ref[...], h2, (((1,), (1,)), ((), ())), preferred_element_type=F32)
    e = jnp.exp(logits - jnp.max(logits, axis=0, keepdims=True))
    sc = e / jnp.sum(e, axis=0, keepdims=True)
    sc_ref[...] = sc
    hi = sc.astype(BF16).astype(F32)
    lo = sc - hi
    tm = sc.shape[1]
    row = lax.broadcasted_iota(jnp.int32, (BF16_ROWS, tm), 0)
    for ex in range(N_EXPERTS):
        tile = jnp.where(row == 0, hi[ex:ex + 1, :], jnp.where(row == 1, lo[ex:ex + 1, :], 0.0))
        scr_ref[ex] = tile.astype(BF16)


def _merge(f, gr, za, zb, x, pos, mods, g2, w_four, w_lru, w_out, w_rt, cast):
    b, t, d = x.shape
    tm = min(TOKEN_BLOCK, t)
    tok = lambda wd: pl.BlockSpec((None, tm, wd), lambda i, bb: (bb, i, 0))
    full = lambda a: pl.BlockSpec(a.shape, lambda i, bb: (0,) * a.ndim)
    cast_spec, cast_shape = _cast_specs(cast, (t // tm) * b, lambda i, bb: i * b + bb)
    return pl.pallas_call(
        _merge_kernel,
        grid=(t // tm, b),
        in_specs=[tok(D_FOURIER), tok(d), tok(d), tok(d), tok(d),
                  pl.BlockSpec((tm, d), lambda i, bb: (i, 0)),
                  pl.BlockSpec((None, SUBLANES, d), lambda i, bb: (bb, 0, 0)),
                  full(g2), full(w_four), full(w_lru), full(w_out), full(w_rt), cast_spec],
        out_specs=[tok(d), tok(d), pl.BlockSpec((None, N_EXPERTS, tm), lambda i, bb: (bb, 0, i)),
                   pl.BlockSpec((None, N_EXPERTS, BF16_ROWS, tm), lambda i, bb: (bb, 0, 0, i)), cast_spec],
        out_shape=[jax.ShapeDtypeStruct((b, t, d), F32),
                   jax.ShapeDtypeStruct((b, t, d), BF16),
                   jax.ShapeDtypeStruct((b, N_EXPERTS, t), F32),
                   jax.ShapeDtypeStruct((b, N_EXPERTS, BF16_ROWS, t), BF16), cast_shape],
        compiler_params=_params("parallel", "parallel"),
        name="merge",
    )(f, gr, za, zb, x, pos, mods, g2, w_four, w_lru, w_out, w_rt, cast)


MAX_EXP = 127
EXP_STEPS = 7
VAL_STEPS = 56


OFF_CHUNK = 256


def _select_kernel(s_ref, slot_ref, offs_ref, *, cap):
    rows, t = s_ref.shape

    def count(mask):
        return jnp.sum(jnp.where(mask, 1.0, 0.0), axis=1, keepdims=True)

    def exp_step(_, carry):
        k_lo, k_hi = carry
        k_mid = jnp.floor(0.5 * (k_lo + k_hi))
        ok = count(s_ref[...] >= jnp.exp2(-k_mid)) >= cap
        return jnp.where(ok, k_lo, k_mid), jnp.where(ok, k_mid, k_hi)

    k_lo, k_hi = lax.fori_loop(
        0, EXP_STEPS, exp_step,
        (jnp.full((rows, 1), -1.0, F32), jnp.full((rows, 1), float(MAX_EXP), F32)))
    lo0 = jnp.where(k_hi >= MAX_EXP, 0.0, jnp.exp2(-k_hi))
    hi0 = jnp.exp2(-k_lo)

    def val_step(_, carry):
        lo, hi = carry
        mid = 0.5 * (lo + hi)
        ok = count(s_ref[...] >= mid) >= cap
        return jnp.where(ok, mid, lo), jnp.where(ok, hi, mid)

    lo, hi = lax.fori_loop(0, VAL_STEPS, val_step, (lo0, hi0))
    s = s_ref[...]
    gt = s >= hi
    eq = (s >= lo) & (s < hi)
    need = cap - count(gt)

    p = lax.broadcasted_iota(jnp.int32, (LANES, LANES), 0)
    q = lax.broadcasted_iota(jnp.int32, (LANES, LANES), 1)
    tri = (p < q).astype(BF16)
    run_gt = jnp.zeros((rows, 1), F32)
    run_eq = jnp.zeros((rows, 1), F32)
    off_lane = lax.broadcasted_iota(jnp.int32, (rows, LANES), 1)
    offs = jnp.zeros((rows, LANES), F32)
    for j in range(t // LANES):
        if (j * LANES) % OFF_CHUNK == 0:
            offs = jnp.where(off_lane == (j * LANES) // OFF_CHUNK, run_gt + jnp.minimum(run_eq, need), offs)
        sl = slice(j * LANES, (j + 1) * LANES)
        g = gt[:, sl]
        q_ = eq[:, sl]
        pg = _dot(g.astype(BF16), tri) + run_gt
        pe = _dot(q_.astype(BF16), tri) + run_eq
        sel = g | (q_ & (pe < need))
        pos = pg + jnp.minimum(pe, need)
        slot_ref[:, sl] = jnp.where(sel, pos, -1.0).astype(jnp.int32)
        run_gt = run_gt + count(g)
        run_eq = run_eq + count(q_)
    offs = jnp.where(off_lane == t // OFF_CHUNK, run_gt + jnp.minimum(run_eq, need), offs)
    offs_ref[...] = offs.astype(jnp.int32)


def _select(scores, cap):
    rows, t = scores.shape
    assert t % OFF_CHUNK == 0 and t // OFF_CHUNK < LANES
    return pl.pallas_call(
        functools.partial(_select_kernel, cap=cap),
        out_shape=[jax.ShapeDtypeStruct((rows, t), jnp.int32),
                   jax.ShapeDtypeStruct((rows, LANES), jnp.int32)],
        compiler_params=_params(),
        name="select",
    )(scores)


GATHER_BLOCK = 512
GATHER_WINDOW = 128


def _expert_kernel(offs_ref, slot_ref, sc_ref, h_ref, wg_ref, wu_ref, wd_ref, o_ref,
                   xg_s, gs_s, rows_s, gates_s, *, cap, n_off):
    t = h_ref.shape[0]
    tk = min(GATHER_BLOCK, t)
    win = min(GATHER_WINDOW, cap)
    per_block = tk // OFF_CHUNK
    slot_iota = lax.broadcasted_iota(jnp.int32, (win, tk), 0)
    xg_s[...] = jnp.zeros_like(xg_s)
    gs_s[...] = jnp.zeros_like(gs_s)
    row0 = (pl.program_id(0) * N_EXPERTS + pl.program_id(1)) * n_off

    def gather(sl, lo):
        onehot = (slot_iota + lo == slot_ref[:, sl]).astype(BF16)
        rows = _dot(onehot, h_ref[sl, :])
        gates = lax.dot_general(onehot, sc_ref[:, sl], (((1,), (1,)), ((), ())), preferred_element_type=F32)
        return rows, gates

    n_blocks = t // tk
    starts = [pl.multiple_of((offs_ref[row0 + j * per_block] // SUBLANES) * SUBLANES, SUBLANES)
              for j in range(n_blocks)]
    for j in range(n_blocks):
        rows_s[j], gates_s[j] = gather(slice(j * tk, (j + 1) * tk), starts[j])
    for j in range(n_blocks):
        sl = slice(j * tk, (j + 1) * tk)
        st = starts[j]
        last = offs_ref[row0 + (j + 1) * per_block]
        xg_s[pl.ds(st, SUBLANES), :] += rows_s[j, 0:SUBLANES, :]
        xg_s[pl.ds(st + SUBLANES, win - SUBLANES), :] = rows_s[j, SUBLANES:win, :]
        gs_s[pl.ds(st, win), :] += gates_s[j]

        def more(c, carry, sl=sl, st=st):
            lo = pl.multiple_of(st + c * win, SUBLANES)
            rows, gates = gather(sl, lo)
            xg_s[pl.ds(lo, win), :] = rows
            gs_s[pl.ds(lo, win), :] += gates
            return carry

        lax.fori_loop(1, (last - st + win - 1) // win, more, 0)
    xb = xg_s[0:cap, :].astype(BF16)
    gate = _dot(xb, wg_ref[...])
    hid = (gate * _sigmoid(gate)) * _dot(xb, wu_ref[...])
    y = _dot(hid.astype(BF16), wd_ref[...])
    o_ref[...] = (y * (gs_s[0:cap, 0:1] + gs_s[0:cap, 1:2])).astype(BF16)


def _experts(offs, slot, score_rows, h2, wg, wu, wd, cap):
    b, t, d = h2.shape
    n_off = offs.shape[0] // (b * N_EXPERTS)
    win = min(GATHER_WINDOW, cap)
    grid_spec = pltpu.PrefetchScalarGridSpec(
        num_scalar_prefetch=1,
        grid=(b, N_EXPERTS),
        in_specs=[pl.BlockSpec((None, 1, t), lambda bb, e, o: (bb * N_EXPERTS + e, 0, 0)),
                  pl.BlockSpec((None, None, BF16_ROWS, t), lambda bb, e, o: (bb, e, 0, 0)),
                  pl.BlockSpec((None, t, d), lambda bb, e, o: (bb, 0, 0)),
                  pl.BlockSpec((None, d, D_EXPERT), lambda bb, e, o: (e, 0, 0)),
                  pl.BlockSpec((None, d, D_EXPERT), lambda bb, e, o: (e, 0, 0)),
                  pl.BlockSpec((None, D_EXPERT, d), lambda bb, e, o: (e, 0, 0))],
        out_specs=pl.BlockSpec((None, cap, d), lambda bb, e, o: (bb * N_EXPERTS + e, 0, 0)),
        scratch_shapes=[pltpu.VMEM((cap + win, d), F32), pltpu.VMEM((cap + win, BF16_ROWS), F32),
                        pltpu.VMEM((t // min(GATHER_BLOCK, t), win, d), F32),
                        pltpu.VMEM((t // min(GATHER_BLOCK, t), win, BF16_ROWS), F32)],
    )
    return pl.pallas_call(
        functools.partial(_expert_kernel, cap=cap, n_off=n_off),
        grid_spec=grid_spec,
        out_shape=jax.ShapeDtypeStruct((b * N_EXPERTS, cap, d), BF16),
        compiler_params=_params("parallel", "parallel"),
        name="experts",
    )(offs, slot.reshape(b * N_EXPERTS, 1, t), score_rows, h2, wg, wu, wd)


COMBINE_BLOCK = 1024
SLOT_WINDOW = 256


COMBINE_GROUP = 8


def _combine_kernel(offs_ref, slot_ref, y_ref, x1_ref, mod_ref, g_ref, o_ref, acc_s, oh_s, yw_s, *, cap, n_off):
    tm = x1_ref.shape[0]
    kc = min(SLOT_WINDOW, cap)
    per_block = tm // OFF_CHUNK
    lane_iota = lax.broadcasted_iota(jnp.int32, (tm, kc), 1)

    def place(e, start0, c):
        lo = start0 + c * kc
        st = pl.multiple_of(jnp.minimum(lo, cap - kc), BF16_ROWS)
        onehot = slot_ref[:, e:e + 1] == st + lane_iota
        if not isinstance(c, int) or c > 0:
            onehot = onehot & (slot_ref[:, e:e + 1] >= lo)
        return onehot.astype(BF16), st

    starts, extras = [], []
    for e in range(N_EXPERTS):
        o0 = (pl.program_id(0) * N_EXPERTS + e) * n_off + pl.program_id(1) * per_block
        start0 = (offs_ref[o0] // BF16_ROWS) * BF16_ROWS
        starts.append(start0)
        extras.append(jnp.maximum((offs_ref[o0 + per_block] - start0 + kc - 1) // kc - 1, 0))

    acc = None
    for g0 in range(0, N_EXPERTS, COMBINE_GROUP):
        for i in range(COMBINE_GROUP):
            onehot, st = place(g0 + i, starts[g0 + i], 0)
            oh_s[:, i * kc:(i + 1) * kc] = onehot
            yw_s[i * kc:(i + 1) * kc, :] = y_ref[pl.ds((g0 + i) * cap + st, kc), :]
        part = _dot(oh_s[...], yw_s[...])
        acc = part if acc is None else acc + part

    def finish(total):
        o_ref[...] = _rmsnorm(x1_ref[...] + mod_ref[5:6, :] * total, g_ref[...])

    finish(acc)

    @pl.when(sum(extras) > 0)
    def _():
        acc_s[...] = acc
        for e in range(N_EXPERTS):
            def more(c, carry, e=e):
                onehot, st = place(e, starts[e], c)
                acc_s[...] += _dot(onehot, y_ref[pl.ds(e * cap + st, kc), :])
                return carry
            lax.fori_loop(1, extras[e] + 1, more, 0)
        finish(acc_s[...])


def _combine(offs, slot_t, y, x1, mods, g, cap):
    b, t, d = x1.shape
    tm = min(COMBINE_BLOCK, t)
    n_off = offs.shape[0] // (b * N_EXPERTS)
    grid_spec = pltpu.PrefetchScalarGridSpec(
        num_scalar_prefetch=1,
        grid=(b, t // tm),
        in_specs=[pl.BlockSpec((None, tm, N_EXPERTS), lambda bb, i, o: (bb, i, 0)),
                  pl.BlockSpec((None, N_EXPERTS * cap, d), lambda bb, i, o: (bb, 0, 0),
                               pipeline_mode=pl.Buffered(1)),
                  pl.BlockSpec((None, tm, d), lambda bb, i, o: (bb, i, 0)),
                  pl.BlockSpec((None, SUBLANES, d), lambda bb, i, o: (bb, 0, 0)),
                  pl.BlockSpec((1, d), lambda bb, i, o: (0, 0))],
        out_specs=pl.BlockSpec((None, tm, d), lambda bb, i, o: (bb, i, 0)),
        scratch_shapes=[pltpu.VMEM((tm, d), F32),
                        pltpu.VMEM((tm, COMBINE_GROUP * min(SLOT_WINDOW, cap)), BF16),
                        pltpu.VMEM((COMBINE_GROUP * min(SLOT_WINDOW, cap), d), BF16)],
    )
    return pl.pallas_call(
        functools.partial(_combine_kernel, cap=cap, n_off=n_off),
        grid_spec=grid_spec,
        out_shape=jax.ShapeDtypeStruct((b, t, d), F32),
        compiler_params=_params("parallel", "parallel"),
        name="combine",
    )(offs, slot_t, y, x1, mods, g)


def kernel(x, c, ctx, c_ctx, w_ada, b_ada, norm1_g, norm2_g, w_in, w_four, conv_w, conv_b, lru_lambda, lru_wa, lru_ba, lru_wi, lru_bi, w_lru, w_out, w_router, w_gate_e, w_up_e, w_down_e, final_g):
    b, t, d = x.shape
    assert d == D_MODEL and b == SUBLANES and t % TOKEN_BLOCK == 0 and ctx.shape[1] % SCAN_CHUNK == 0
    assert w_ada.shape[0] == 1, "single-layer problem: the context stream is only read by the recurrence"
    cap = CAPACITY_FACTOR * t // N_EXPERTS
    col_x = D_FOURIER + D_RNN
    col_g = col_x + D_RNN
    col_a = col_g + D_MODEL

    pos = _pos_table(t)
    ct, st = _dft_tables(t)

    cc = jnp.concatenate([c, c_ctx[None], jnp.zeros((SUBLANES - 1, d), F32)], axis=0)
    mods = _ada(cc, w_ada[0], b_ada[0]).reshape(cc.shape[0], N_MOD, d)
    mods = jnp.pad(mods, ((0, 0), (0, SUBLANES - N_MOD), (0, 0)))

    w = w_in[0].astype(BF16)
    w_x = w[:, D_FOURIER:col_x]
    w_cat = jnp.concatenate([w[:, D_FOURIER:], w[:, :D_FOURIER]], axis=1)
    g1 = norm1_g[0].reshape(1, d)
    zx, zg, za, zb, zf, wg_b, wu_b = _proj(
        x, pos, mods, None, g1, w_cat, (D_RNN, D_RNN, D_MODEL, D_MODEL, D_FOURIER),
        casts=(w_gate_e[0].reshape(N_EXPERTS * d, D_EXPERT), w_up_e[0].reshape(N_EXPERTS * d, D_EXPERT)))
    (zx_ctx,) = _proj(ctx, None, mods, b, g1, w_x, (D_RNN,))

    for dr in range(2):
        vecs = jnp.concatenate([lru_lambda[0, dr][None], 0.5 * lru_ba[0, dr][None], 0.5 * lru_bi[0, dr][None],
                                conv_b[0][None], conv_w[0]], axis=0)
        wg = (0.5 * jnp.concatenate([lru_wa[0, dr], lru_wi[0, dr]], axis=-1)).astype(BF16)
        if dr == 0:
            u_lat = _conv(zx, vecs)
            u_ctx = _conv(zx_ctx, vecs)
        _, h_ctx = _scan(u_ctx, vecs, wg, jnp.zeros((b, D_RNN), F32), reverse=bool(dr))
        if dr == 0:
            h_fwd, _ = _scan(u_lat, vecs, wg, h_ctx, reverse=False)
        else:
            gr, _ = _scan(u_lat, vecs, wg, h_ctx, reverse=True, hf=h_fwd, zg=zg)

    f = _dft_apply(ct, st, zf)

    x1, h2, scores, score_rows, wd_b = _merge(
        f, gr, za, zb, x, pos, mods, norm2_g[0].reshape(1, d),
        w_four[0].astype(BF16), w_lru[0].astype(BF16), w_out[0].astype(BF16), w_router[0].T.astype(BF16),
        w_down_e[0].reshape(N_EXPERTS * D_EXPERT, d))

    slot, offs = _select(scores.reshape(b * N_EXPERTS, t), cap)
    offs = offs[:, :t // OFF_CHUNK + 1].reshape(-1)
    y = _experts(offs, slot, score_rows, h2, wg_b.reshape(N_EXPERTS, d, D_EXPERT),
                 wu_b.reshape(N_EXPERTS, d, D_EXPERT), wd_b.reshape(N_EXPERTS, D_EXPERT, d), cap)
    slot_t = jnp.swapaxes(slot.reshape(b, N_EXPERTS, t), 1, 2)
    return _combine(offs, slot_t, y.reshape(b, N_EXPERTS * cap, d), x1, mods, final_g.reshape(1, d), cap)
```

```python
import functools
import math

import numpy as np
import jax
import jax.numpy as jnp
from jax import lax
from jax.experimental import pallas as pl
from jax.experimental.pallas import tpu as pltpu

F32 = jnp.float32
BF16 = jnp.bfloat16

D_MODEL = 1024
GRID_W = 64
N_FOURIER_GROUPS = 4
FOURIER_GROUP = 128
D_FOURIER = N_FOURIER_GROUPS * FOURIER_GROUP
D_RNN = D_MODEL
N_RNN_HEADS = 8
RNN_HEAD = D_RNN // N_RNN_HEADS
CONV_W = 4
CONV_LEFT = 2
LRU_C = 8.0
LOG2E = math.log2(math.e)
N_EXPERTS = 16
CAPACITY_FACTOR = 2
D_EXPERT = 1536
N_MOD = 6
EPS = 1e-6
POS_MAX_PERIOD = 10000.0

LANES = 128
SUBLANES = 8
BF16_ROWS = 16
VMEM_LIMIT = 56 * 1024 * 1024

TOKEN_BLOCK = 512
SCAN_CHUNK = 128


def _params(*sem):
    return pltpu.CompilerParams(dimension_semantics=sem, vmem_limit_bytes=VMEM_LIMIT)


def _sigmoid(x):
    return 0.5 * jnp.tanh(0.5 * x) + 0.5


def _rmsnorm(x, g):
    return x * lax.rsqrt(jnp.mean(x * x, axis=-1, keepdims=True) + EPS) * g


def _dot(a, b):
    return jnp.dot(a, b, preferred_element_type=F32)


def _pos_kernel(o_ref):
    rows = o_ref.shape[0]
    q = D_MODEL // 4
    k = lax.broadcasted_iota(jnp.int32, (1, q), 1).astype(F32)
    freqs = jnp.exp(-math.log(POS_MAX_PERIOD) * k / q)
    ang_r = lax.broadcasted_iota(jnp.int32, (rows, q), 0).astype(F32) * freqs
    ang_c = lax.broadcasted_iota(jnp.int32, (GRID_W, q), 0).astype(F32) * freqs
    shape = (rows, GRID_W, q)
    o_ref[:, :, 0 * q:1 * q] = jnp.broadcast_to(jnp.sin(ang_r)[:, None, :], shape)
    o_ref[:, :, 1 * q:2 * q] = jnp.broadcast_to(jnp.cos(ang_r)[:, None, :], shape)
    o_ref[:, :, 2 * q:3 * q] = jnp.broadcast_to(jnp.sin(ang_c)[None, :, :], shape)
    o_ref[:, :, 3 * q:4 * q] = jnp.broadcast_to(jnp.cos(ang_c)[None, :, :], shape)


def _pos_table(n_tokens):
    rows = n_tokens // GRID_W
    out = pl.pallas_call(
        _pos_kernel,
        out_shape=jax.ShapeDtypeStruct((rows, GRID_W, D_MODEL), F32),
        name="pos_table",
    )()
    return out.reshape(n_tokens, D_MODEL)


RADIX = 8


def _dft_kernel(c_ref, s_ref, *, n):
    ns = c_ref.shape[1]
    k2 = lax.broadcasted_iota(jnp.int32, (ns, ns), 0)
    t2 = lax.broadcasted_iota(jnp.int32, (ns, ns), 1)
    common = ((k2 * t2) & (ns - 1)).astype(F32) * (2.0 * math.pi / ns)
    cos_c = jnp.cos(common)
    sin_c = jnp.sin(common)
    scale = 1.0 / math.sqrt(n)
    t_row = lax.broadcasted_iota(jnp.int32, (1, ns), 1)
    for k1 in range(RADIX):
        slab = (k1 * t_row).astype(F32) * (2.0 * math.pi / n)
        cos_s = jnp.cos(slab) * scale
        sin_s = jnp.sin(slab) * scale
        c_ref[k1] = (cos_c * cos_s - sin_c * sin_s).astype(BF16)
        s_ref[k1] = (sin_c * cos_s + cos_c * sin_s).astype(BF16)


def _dft_tables(n):
    assert n & (n - 1) == 0 and n % (RADIX * BF16_ROWS) == 0
    ns = n // RADIX
    return pl.pallas_call(
        functools.partial(_dft_kernel, n=n),
        out_shape=[jax.ShapeDtypeStruct((RADIX, ns, ns), BF16)] * 2,
        compiler_params=_params(),
        name="dft_tables",
    )()


def _ada_kernel(c_ref, w_ref, b_ref, o_ref):
    c = c_ref[...]
    o_ref[...] = jnp.dot(c * _sigmoid(c), w_ref[...], preferred_element_type=F32,
                         precision=lax.Precision.HIGHEST) + b_ref[...]


def _ada(cc, w_ada, b_ada):
    n = w_ada.shape[1]
    tn = D_MODEL
    return pl.pallas_call(
        _ada_kernel,
        grid=(n // tn,),
        in_specs=[pl.BlockSpec(cc.shape, lambda j: (0, 0)),
                  pl.BlockSpec((D_MODEL, tn), lambda j: (0, j)),
                  pl.BlockSpec((1, tn), lambda j: (0, j))],
        out_specs=pl.BlockSpec((cc.shape[0], tn), lambda j: (0, j)),
        out_shape=jax.ShapeDtypeStruct((cc.shape[0], n), F32),
        compiler_params=_params("parallel"),
        name="ada",
    )(cc, w_ada, b_ada.reshape(1, n))


def _proj_kernel(*refs, widths, has_pos, n_cast):
    refs = list(refs)
    if n_cast:
        outs_cast = refs[-n_cast:]
        del refs[-n_cast:]
        n_in = len(refs) - len(widths)
        for w32_ref, w16_ref in zip(refs[n_in - n_cast:n_in], outs_cast):
            w16_ref[...] = w32_ref[...].astype(BF16)
        del refs[n_in - n_cast:n_in]
    if has_pos:
        x_ref, pos_ref, mod_ref, g_ref, w_ref = refs[:5]
        outs = refs[5:]
        x = x_ref[...] + pos_ref[...]
    else:
        x_ref, mod_ref, g_ref, w_ref = refs[:4]
        outs = refs[4:]
        x = x_ref[...]
    h = _rmsnorm(x, g_ref[...]) * (1.0 + mod_ref[1:2, :]) + mod_ref[0:1, :]
    hb = h.astype(BF16)
    c0 = 0
    for o_ref, wd in zip(outs, widths):
        o_ref[...] = _dot(hb, w_ref[:, c0:c0 + wd]).astype(BF16)
        c0 += wd


def _proj(x, pos, mods, mod_row, g, w, widths, casts=()):
    b, t, d = x.shape
    tm = min(TOKEN_BLOCK, t)
    has_pos = pos is not None
    in_specs = [pl.BlockSpec((None, tm, d), lambda i, bb: (bb, i, 0))]
    args = [x]
    if has_pos:
        in_specs.append(pl.BlockSpec((tm, d), lambda i, bb: (i, 0)))
        args.append(pos)
    if mod_row is None:
        mod_map = lambda i, bb: (bb, 0, 0)
    else:
        mod_map = lambda i, bb: (mod_row, 0, 0)
    in_specs += [pl.BlockSpec((None, SUBLANES, d), mod_map),
                 pl.BlockSpec((1, d), lambda i, bb: (0, 0)),
                 pl.BlockSpec(w.shape, lambda i, bb: (0, 0), pipeline_mode=pl.Buffered(1))]
    args += [mods, g, w]
    out_specs = [pl.BlockSpec((None, tm, wd), lambda i, bb: (bb, i, 0)) for wd in widths]
    out_shape = [jax.ShapeDtypeStruct((b, t, wd), BF16) for wd in widths]
    for w32 in casts:
        spec, shape = _cast_specs(w32, (t // tm) * b, lambda i, bb: i * b + bb)
        in_specs.append(spec)
        args.append(w32)
        out_specs.append(spec)
        out_shape.append(shape)
    return pl.pallas_call(
        functools.partial(_proj_kernel, widths=widths, has_pos=has_pos, n_cast=len(casts)),
        grid=(t // tm, b),
        in_specs=in_specs,
        out_specs=out_specs,
        out_shape=out_shape,
        compiler_params=_params("parallel", "parallel"),
        name="proj",
    )(*args)


def _gelu(x):
    return 0.5 * x * (1.0 + jnp.tanh(math.sqrt(2.0 / math.pi) * (x + 0.044715 * x * x * x)))


def _conv_kernel(x_ref, xp_ref, xn_ref, vec_ref, o_ref):
    tm = x_ref.shape[0]
    edge = BF16_ROWS
    i = pl.program_id(0)
    first = i == 0
    last = i == pl.num_programs(0) - 1
    w = [vec_ref[4 + kk:5 + kk, :] for kk in range(CONV_W)]
    bias = vec_ref[3:4, :]

    def taps(xm2, xm1, x0, xp1):
        return bias + w[0] * xm2 + w[1] * xm1 + w[2] * x0 + w[3] * xp1

    x = x_ref[...].astype(F32)
    o_ref[...] = taps(pltpu.roll(x, 2, 0), pltpu.roll(x, 1, 0), x, pltpu.roll(x, tm - 1, 0)).astype(BF16)

    row = lax.broadcasted_iota(jnp.int32, (edge, x.shape[1]), 0)

    def shifted(before, cur, after, s):
        if s < 0:
            return jnp.where(row < -s, pltpu.roll(before, -s, 0), pltpu.roll(cur, -s, 0))
        return jnp.where(row < edge - s, pltpu.roll(cur, edge - s, 0), pltpu.roll(after, edge - s, 0))

    prev = jnp.where(first, 0.0, xp_ref[...].astype(F32))
    nxt = jnp.where(last, 0.0, xn_ref[...].astype(F32))
    head, head2 = x[0:edge], x[edge:2 * edge]
    tail, tail2 = x[tm - edge:tm], x[tm - 2 * edge:tm - edge]
    o_ref[0:edge, :] = taps(shifted(prev, head, head2, -2), shifted(prev, head, head2, -1), head,
                            shifted(prev, head, head2, 1)).astype(BF16)
    o_ref[tm - edge:tm, :] = taps(shifted(tail2, tail, nxt, -2), shifted(tail2, tail, nxt, -1), tail,
                                  shifted(tail2, tail, nxt, 1)).astype(BF16)


def _conv(zx, vecs):
    b, t, c = zx.shape
    tm = min(TOKEN_BLOCK, t)
    assert tm >= 2 * BF16_ROWS
    hpb = tm // BF16_ROWS
    n_halo = t // BF16_ROWS
    return pl.pallas_call(
        _conv_kernel,
        grid=(t // tm, b),
        in_specs=[pl.BlockSpec((None, tm, c), lambda i, bb: (bb, i, 0)),
                  pl.BlockSpec((None, BF16_ROWS, c), lambda i, bb: (bb, jnp.maximum(i * hpb - 1, 0), 0)),
                  pl.BlockSpec((None, BF16_ROWS, c), lambda i, bb: (bb, jnp.minimum((i + 1) * hpb, n_halo - 1), 0)),
                  pl.BlockSpec(vecs.shape, lambda i, bb: (0, 0))],
        out_specs=pl.BlockSpec((None, tm, c), lambda i, bb: (bb, i, 0)),
        out_shape=jax.ShapeDtypeStruct((b, t, c), BF16),
        compiler_params=_params("parallel", "parallel"),
        name="conv",
    )(zx, zx, zx, vecs)


def _scan_kernel(*refs, reverse, fuse):
    if fuse:
        u_ref, vec_ref, wg_ref, h0_ref, hf_ref, zg_ref, o_ref, hT_ref, a_s, b_s, h_s, hc_s = refs
    else:
        u_ref, vec_ref, wg_ref, h0_ref, o_ref, hT_ref, a_s, b_s, h_s, hc_s = refs
    nb, tc, c = u_ref.shape

    @pl.when(pl.program_id(0) == 0)
    def _():
        for hh in range(N_RNN_HEADS):
            hc_s[hh] = h0_ref[:, hh * RNN_HEAD:(hh + 1) * RNN_HEAD]

    lam = vec_ref[0:1, :]
    half_b_a = vec_ref[1:2, :]
    half_b_i = vec_ref[2:3, :]
    k_row = (-0.5 * LRU_C * LOG2E) * (jnp.maximum(-lam, 0.0) + jnp.log1p(jnp.exp(-jnp.abs(lam))))

    for bb in range(nb):
        ub = u_ref[bb]
        u = ub.astype(F32)
        rows = pl.ds(bb, tc, stride=nb)
        for hh in range(N_RNN_HEADS):
            sl = slice(hh * RNN_HEAD, (hh + 1) * RNN_HEAD)
            gates = _dot(ub[:, sl], wg_ref[hh])
            t_r = jnp.tanh(gates[:, :RNN_HEAD] + half_b_a[:, sl])
            t_i = jnp.tanh(gates[:, RNN_HEAD:] + half_b_i[:, sl])
            a = jnp.exp2(k_row[:, sl] * t_r + k_row[:, sl])
            a_s[hh, rows, :] = a
            scale = jnp.exp2((0.5 * LOG2E) * jnp.log(1.0 - a * a)) * u[:, sl]
            b_s[hh, rows, :] = scale * (0.5 * t_i + 0.5)

    def body(j, hs):
        t = (tc - 1 - j) if reverse else j
        idx = pl.ds(pl.multiple_of(t * nb, nb), nb)
        new = []
        for hh in range(N_RNN_HEADS):
            h = a_s[hh, idx, :] * hs[hh] + b_s[hh, idx, :]
            h_s[hh, idx, :] = h
            new.append(h)
        return tuple(new)

    hs = lax.fori_loop(0, tc, body, tuple(hc_s[hh] for hh in range(N_RNN_HEADS)), unroll=8)
    for hh in range(N_RNN_HEADS):
        hc_s[hh] = hs[hh]
        hT_ref[:, hh * RNN_HEAD:(hh + 1) * RNN_HEAD] = hs[hh]

    for bb in range(nb):
        for hh in range(N_RNN_HEADS):
            sl = slice(hh * RNN_HEAD, (hh + 1) * RNN_HEAD)
            hb = h_s[hh, pl.ds(bb, tc, stride=nb), :]
            if fuse:
                hb = _gelu(zg_ref[bb, :, sl].astype(F32)) * (hf_ref[bb, :, sl].astype(F32) + hb)
            o_ref[bb, :, sl] = hb.astype(BF16)


def _cast_specs(w, n_steps, step_index):
    rows, cols = w.shape
    assert rows % (n_steps * BF16_ROWS) == 0
    spec = pl.BlockSpec((rows // n_steps, cols), lambda *idx: (step_index(*idx), 0))
    return spec, jax.ShapeDtypeStruct((rows, cols), BF16)


def _scan(u, vecs, wg, h0, *, reverse, hf=None, zg=None):
    nb, t, c = u.shape
    tc = SCAN_CHUNK
    n_chunks = t // tc
    fuse = hf is not None

    def cidx(i):
        return (n_chunks - 1 - i) if reverse else i

    main_spec = pl.BlockSpec((nb, tc, c), lambda i: (0, cidx(i), 0))
    in_specs = [
        main_spec,
        pl.BlockSpec(vecs.shape, lambda i: (0, 0)),
        pl.BlockSpec(wg.shape, lambda i: (0, 0, 0)),
        pl.BlockSpec(h0.shape, lambda i: (0, 0)),
    ]
    args = [u, vecs, wg, h0]
    if fuse:
        in_specs += [main_spec, main_spec]
        args += [hf, zg]
    return pl.pallas_call(
        functools.partial(_scan_kernel, reverse=reverse, fuse=fuse),
        grid=(n_chunks,),
        in_specs=in_specs,
        out_specs=[main_spec, pl.BlockSpec((nb, c), lambda i: (0, 0))],
        out_shape=[jax.ShapeDtypeStruct((nb, t, c), BF16), jax.ShapeDtypeStruct((nb, c), F32)],
        scratch_shapes=[pltpu.VMEM((N_RNN_HEADS, nb * tc, RNN_HEAD), F32),
                        pltpu.VMEM((N_RNN_HEADS, nb * tc, RNN_HEAD), F32),
                        pltpu.VMEM((N_RNN_HEADS, nb * tc, RNN_HEAD), F32),
                        pltpu.VMEM((N_RNN_HEADS, nb, RNN_HEAD), F32)],
        compiler_params=_params("arbitrary"),
        name="scan_bwd" if reverse else "scan_fwd",
    )(*args)


DFT_ROWS = 64
DFT_COLS = 256


def _dftmm_kernel(zf_ref, cs_ref, c_ref, s_ref, o_ref, zc_s, zs_s, ar_s, ai_s, o_s):
    t, f = zf_ref.shape
    ns = t // RADIX
    rb = min(DFT_ROWS, ns)
    r2 = math.sqrt(0.5)

    cs_table = cs_ref[...].astype(BF16)
    for j in range(RADIX):
        rows = slice(j * ns, (j + 1) * ns)
        for g in range(N_FOURIER_GROUPS):
            sl = slice(g * FOURIER_GROUP, (g + 1) * FOURIER_GROUP)
            cs = _dot(zf_ref[rows, sl], cs_table)
            zc_s[rows, sl] = cs[:, :FOURIER_GROUP].astype(BF16)
            zs_s[rows, sl] = cs[:, FOURIER_GROUP:].astype(BF16)

    def dft4(x):
        s0 = (x[0][0] + x[2][0], x[0][1] + x[2][1])
        s1 = (x[0][0] - x[2][0], x[0][1] - x[2][1])
        s2 = (x[1][0] + x[3][0], x[1][1] + x[3][1])
        s3 = (x[1][0] - x[3][0], x[1][1] - x[3][1])
        return [(s0[0] + s2[0], s0[1] + s2[1]), (s1[0] + s3[1], s1[1] - s3[0]),
                (s0[0] - s2[0], s0[1] - s2[1]), (s1[0] - s3[1], s1[1] + s3[0])]

    def butterfly(i, carry):
        r0 = pl.multiple_of(i * rb, rb)
        w = [(zc_s[pl.ds(j * ns + r0, rb), :].astype(F32), -zs_s[pl.ds(j * ns + r0, rb), :].astype(F32))
             for j in range(RADIX)]
        u = [(w[j][0] + w[j + 4][0], w[j][1] + w[j + 4][1]) for j in range(4)]
        v = [(w[j][0] - w[j + 4][0], w[j][1] - w[j + 4][1]) for j in range(4)]
        v = [v[0],
             ((v[1][0] + v[1][1]) * r2, (v[1][1] - v[1][0]) * r2),
             (v[2][1], -v[2][0]),
             ((v[3][1] - v[3][0]) * r2, (-v[3][0] - v[3][1]) * r2)]
        even = dft4(u)
        odd = dft4(v)
        for m in range(4):
            for k1, a in ((2 * m, even[m]), (2 * m + 1, odd[m])):
                ar_s[k1, pl.ds(r0, rb), :] = a[0].astype(BF16)
                ai_s[k1, pl.ds(r0, rb), :] = a[1].astype(BF16)
        return carry

    lax.fori_loop(0, ns // rb, butterfly, 0)

    for c0 in range(0, f, DFT_COLS):
        for k1 in range(RADIX):
            res = (_dot(c_ref[k1], ar_s[k1, :, c0:c0 + DFT_COLS])
                   + _dot(s_ref[k1], ai_s[k1, :, c0:c0 + DFT_COLS]))
            for j in range(DFT_COLS // LANES):
                o_s[j, pl.ds(k1, ns, stride=RADIX), :] = res[:, j * LANES:(j + 1) * LANES]
        for j in range(DFT_COLS // LANES):
            o_ref[:, c0 + j * LANES:c0 + (j + 1) * LANES] = o_s[j].astype(BF16)


def _channel_dft_table():
    p = np.arange(FOURIER_GROUP)
    ang = 2.0 * np.pi * ((p[:, None] * p[None, :]) % FOURIER_GROUP) / FOURIER_GROUP
    cs = np.concatenate([np.cos(ang), np.sin(ang)], axis=1) / math.sqrt(FOURIER_GROUP)
    return jnp.asarray(cs, F32)


def _dft_apply(ct, st, zf):
    b, t, f = zf.shape
    ns = t // RADIX
    table = pl.BlockSpec((RADIX, ns, ns), lambda bb: (0, 0, 0), pipeline_mode=pl.Buffered(1))
    cs = _channel_dft_table()
    return pl.pallas_call(
        _dftmm_kernel,
        grid=(b,),
        in_specs=[pl.BlockSpec((None, t, f), lambda bb: (bb, 0, 0)),
                  pl.BlockSpec(cs.shape, lambda bb: (0, 0)),
                  table, table],
        out_specs=pl.BlockSpec((None, t, f), lambda bb: (bb, 0, 0)),
        out_shape=jax.ShapeDtypeStruct((b, t, f), BF16),
        scratch_shapes=[pltpu.VMEM((t, f), BF16),
                        pltpu.VMEM((t, f), BF16),
                        pltpu.VMEM((RADIX, ns, f), BF16),
                        pltpu.VMEM((RADIX, ns, f), BF16),
                        pltpu.VMEM((DFT_COLS // LANES, t, LANES), F32)],
        compiler_params=_params("parallel"),
        name="dft_apply",
    )(zf, cs, ct, st)


def _merge_kernel(f_ref, gr_ref, za_ref, zb_ref, x_ref, pos_ref, mod_ref, g2_ref,
                  wf_ref, wl_ref, wo_ref, wr_ref, x1_ref, h2_ref, sc_ref, scr_ref):
    y_four = _dot(f_ref[...], wf_ref[...])
    y_rnn = _dot(gr_ref[...], wl_ref[...])
    m = _sigmoid(za_ref[...].astype(F32)) * y_four + _sigmoid(zb_ref[...].astype(F32)) * y_rnn
    y = _dot(m.astype(BF16), wo_ref[...])
    x1 = (x_ref[...] + pos_ref[...]) + mod_ref[2:3, :] * y
    x1_ref[...] = x1
    h2 = (_rmsnorm(x1, g2_ref[...]) * (1.0 + mod_ref[4:5, :]) + mod_ref[3:4, :]).astype(BF16)
    h2_ref[...] = h2
    logits = lax.dot_general(wr_ref[...], h2, (((1,), (1,)), ((), ())), preferred_element_type=F32)
    e = jnp.exp(logits - jnp.max(logits, axis=0, keepdims=True))
    sc = e / jnp.sum(e, axis=0, keepdims=True)
    sc_ref[...] = sc
    hi = sc.astype(BF16).astype(F32)
    lo = sc - hi
    tm = sc.shape[1]
    row = lax.broadcasted_iota(jnp.int32, (BF16_ROWS, tm), 0)
    for ex in range(N_EXPERTS):
        tile = jnp.where(row == 0, hi[ex:ex + 1, :], jnp.where(row == 1, lo[ex:ex + 1, :], 0.0))
        scr_ref[ex] = tile.astype(BF16)


def _merge(f, gr, za, zb, x, pos, mods, g2, w_four, w_lru, w_out, w_rt):
    b, t, d = x.shape
    tm = min(TOKEN_BLOCK, t)
    tok = lambda wd: pl.BlockSpec((None, tm, wd), lambda i, bb: (bb, i, 0))
    full = lambda a: pl.BlockSpec(a.shape, lambda i, bb: (0,) * a.ndim)
    return pl.pallas_call(
        _merge_kernel,
        grid=(t // tm, b),
        in_specs=[tok(D_FOURIER), tok(d), tok(d), tok(d), tok(d),
                  pl.BlockSpec((tm, d), lambda i, bb: (i, 0)),
                  pl.BlockSpec((None, SUBLANES, d), lambda i, bb: (bb, 0, 0)),
                  full(g2), full(w_four), full(w_lru), full(w_out), full(w_rt)],
        out_specs=[tok(d), tok(d), pl.BlockSpec((None, N_EXPERTS, tm), lambda i, bb: (bb, 0, i)),
                   pl.BlockSpec((None, N_EXPERTS, BF16_ROWS, tm), lambda i, bb: (bb, 0, 0, i))],
        out_shape=[jax.ShapeDtypeStruct((b, t, d), F32),
                   jax.ShapeDtypeStruct((b, t, d), BF16),
                   jax.ShapeDtypeStruct((b, N_EXPERTS, t), F32),
                   jax.ShapeDtypeStruct((b, N_EXPERTS, BF16_ROWS, t), BF16)],
        compiler_params=_params("parallel", "parallel"),
        name="merge",
    )(f, gr, za, zb, x, pos, mods, g2, w_four, w_lru, w_out, w_rt)


MAX_EXP = 127
EXP_STEPS = 7
VAL_STEPS = 56


OFF_CHUNK = 256


def _select_kernel(s_ref, slot_ref, offs_ref, *, cap):
    rows, t = s_ref.shape

    def count(mask):
        return jnp.sum(jnp.where(mask, 1.0, 0.0), axis=1, keepdims=True)

    def exp_step(_, carry):
        k_lo, k_hi = carry
        k_mid = jnp.floor(0.5 * (k_lo + k_hi))
        ok = count(s_ref[...] >= jnp.exp2(-k_mid)) >= cap
        return jnp.where(ok, k_lo, k_mid), jnp.where(ok, k_mid, k_hi)

    k_lo, k_hi = lax.fori_loop(
        0, EXP_STEPS, exp_step,
        (jnp.full((rows, 1), -1.0, F32), jnp.full((rows, 1), float(MAX_EXP), F32)))
    lo0 = jnp.where(k_hi >= MAX_EXP, 0.0, jnp.exp2(-k_hi))
    hi0 = jnp.exp2(-k_lo)

    def val_step(_, carry):
        lo, hi = carry
        mid = 0.5 * (lo + hi)
        ok = count(s_ref[...] >= mid) >= cap
        return jnp.where(ok, mid, lo), jnp.where(ok, hi, mid)

    lo, hi = lax.fori_loop(0, VAL_STEPS, val_step, (lo0, hi0))
    s = s_ref[...]
    gt = s >= hi
    eq = (s >= lo) & (s < hi)
    need = cap - count(gt)

    p = lax.broadcasted_iota(jnp.int32, (LANES, LANES), 0)
    q = lax.broadcasted_iota(jnp.int32, (LANES, LANES), 1)
    tri = (p < q).astype(BF16)
    run_gt = jnp.zeros((rows, 1), F32)
    run_eq = jnp.zeros((rows, 1), F32)
    off_lane = lax.broadcasted_iota(jnp.int32, (rows, LANES), 1)
    offs = jnp.zeros((rows, LANES), F32)
    for j in range(t // LANES):
        if (j * LANES) % OFF_CHUNK == 0:
            offs = jnp.where(off_lane == (j * LANES) // OFF_CHUNK, run_gt + jnp.minimum(run_eq, need), offs)
        sl = slice(j * LANES, (j + 1) * LANES)
        g = gt[:, sl]
        q_ = eq[:, sl]
        pg = _dot(g.astype(BF16), tri) + run_gt
        pe = _dot(q_.astype(BF16), tri) + run_eq
        sel = g | (q_ & (pe < need))
        pos = pg + jnp.minimum(pe, need)
        slot_ref[:, sl] = jnp.where(sel, pos, -1.0).astype(jnp.int32)
        run_gt = run_gt + count(g)
        run_eq = run_eq + count(q_)
    offs = jnp.where(off_lane == t // OFF_CHUNK, run_gt + jnp.minimum(run_eq, need), offs)
    offs_ref[...] = offs.astype(jnp.int32)


def _select(scores, cap):
    rows, t = scores.shape
    assert t % OFF_CHUNK == 0 and t // OFF_CHUNK < LANES
    return pl.pallas_call(
        functools.partial(_select_kernel, cap=cap),
        out_shape=[jax.ShapeDtypeStruct((rows, t), jnp.int32),
                   jax.ShapeDtypeStruct((rows, LANES), jnp.int32)],
        compiler_params=_params(),
        name="select",
    )(scores)


GATHER_BLOCK = 512
GATHER_WINDOW = 128


def _expert_kernel(offs_ref, slot_ref, sc_ref, h_ref, wg_ref, wu_ref, wd_ref, o_ref,
                   xg_s, gs_s, rows_s, gates_s, *, cap, n_off):
    t = h_ref.shape[0]
    tk = min(GATHER_BLOCK, t)
    win = min(GATHER_WINDOW, cap)
    per_block = tk // OFF_CHUNK
    slot_iota = lax.broadcasted_iota(jnp.int32, (win, tk), 0)
    xg_s[...] = jnp.zeros_like(xg_s)
    gs_s[...] = jnp.zeros_like(gs_s)
    row0 = (pl.program_id(0) * N_EXPERTS + pl.program_id(1)) * n_off

    def gather(sl, lo):
        onehot = (slot_iota + lo == slot_ref[:, sl]).astype(BF16)
        rows = _dot(onehot, h_ref[sl, :])
        gates = lax.dot_general(onehot, sc_ref[:, sl], (((1,), (1,)), ((), ())), preferred_element_type=F32)
        return rows, gates

    n_blocks = t // tk
    starts = [pl.multiple_of((offs_ref[row0 + j * per_block] // SUBLANES) * SUBLANES, SUBLANES)
              for j in range(n_blocks)]
    for j in range(n_blocks):
        rows_s[j], gates_s[j] = gather(slice(j * tk, (j + 1) * tk), starts[j])
    for j in range(n_blocks):
        sl = slice(j * tk, (j + 1) * tk)
        st = starts[j]
        last = offs_ref[row0 + (j + 1) * per_block]
        xg_s[pl.ds(st, SUBLANES), :] += rows_s[j, 0:SUBLANES, :]
        xg_s[pl.ds(st + SUBLANES, win - SUBLANES), :] = rows_s[j, SUBLANES:win, :]
        gs_s[pl.ds(st, win), :] += gates_s[j]

        def more(c, carry, sl=sl, st=st):
            lo = pl.multiple_of(st + c * win, SUBLANES)
            rows, gates = gather(sl, lo)
            xg_s[pl.ds(lo, win), :] = rows
            gs_s[pl.ds(lo, win), :] += gates
            return carry

        lax.fori_loop(1, (last - st + win - 1) // win, more, 0)
    xb = xg_s[0:cap, :].astype(BF16)
    gate = _dot(xb, wg_ref[...])
    hid = (gate * _sigmoid(gate)) * _dot(xb, wu_ref[...])
    y = _dot(hid.astype(BF16), wd_ref[...])
    o_ref[...] = (y * (gs_s[0:cap, 0:1] + gs_s[0:cap, 1:2])).astype(BF16)


def _experts(offs, slot, score_rows, h2, wg, wu, wd, cap):
    b, t, d = h2.shape
    n_off = offs.shape[0] // (b * N_EXPERTS)
    win = min(GATHER_WINDOW, cap)
    grid_spec = pltpu.PrefetchScalarGridSpec(
        num_scalar_prefetch=1,
        grid=(b, N_EXPERTS),
        in_specs=[pl.BlockSpec((None, 1, t), lambda bb, e, o: (bb * N_EXPERTS + e, 0, 0)),
                  pl.BlockSpec((None, None, BF16_ROWS, t), lambda bb, e, o: (bb, e, 0, 0)),
                  pl.BlockSpec((None, t, d), lambda bb, e, o: (bb, 0, 0)),
                  pl.BlockSpec((None, d, D_EXPERT), lambda bb, e, o: (e, 0, 0)),
                  pl.BlockSpec((None, d, D_EXPERT), lambda bb, e, o: (e, 0, 0)),
                  pl.BlockSpec((None, D_EXPERT, d), lambda bb, e, o: (e, 0, 0))],
        out_specs=pl.BlockSpec((None, cap, d), lambda bb, e, o: (bb * N_EXPERTS + e, 0, 0)),
        scratch_shapes=[pltpu.VMEM((cap + win, d), F32), pltpu.VMEM((cap + win, BF16_ROWS), F32),
                        pltpu.VMEM((t // min(GATHER_BLOCK, t), win, d), F32),
                        pltpu.VMEM((t // min(GATHER_BLOCK, t), win, BF16_ROWS), F32)],
    )
    return pl.pallas_call(
        functools.partial(_expert_kernel, cap=cap, n_off=n_off),
        grid_spec=grid_spec,
        out_shape=jax.ShapeDtypeStruct((b * N_EXPERTS, cap, d), BF16),
        compiler_params=_params("parallel", "parallel"),
        name="experts",
    )(offs, slot.reshape(b * N_EXPERTS, 1, t), score_rows, h2, wg, wu, wd)


COMBINE_BLOCK = 1024
SLOT_WINDOW = 256


COMBINE_GROUP = 8


def _combine_kernel(offs_ref, slot_ref, y_ref, x1_ref, mod_ref, g_ref, o_ref, acc_s, oh_s, yw_s, *, cap, n_off):
    tm = x1_ref.shape[0]
    kc = min(SLOT_WINDOW, cap)
    per_block = tm // OFF_CHUNK
    lane_iota = lax.broadcasted_iota(jnp.int32, (tm, kc), 1)

    def place(e, start0, c):
        lo = start0 + c * kc
        st = pl.multiple_of(jnp.minimum(lo, cap - kc), BF16_ROWS)
        onehot = slot_ref[:, e:e + 1] == st + lane_iota
        if not isinstance(c, int) or c > 0:
            onehot = onehot & (slot_ref[:, e:e + 1] >= lo)
        return onehot.astype(BF16), st

    starts, extras = [], []
    for e in range(N_EXPERTS):
        o0 = (pl.program_id(0) * N_EXPERTS + e) * n_off + pl.program_id(1) * per_block
        start0 = (offs_ref[o0] // BF16_ROWS) * BF16_ROWS
        starts.append(start0)
        extras.append(jnp.maximum((offs_ref[o0 + per_block] - start0 + kc - 1) // kc - 1, 0))

    acc = None
    for g0 in range(0, N_EXPERTS, COMBINE_GROUP):
        for i in range(COMBINE_GROUP):
            onehot, st = place(g0 + i, starts[g0 + i], 0)
            oh_s[:, i * kc:(i + 1) * kc] = onehot
            yw_s[i * kc:(i + 1) * kc, :] = y_ref[pl.ds((g0 + i) * cap + st, kc), :]
        part = _dot(oh_s[...], yw_s[...])
        acc = part if acc is None else acc + part

    def finish(total):
        o_ref[...] = _rmsnorm(x1_ref[...] + mod_ref[5:6, :] * total, g_ref[...])

    finish(acc)

    @pl.when(sum(extras) > 0)
    def _():
        acc_s[...] = acc
        for e in range(N_EXPERTS):
            def more(c, carry, e=e):
                onehot, st = place(e, starts[e], c)
                acc_s[...] += _dot(onehot, y_ref[pl.ds(e * cap + st, kc), :])
                return carry
            lax.fori_loop(1, extras[e] + 1, more, 0)
        finish(acc_s[...])


def _combine(offs, slot_t, y, x1, mods, g, cap):
    b, t, d = x1.shape
    tm = min(COMBINE_BLOCK, t)
    n_off = offs.shape[0] // (b * N_EXPERTS)
    grid_spec = pltpu.PrefetchScalarGridSpec(
        num_scalar_prefetch=1,
        grid=(b, t // tm),
        in_specs=[pl.BlockSpec((None, tm, N_EXPERTS), lambda bb, i, o: (bb, i, 0)),
                  pl.BlockSpec((None, N_EXPERTS * cap, d), lambda bb, i, o: (bb, 0, 0),
                               pipeline_mode=pl.Buffered(1)),
                  pl.BlockSpec((None, tm, d), lambda bb, i, o: (bb, i, 0)),
                  pl.BlockSpec((None, SUBLANES, d), lambda bb, i, o: (bb, 0, 0)),
                  pl.BlockSpec((1, d), lambda bb, i, o: (0, 0))],
        out_specs=pl.BlockSpec((None, tm, d), lambda bb, i, o: (bb, i, 0)),
        scratch_shapes=[pltpu.VMEM((tm, d), F32),
                        pltpu.VMEM((tm, COMBINE_GROUP * min(SLOT_WINDOW, cap)), BF16),
                        pltpu.VMEM((COMBINE_GROUP * min(SLOT_WINDOW, cap), d), BF16)],
    )
    return pl.pallas_call(
        functools.partial(_combine_kernel, cap=cap, n_off=n_off),
        grid_spec=grid_spec,
        out_shape=jax.ShapeDtypeStruct((b, t, d), F32),
        compiler_params=_params("parallel", "parallel"),
        name="combine",
    )(offs, slot_t, y, x1, mods, g)


def kernel(x, c, ctx, c_ctx, w_ada, b_ada, norm1_g, norm2_g, w_in, w_four, conv_w, conv_b, lru_lambda, lru_wa, lru_ba, lru_wi, lru_bi, w_lru, w_out, w_router, w_gate_e, w_up_e, w_down_e, final_g):
    b, t, d = x.shape
    assert d == D_MODEL and b == SUBLANES and t % TOKEN_BLOCK == 0 and ctx.shape[1] % SCAN_CHUNK == 0
    assert w_ada.shape[0] == 1, "single-layer problem: the context stream is only read by the recurrence"
    cap = CAPACITY_FACTOR * t // N_EXPERTS
    col_x = D_FOURIER + D_RNN
    col_g = col_x + D_RNN
    col_a = col_g + D_MODEL

    pos = _pos_table(t)
    ct, st = _dft_tables(t)

    cc = jnp.concatenate([c, c_ctx[None], jnp.zeros((SUBLANES - 1, d), F32)], axis=0)
    mods = _ada(cc, w_ada[0], b_ada[0]).reshape(cc.shape[0], N_MOD, d)
    mods = jnp.pad(mods, ((0, 0), (0, SUBLANES - N_MOD), (0, 0)))

    w = w_in[0].astype(BF16)
    w_x = w[:, D_FOURIER:col_x]
    w_cat = jnp.concatenate([w[:, D_FOURIER:], w[:, :D_FOURIER]], axis=1)
    g1 = norm1_g[0].reshape(1, d)
    zx, zg, za, zb, zf, wg_b, wu_b, wd_b = _proj(
        x, pos, mods, None, g1, w_cat, (D_RNN, D_RNN, D_MODEL, D_MODEL, D_FOURIER),
        casts=(w_gate_e[0].reshape(N_EXPERTS * d, D_EXPERT), w_up_e[0].reshape(N_EXPERTS * d, D_EXPERT),
               w_down_e[0].reshape(N_EXPERTS * D_EXPERT, d)))
    (zx_ctx,) = _proj(ctx, None, mods, b, g1, w_x, (D_RNN,))

    for dr in range(2):
        vecs = jnp.concatenate([lru_lambda[0, dr][None], 0.5 * lru_ba[0, dr][None], 0.5 * lru_bi[0, dr][None],
                                conv_b[0][None], conv_w[0]], axis=0)
        wg = (0.5 * jnp.concatenate([lru_wa[0, dr], lru_wi[0, dr]], axis=-1)).astype(BF16)
        if dr == 0:
            u_lat = _conv(zx, vecs)
            u_ctx = _conv(zx_ctx, vecs)
        _, h_ctx = _scan(u_ctx, vecs, wg, jnp.zeros((b, D_RNN), F32), reverse=bool(dr))
        if dr == 0:
            h_fwd, _ = _scan(u_lat, vecs, wg, h_ctx, reverse=False)
        else:
            gr, _ = _scan(u_lat, vecs, wg, h_ctx, reverse=True, hf=h_fwd, zg=zg)

    f = _dft_apply(ct, st, zf)

    x1, h2, scores, score_rows = _merge(
        f, gr, za, zb, x, pos, mods, norm2_g[0].reshape(1, d),
        w_four[0].astype(BF16), w_lru[0].astype(BF16), w_out[0].astype(BF16), w_router[0].T.astype(BF16))

    slot, offs = _select(scores.reshape(b * N_EXPERTS, t), cap)
    offs = offs[:, :t // OFF_CHUNK + 1].reshape(-1)
    y = _experts(offs, slot, score_rows, h2, wg_b.reshape(N_EXPERTS, d, D_EXPERT),
                 wu_b.reshape(N_EXPERTS, d, D_EXPERT), wd_b.reshape(N_EXPERTS, D_EXPERT, d), cap)
    slot_t = jnp.swapaxes(slot.reshape(b, N_EXPERTS, t), 1, 2)
    return _combine(offs, slot_t, y.reshape(b, N_EXPERTS * cap, d), x1, mods, final_g.reshape(1, d), cap)
```

```python
import functools
import math

import numpy as np
import jax
import jax.numpy as jnp
from jax import lax
from jax.experimental import pallas as pl
from jax.experimental.pallas import tpu as pltpu

F32 = jnp.float32
BF16 = jnp.bfloat16

D_MODEL = 1024
GRID_W = 64
N_FOURIER_GROUPS = 4
FOURIER_GROUP = 128
D_FOURIER = N_FOURIER_GROUPS * FOURIER_GROUP
D_RNN = D_MODEL
N_RNN_HEADS = 8
RNN_HEAD = D_RNN // N_RNN_HEADS
CONV_W = 4
CONV_LEFT = 2
LRU_C = 8.0
LOG2E = math.log2(math.e)
N_EXPERTS = 16
CAPACITY_FACTOR = 2
D_EXPERT = 1536
N_MOD = 6
EPS = 1e-6
POS_MAX_PERIOD = 10000.0

LANES = 128
SUBLANES = 8
BF16_ROWS = 16
VMEM_LIMIT = 58 * 1024 * 1024

TOKEN_BLOCK = 512
ROW_GROUPS = 2
SCAN_CHUNK = 128


def _params(*sem):
    return pltpu.CompilerParams(dimension_semantics=sem, vmem_limit_bytes=VMEM_LIMIT)


def _sigmoid(x):
    return 0.5 * jnp.tanh(0.5 * x) + 0.5


def _rmsnorm(x, g):
    return x * lax.rsqrt(jnp.mean(x * x, axis=-1, keepdims=True) + EPS) * g


def _dot(a, b):
    return jnp.dot(a, b, preferred_element_type=F32)


def _pos_kernel(o_ref):
    rows = o_ref.shape[0]
    q = D_MODEL // 4
    k = lax.broadcasted_iota(jnp.int32, (1, q), 1).astype(F32)
    freqs = jnp.exp(-math.log(POS_MAX_PERIOD) * k / q)
    ang_r = lax.broadcasted_iota(jnp.int32, (rows, q), 0).astype(F32) * freqs
    ang_c = lax.broadcasted_iota(jnp.int32, (GRID_W, q), 0).astype(F32) * freqs
    shape = (rows, GRID_W, q)
    o_ref[:, :, 0 * q:1 * q] = jnp.broadcast_to(jnp.sin(ang_r)[:, None, :], shape)
    o_ref[:, :, 1 * q:2 * q] = jnp.broadcast_to(jnp.cos(ang_r)[:, None, :], shape)
    o_ref[:, :, 2 * q:3 * q] = jnp.broadcast_to(jnp.sin(ang_c)[None, :, :], shape)
    o_ref[:, :, 3 * q:4 * q] = jnp.broadcast_to(jnp.cos(ang_c)[None, :, :], shape)


def _pos_table(n_tokens):
    rows = n_tokens // GRID_W
    out = pl.pallas_call(
        _pos_kernel,
        out_shape=jax.ShapeDtypeStruct((rows, GRID_W, D_MODEL), F32),
        name="pos_table",
    )()
    return out.reshape(n_tokens, D_MODEL)


RADIX = 8


def _dft_kernel(c_ref, s_ref, *, n):
    ns = c_ref.shape[1]
    k2 = lax.broadcasted_iota(jnp.int32, (ns, ns), 0)
    t2 = lax.broadcasted_iota(jnp.int32, (ns, ns), 1)
    common = ((k2 * t2) & (ns - 1)).astype(F32) * (2.0 * math.pi / ns)
    cos_c = jnp.cos(common)
    sin_c = jnp.sin(common)
    scale = 1.0 / math.sqrt(n)
    t_row = lax.broadcasted_iota(jnp.int32, (1, ns), 1)
    for k1 in range(RADIX):
        slab = (k1 * t_row).astype(F32) * (2.0 * math.pi / n)
        cos_s = jnp.cos(slab) * scale
        sin_s = jnp.sin(slab) * scale
        c_ref[k1] = (cos_c * cos_s - sin_c * sin_s).astype(BF16)
        s_ref[k1] = (sin_c * cos_s + cos_c * sin_s).astype(BF16)


def _dft_tables(n):
    assert n & (n - 1) == 0 and n % (RADIX * BF16_ROWS) == 0
    ns = n // RADIX
    return pl.pallas_call(
        functools.partial(_dft_kernel, n=n),
        out_shape=[jax.ShapeDtypeStruct((RADIX, ns, ns), BF16)] * 2,
        compiler_params=_params(),
        name="dft_tables",
    )()


def _ada_kernel(c_ref, w_ref, b_ref, o_ref):
    c = c_ref[...]
    o_ref[...] = jnp.dot(c * _sigmoid(c), w_ref[...], preferred_element_type=F32,
                         precision=lax.Precision.HIGHEST) + b_ref[...]


def _ada(cc, w_ada, b_ada):
    n = w_ada.shape[1]
    tn = D_MODEL
    return pl.pallas_call(
        _ada_kernel,
        grid=(n // tn,),
        in_specs=[pl.BlockSpec(cc.shape, lambda j: (0, 0)),
                  pl.BlockSpec((D_MODEL, tn), lambda j: (0, j)),
                  pl.BlockSpec((1, tn), lambda j: (0, j))],
        out_specs=pl.BlockSpec((cc.shape[0], tn), lambda j: (0, j)),
        out_shape=jax.ShapeDtypeStruct((cc.shape[0], n), F32),
        compiler_params=_params("parallel"),
        name="ada",
    )(cc, w_ada, b_ada.reshape(1, n))


def _proj_kernel(*refs, widths, has_pos, n_cast):
    refs = list(refs)
    if n_cast:
        outs_cast = refs[-n_cast:]
        del refs[-n_cast:]
        n_in = len(refs) - len(widths)
        for w32_ref, w16_ref in zip(refs[n_in - n_cast:n_in], outs_cast):
            w16_ref[...] = w32_ref[...].astype(BF16)
        del refs[n_in - n_cast:n_in]
    if has_pos:
        x_ref, pos_ref, mod_ref, g_ref, w_ref = refs[:5]
        outs = refs[5:]
    else:
        x_ref, mod_ref, g_ref, w_ref = refs[:4]
        outs = refs[4:]
    tg = x_ref.shape[0] // ROW_GROUPS
    for grp in range(ROW_GROUPS):
        rs = slice(grp * tg, (grp + 1) * tg)
        x = x_ref[rs, :] + pos_ref[rs, :] if has_pos else x_ref[rs, :]
        h = _rmsnorm(x, g_ref[...]) * (1.0 + mod_ref[1:2, :]) + mod_ref[0:1, :]
        hb = h.astype(BF16)
        c0 = 0
        for o_ref, wd in zip(outs, widths):
            o_ref[rs, :] = _dot(hb, w_ref[:, c0:c0 + wd]).astype(BF16)
            c0 += wd


def _proj(x, pos, mods, mod_row, g, w, widths, casts=()):
    b, t, d = x.shape
    tm = min(TOKEN_BLOCK, t)
    has_pos = pos is not None
    in_specs = [pl.BlockSpec((None, tm, d), lambda i, bb: (bb, i, 0))]
    args = [x]
    if has_pos:
        in_specs.append(pl.BlockSpec((tm, d), lambda i, bb: (i, 0)))
        args.append(pos)
    if mod_row is None:
        mod_map = lambda i, bb: (bb, 0, 0)
    else:
        mod_map = lambda i, bb: (mod_row, 0, 0)
    in_specs += [pl.BlockSpec((None, SUBLANES, d), mod_map),
                 pl.BlockSpec((1, d), lambda i, bb: (0, 0)),
                 pl.BlockSpec(w.shape, lambda i, bb: (0, 0), pipeline_mode=pl.Buffered(1))]
    args += [mods, g, w]
    out_specs = [pl.BlockSpec((None, tm, wd), lambda i, bb: (bb, i, 0)) for wd in widths]
    out_shape = [jax.ShapeDtypeStruct((b, t, wd), BF16) for wd in widths]
    for w32 in casts:
        spec, shape = _cast_specs(w32, (t // tm) * b, lambda i, bb: i * b + bb)
        in_specs.append(spec)
        args.append(w32)
        out_specs.append(spec)
        out_shape.append(shape)
    return pl.pallas_call(
        functools.partial(_proj_kernel, widths=widths, has_pos=has_pos, n_cast=len(casts)),
        grid=(t // tm, b),
        in_specs=in_specs,
        out_specs=out_specs,
        out_shape=out_shape,
        compiler_params=_params("parallel", "parallel"),
        name="proj",
    )(*args)


def _gelu(x):
    return 0.5 * x * (1.0 + jnp.tanh(math.sqrt(2.0 / math.pi) * (x + 0.044715 * x * x * x)))


def _conv_kernel(x_ref, xp_ref, xn_ref, vec_ref, o_ref):
    tm = x_ref.shape[0]
    edge = BF16_ROWS
    i = pl.program_id(0)
    first = i == 0
    last = i == pl.num_programs(0) - 1
    w = [vec_ref[4 + kk:5 + kk, :] for kk in range(CONV_W)]
    bias = vec_ref[3:4, :]

    def taps(xm2, xm1, x0, xp1):
        return bias + w[0] * xm2 + w[1] * xm1 + w[2] * x0 + w[3] * xp1

    x = x_ref[...].astype(F32)
    o_ref[...] = taps(pltpu.roll(x, 2, 0), pltpu.roll(x, 1, 0), x, pltpu.roll(x, tm - 1, 0)).astype(BF16)

    row = lax.broadcasted_iota(jnp.int32, (edge, x.shape[1]), 0)

    def shifted(before, cur, after, s):
        if s < 0:
            return jnp.where(row < -s, pltpu.roll(before, -s, 0), pltpu.roll(cur, -s, 0))
        return jnp.where(row < edge - s, pltpu.roll(cur, edge - s, 0), pltpu.roll(after, edge - s, 0))

    prev = jnp.where(first, 0.0, xp_ref[...].astype(F32))
    nxt = jnp.where(last, 0.0, xn_ref[...].astype(F32))
    head, head2 = x[0:edge], x[edge:2 * edge]
    tail, tail2 = x[tm - edge:tm], x[tm - 2 * edge:tm - edge]
    o_ref[0:edge, :] = taps(shifted(prev, head, head2, -2), shifted(prev, head, head2, -1), head,
                            shifted(prev, head, head2, 1)).astype(BF16)
    o_ref[tm - edge:tm, :] = taps(shifted(tail2, tail, nxt, -2), shifted(tail2, tail, nxt, -1), tail,
                                  shifted(tail2, tail, nxt, 1)).astype(BF16)


def _conv(zx, vecs):
    b, t, c = zx.shape
    tm = min(TOKEN_BLOCK, t)
    assert tm >= 2 * BF16_ROWS
    hpb = tm // BF16_ROWS
    n_halo = t // BF16_ROWS
    return pl.pallas_call(
        _conv_kernel,
        grid=(t // tm, b),
        in_specs=[pl.BlockSpec((None, tm, c), lambda i, bb: (bb, i, 0)),
                  pl.BlockSpec((None, BF16_ROWS, c), lambda i, bb: (bb, jnp.maximum(i * hpb - 1, 0), 0)),
                  pl.BlockSpec((None, BF16_ROWS, c), lambda i, bb: (bb, jnp.minimum((i + 1) * hpb, n_halo - 1), 0)),
                  pl.BlockSpec(vecs.shape, lambda i, bb: (0, 0))],
        out_specs=pl.BlockSpec((None, tm, c), lambda i, bb: (bb, i, 0)),
        out_shape=jax.ShapeDtypeStruct((b, t, c), BF16),
        compiler_params=_params("parallel", "parallel"),
        name="conv",
    )(zx, zx, zx, vecs)


def _scan_kernel(*refs, reverse, fuse):
    if fuse:
        u_ref, vec_ref, wg_ref, h0_ref, hf_ref, zg_ref, o_ref, hT_ref, a_s, b_s, h_s, hc_s = refs
    else:
        u_ref, vec_ref, wg_ref, h0_ref, o_ref, hT_ref, a_s, b_s, h_s, hc_s = refs
    nb, tc, c = u_ref.shape

    @pl.when(pl.program_id(0) == 0)
    def _():
        for hh in range(N_RNN_HEADS):
            hc_s[hh] = h0_ref[:, hh * RNN_HEAD:(hh + 1) * RNN_HEAD]

    lam = vec_ref[0:1, :]
    half_b_a = vec_ref[1:2, :]
    half_b_i = vec_ref[2:3, :]
    k_row = (-0.5 * LRU_C * LOG2E) * (jnp.maximum(-lam, 0.0) + jnp.log1p(jnp.exp(-jnp.abs(lam))))

    for bb in range(nb):
        ub = u_ref[bb]
        u = ub.astype(F32)
        rows = pl.ds(bb, tc, stride=nb)
        for hh in range(N_RNN_HEADS):
            sl = slice(hh * RNN_HEAD, (hh + 1) * RNN_HEAD)
            gates = _dot(ub[:, sl], wg_ref[hh])
            t_r = jnp.tanh(gates[:, :RNN_HEAD] + half_b_a[:, sl])
            t_i = jnp.tanh(gates[:, RNN_HEAD:] + half_b_i[:, sl])
            a = jnp.exp2(k_row[:, sl] * t_r + k_row[:, sl])
            a_s[hh, rows, :] = a
            scale = jnp.exp2((0.5 * LOG2E) * jnp.log(1.0 - a * a)) * u[:, sl]
            b_s[hh, rows, :] = scale * (0.5 * t_i + 0.5)

    def body(j, hs):
        t = (tc - 1 - j) if reverse else j
        idx = pl.ds(pl.multiple_of(t * nb, nb), nb)
        new = []
        for hh in range(N_RNN_HEADS):
            h = a_s[hh, idx, :] * hs[hh] + b_s[hh, idx, :]
            h_s[hh, idx, :] = h
            new.append(h)
        return tuple(new)

    hs = lax.fori_loop(0, tc, body, tuple(hc_s[hh] for hh in range(N_RNN_HEADS)), unroll=8)
    for hh in range(N_RNN_HEADS):
        hc_s[hh] = hs[hh]
        hT_ref[:, hh * RNN_HEAD:(hh + 1) * RNN_HEAD] = hs[hh]

    for bb in range(nb):
        for hh in range(N_RNN_HEADS):
            sl = slice(hh * RNN_HEAD, (hh + 1) * RNN_HEAD)
            hb = h_s[hh, pl.ds(bb, tc, stride=nb), :]
            if fuse:
                hb = _gelu(zg_ref[bb, :, sl].astype(F32)) * (hf_ref[bb, :, sl].astype(F32) + hb)
            o_ref[bb, :, sl] = hb.astype(BF16)


def _cast_specs(w, n_steps, step_index):
    rows, cols = w.shape
    assert rows % (n_steps * BF16_ROWS) == 0
    spec = pl.BlockSpec((rows // n_steps, cols), lambda *idx: (step_index(*idx), 0))
    return spec, jax.ShapeDtypeStruct((rows, cols), BF16)


def _scan(u, vecs, wg, h0, *, reverse, hf=None, zg=None):
    nb, t, c = u.shape
    tc = SCAN_CHUNK
    n_chunks = t // tc
    fuse = hf is not None

    def cidx(i):
        return (n_chunks - 1 - i) if reverse else i

    main_spec = pl.BlockSpec((nb, tc, c), lambda i: (0, cidx(i), 0))
    in_specs = [
        main_spec,
        pl.BlockSpec(vecs.shape, lambda i: (0, 0)),
        pl.BlockSpec(wg.shape, lambda i: (0, 0, 0)),
        pl.BlockSpec(h0.shape, lambda i: (0, 0)),
    ]
    args = [u, vecs, wg, h0]
    if fuse:
        in_specs += [main_spec, main_spec]
        args += [hf, zg]
    return pl.pallas_call(
        functools.partial(_scan_kernel, reverse=reverse, fuse=fuse),
        grid=(n_chunks,),
        in_specs=in_specs,
        out_specs=[main_spec, pl.BlockSpec((nb, c), lambda i: (0, 0))],
        out_shape=[jax.ShapeDtypeStruct((nb, t, c), BF16), jax.ShapeDtypeStruct((nb, c), F32)],
        scratch_shapes=[pltpu.VMEM((N_RNN_HEADS, nb * tc, RNN_HEAD), F32),
                        pltpu.VMEM((N_RNN_HEADS, nb * tc, RNN_HEAD), F32),
                        pltpu.VMEM((N_RNN_HEADS, nb * tc, RNN_HEAD), F32),
                        pltpu.VMEM((N_RNN_HEADS, nb, RNN_HEAD), F32)],
        compiler_params=_params("arbitrary"),
        name="scan_bwd" if reverse else "scan_fwd",
    )(*args)


DFT_ROWS = 64
DFT_COLS = 256


def _dftmm_kernel(zf_ref, cs_ref, c_ref, s_ref, o_ref, zc_s, zs_s, ar_s, ai_s, o_s):
    t, f = zf_ref.shape
    ns = t // RADIX
    rb = min(DFT_ROWS, ns)
    r2 = math.sqrt(0.5)

    cs_table = cs_ref[...].astype(BF16)
    for j in range(RADIX):
        rows = slice(j * ns, (j + 1) * ns)
        for g in range(N_FOURIER_GROUPS):
            sl = slice(g * FOURIER_GROUP, (g + 1) * FOURIER_GROUP)
            cs = _dot(zf_ref[rows, sl], cs_table)
            zc_s[rows, sl] = cs[:, :FOURIER_GROUP].astype(BF16)
            zs_s[rows, sl] = cs[:, FOURIER_GROUP:].astype(BF16)

    def dft4(x):
        s0 = (x[0][0] + x[2][0], x[0][1] + x[2][1])
        s1 = (x[0][0] - x[2][0], x[0][1] - x[2][1])
        s2 = (x[1][0] + x[3][0], x[1][1] + x[3][1])
        s3 = (x[1][0] - x[3][0], x[1][1] - x[3][1])
        return [(s0[0] + s2[0], s0[1] + s2[1]), (s1[0] + s3[1], s1[1] - s3[0]),
                (s0[0] - s2[0], s0[1] - s2[1]), (s1[0] - s3[1], s1[1] + s3[0])]

    def butterfly(i, carry):
        r0 = pl.multiple_of(i * rb, rb)
        w = [(zc_s[pl.ds(j * ns + r0, rb), :].astype(F32), -zs_s[pl.ds(j * ns + r0, rb), :].astype(F32))
             for j in range(RADIX)]
        u = [(w[j][0] + w[j + 4][0], w[j][1] + w[j + 4][1]) for j in range(4)]
        v = [(w[j][0] - w[j + 4][0], w[j][1] - w[j + 4][1]) for j in range(4)]
        v = [v[0],
             ((v[1][0] + v[1][1]) * r2, (v[1][1] - v[1][0]) * r2),
             (v[2][1], -v[2][0]),
             ((v[3][1] - v[3][0]) * r2, (-v[3][0] - v[3][1]) * r2)]
        even = dft4(u)
        odd = dft4(v)
        for m in range(4):
            for k1, a in ((2 * m, even[m]), (2 * m + 1, odd[m])):
                ar_s[k1, pl.ds(r0, rb), :] = a[0].astype(BF16)
                ai_s[k1, pl.ds(r0, rb), :] = a[1].astype(BF16)
        return carry

    lax.fori_loop(0, ns // rb, butterfly, 0)

    for c0 in range(0, f, DFT_COLS):
        for k1 in range(RADIX):
            res = (_dot(c_ref[k1], ar_s[k1, :, c0:c0 + DFT_COLS])
                   + _dot(s_ref[k1], ai_s[k1, :, c0:c0 + DFT_COLS]))
            for j in range(DFT_COLS // LANES):
                o_s[j, pl.ds(k1, ns, stride=RADIX), :] = res[:, j * LANES:(j + 1) * LANES]
        for j in range(DFT_COLS // LANES):
            o_ref[:, c0 + j * LANES:c0 + (j + 1) * LANES] = o_s[j].astype(BF16)


def _channel_dft_table():
    p = np.arange(FOURIER_GROUP)
    ang = 2.0 * np.pi * ((p[:, None] * p[None, :]) % FOURIER_GROUP) / FOURIER_GROUP
    cs = np.concatenate([np.cos(ang), np.sin(ang)], axis=1) / math.sqrt(FOURIER_GROUP)
    return jnp.asarray(cs, F32)


def _dft_apply(ct, st, zf):
    b, t, f = zf.shape
    ns = t // RADIX
    table = pl.BlockSpec((RADIX, ns, ns), lambda bb: (0, 0, 0), pipeline_mode=pl.Buffered(1))
    cs = _channel_dft_table()
    return pl.pallas_call(
        _dftmm_kernel,
        grid=(b,),
        in_specs=[pl.BlockSpec((None, t, f), lambda bb: (bb, 0, 0)),
                  pl.BlockSpec(cs.shape, lambda bb: (0, 0)),
                  table, table],
        out_specs=pl.BlockSpec((None, t, f), lambda bb: (bb, 0, 0)),
        out_shape=jax.ShapeDtypeStruct((b, t, f), BF16),
        scratch_shapes=[pltpu.VMEM((t, f), BF16),
                        pltpu.VMEM((t, f), BF16),
                        pltpu.VMEM((RADIX, ns, f), BF16),
                        pltpu.VMEM((RADIX, ns, f), BF16),
                        pltpu.VMEM((DFT_COLS // LANES, t, LANES), F32)],
        compiler_params=_params("parallel"),
        name="dft_apply",
    )(zf, cs, ct, st)


def _merge_kernel(f_ref, gr_ref, za_ref, zb_ref, x_ref, pos_ref, mod_ref, g2_ref,
                  wf_ref, wl_ref, wo_ref, wr_ref, x1_ref, h2_ref, sc_ref, scr_ref):
    y_four = _dot(f_ref[...], wf_ref[...])
    y_rnn = _dot(gr_ref[...], wl_ref[...])
    m = _sigmoid(za_ref[...].astype(F32)) * y_four + _sigmoid(zb_ref[...].astype(F32)) * y_rnn
    y = _dot(m.astype(BF16), wo_ref[...])
    x1 = (x_ref[...] + pos_ref[...]) + mod_ref[2:3, :] * y
    x1_ref[...] = x1
    h2 = (_rmsnorm(x1, g2_ref[...]) * (1.0 + mod_ref[4:5, :]) + mod_ref[3:4, :]).astype(BF16)
    h2_ref[...] = h2
    logits = lax.dot_general(wr_ref[...], h2, (((1,), (1,)), ((), ())), preferred_element_type=F32)
    e = jnp.exp(logits - jnp.max(logits, axis=0, keepdims=True))
    sc = e / jnp.sum(e, axis=0, keepdims=True)
    sc_ref[...] = sc
    hi = sc.astype(BF16).astype(F32)
    lo = sc - hi
    tm = sc.shape[1]
    row = lax.broadcasted_iota(jnp.int32, (BF16_ROWS, tm), 0)
    for ex in range(N_EXPERTS):
        tile = jnp.where(row == 0, hi[ex:ex + 1, :], jnp.where(row == 1, lo[ex:ex + 1, :], 0.0))
        scr_ref[ex] = tile.astype(BF16)


def _merge(f, gr, za, zb, x, pos, mods, g2, w_four, w_lru, w_out, w_rt):
    b, t, d = x.shape
    tm = min(TOKEN_BLOCK, t)
    tok = lambda wd: pl.BlockSpec((None, tm, wd), lambda i, bb: (bb, i, 0))
    full = lambda a: pl.BlockSpec(a.shape, lambda i, bb: (0,) * a.ndim)
    return pl.pallas_call(
        _merge_kernel,
        grid=(t // tm, b),
        in_specs=[tok(D_FOURIER), tok(d), tok(d), tok(d), tok(d),
                  pl.BlockSpec((tm, d), lambda i, bb: (i, 0)),
                  pl.BlockSpec((None, SUBLANES, d), lambda i, bb: (bb, 0, 0)),
                  full(g2), full(w_four), full(w_lru), full(w_out), full(w_rt)],
        out_specs=[tok(d), tok(d), pl.BlockSpec((None, N_EXPERTS, tm), lambda i, bb: (bb, 0, i)),
                   pl.BlockSpec((None, N_EXPERTS, BF16_ROWS, tm), lambda i, bb: (bb, 0, 0, i))],
        out_shape=[jax.ShapeDtypeStruct((b, t, d), F32),
                   jax.ShapeDtypeStruct((b, t, d), BF16),
                   jax.ShapeDtypeStruct((b, N_EXPERTS, t), F32),
                   jax.ShapeDtypeStruct((b, N_EXPERTS, BF16_ROWS, t), BF16)],
        compiler_params=_params("parallel", "parallel"),
        name="merge",
    )(f, gr, za, zb, x, pos, mods, g2, w_four, w_lru, w_out, w_rt)


MAX_EXP = 127
EXP_STEPS = 7
VAL_STEPS = 56


OFF_CHUNK = 256


def _select_kernel(s_ref, slot_ref, offs_ref, *, cap):
    rows, t = s_ref.shape

    def count(mask):
        return jnp.sum(jnp.where(mask, 1.0, 0.0), axis=1, keepdims=True)

    def exp_step(_, carry):
        k_lo, k_hi = carry
        k_mid = jnp.floor(0.5 * (k_lo + k_hi))
        ok = count(s_ref[...] >= jnp.exp2(-k_mid)) >= cap
        return jnp.where(ok, k_lo, k_mid), jnp.where(ok, k_mid, k_hi)

    k_lo, k_hi = lax.fori_loop(
        0, EXP_STEPS, exp_step,
        (jnp.full((rows, 1), -1.0, F32), jnp.full((rows, 1), float(MAX_EXP), F32)))
    lo0 = jnp.where(k_hi >= MAX_EXP, 0.0, jnp.exp2(-k_hi))
    hi0 = jnp.exp2(-k_lo)

    def val_step(_, carry):
        lo, hi = carry
        mid = 0.5 * (lo + hi)
        ok = count(s_ref[...] >= mid) >= cap
        return jnp.where(ok, mid, lo), jnp.where(ok, hi, mid)

    lo, hi = lax.fori_loop(0, VAL_STEPS, val_step, (lo0, hi0))
    s = s_ref[...]
    gt = s >= hi
    eq = (s >= lo) & (s < hi)
    need = cap - count(gt)

    p = lax.broadcasted_iota(jnp.int32, (LANES, LANES), 0)
    q = lax.broadcasted_iota(jnp.int32, (LANES, LANES), 1)
    tri = (p < q).astype(BF16)
    run_gt = jnp.zeros((rows, 1), F32)
    run_eq = jnp.zeros((rows, 1), F32)
    off_lane = lax.broadcasted_iota(jnp.int32, (rows, LANES), 1)
    offs = jnp.zeros((rows, LANES), F32)
    for j in range(t // LANES):
        if (j * LANES) % OFF_CHUNK == 0:
            offs = jnp.where(off_lane == (j * LANES) // OFF_CHUNK, run_gt + jnp.minimum(run_eq, need), offs)
        sl = slice(j * LANES, (j + 1) * LANES)
        g = gt[:, sl]
        q_ = eq[:, sl]
        pg = _dot(g.astype(BF16), tri) + run_gt
        pe = _dot(q_.astype(BF16), tri) + run_eq
        sel = g | (q_ & (pe < need))
        pos = pg + jnp.minimum(pe, need)
        slot_ref[:, sl] = jnp.where(sel, pos, -1.0).astype(jnp.int32)
        run_gt = run_gt + count(g)
        run_eq = run_eq + count(q_)
    offs = jnp.where(off_lane == t // OFF_CHUNK, run_gt + jnp.minimum(run_eq, need), offs)
    offs_ref[...] = offs.astype(jnp.int32)


def _select(scores, cap):
    rows, t = scores.shape
    assert t % OFF_CHUNK == 0 and t // OFF_CHUNK < LANES
    return pl.pallas_call(
        functools.partial(_select_kernel, cap=cap),
        out_shape=[jax.ShapeDtypeStruct((rows, t), jnp.int32),
                   jax.ShapeDtypeStruct((rows, LANES), jnp.int32)],
        compiler_params=_params(),
        name="select",
    )(scores)


GATHER_BLOCK = 512
GATHER_WINDOW = 128


def _expert_kernel(offs_ref, slot_ref, sc_ref, h_ref, wg_ref, wu_ref, wd_ref, o_ref,
                   xg_s, gs_s, rows_s, gates_s, *, cap, n_off):
    t = h_ref.shape[0]
    tk = min(GATHER_BLOCK, t)
    win = min(GATHER_WINDOW, cap)
    per_block = tk // OFF_CHUNK
    slot_iota = lax.broadcasted_iota(jnp.int32, (win, tk), 0)
    xg_s[...] = jnp.zeros_like(xg_s)
    gs_s[...] = jnp.zeros_like(gs_s)
    row0 = (pl.program_id(0) * N_EXPERTS + pl.program_id(1)) * n_off

    def gather(sl, lo):
        onehot = (slot_iota + lo == slot_ref[:, sl]).astype(BF16)
        rows = _dot(onehot, h_ref[sl, :])
        gates = lax.dot_general(onehot, sc_ref[:, sl], (((1,), (1,)), ((), ())), preferred_element_type=F32)
        return rows, gates

    n_blocks = t // tk
    starts = [pl.multiple_of((offs_ref[row0 + j * per_block] // SUBLANES) * SUBLANES, SUBLANES)
              for j in range(n_blocks)]
    for j in range(n_blocks):
        rows_s[j], gates_s[j] = gather(slice(j * tk, (j + 1) * tk), starts[j])
    for j in range(n_blocks):
        sl = slice(j * tk, (j + 1) * tk)
        st = starts[j]
        last = offs_ref[row0 + (j + 1) * per_block]
        xg_s[pl.ds(st, SUBLANES), :] += rows_s[j, 0:SUBLANES, :]
        xg_s[pl.ds(st + SUBLANES, win - SUBLANES), :] = rows_s[j, SUBLANES:win, :]
        gs_s[pl.ds(st, win), :] += gates_s[j]

        def more(c, carry, sl=sl, st=st):
            lo = pl.multiple_of(st + c * win, SUBLANES)
            rows, gates = gather(sl, lo)
            xg_s[pl.ds(lo, win), :] = rows
            gs_s[pl.ds(lo, win), :] += gates
            return carry

        lax.fori_loop(1, (last - st + win - 1) // win, more, 0)
    xb = xg_s[0:cap, :].astype(BF16)
    gate = _dot(xb, wg_ref[...])
    hid = (gate * _sigmoid(gate)) * _dot(xb, wu_ref[...])
    y = _dot(hid.astype(BF16), wd_ref[...])
    o_ref[...] = (y * (gs_s[0:cap, 0:1] + gs_s[0:cap, 1:2])).astype(BF16)


def _experts(offs, slot, score_rows, h2, wg, wu, wd, cap):
    b, t, d = h2.shape
    n_off = offs.shape[0] // (b * N_EXPERTS)
    win = min(GATHER_WINDOW, cap)
    grid_spec = pltpu.PrefetchScalarGridSpec(
        num_scalar_prefetch=1,
        grid=(b, N_EXPERTS),
        in_specs=[pl.BlockSpec((None, 1, t), lambda bb, e, o: (bb * N_EXPERTS + e, 0, 0)),
                  pl.BlockSpec((None, None, BF16_ROWS, t), lambda bb, e, o: (bb, e, 0, 0)),
                  pl.BlockSpec((None, t, d), lambda bb, e, o: (bb, 0, 0)),
                  pl.BlockSpec((None, d, D_EXPERT), lambda bb, e, o: (e, 0, 0)),
                  pl.BlockSpec((None, d, D_EXPERT), lambda bb, e, o: (e, 0, 0)),
                  pl.BlockSpec((None, D_EXPERT, d), lambda bb, e, o: (e, 0, 0))],
        out_specs=pl.BlockSpec((None, cap, d), lambda bb, e, o: (bb * N_EXPERTS + e, 0, 0)),
        scratch_shapes=[pltpu.VMEM((cap + win, d), F32), pltpu.VMEM((cap + win, BF16_ROWS), F32),
                        pltpu.VMEM((t // min(GATHER_BLOCK, t), win, d), F32),
                        pltpu.VMEM((t // min(GATHER_BLOCK, t), win, BF16_ROWS), F32)],
    )
    return pl.pallas_call(
        functools.partial(_expert_kernel, cap=cap, n_off=n_off),
        grid_spec=grid_spec,
        out_shape=jax.ShapeDtypeStruct((b * N_EXPERTS, cap, d), BF16),
        compiler_params=_params("parallel", "parallel"),
        name="experts",
    )(offs, slot.reshape(b * N_EXPERTS, 1, t), score_rows, h2, wg, wu, wd)


COMBINE_BLOCK = 1024
SLOT_WINDOW = 256


COMBINE_GROUP = 8


def _combine_kernel(offs_ref, slot_ref, y_ref, x1_ref, mod_ref, g_ref, o_ref, acc_s, oh_s, yw_s, *, cap, n_off):
    tm = x1_ref.shape[0]
    kc = min(SLOT_WINDOW, cap)
    per_block = tm // OFF_CHUNK
    tg = tm // ROW_GROUPS
    group_k = COMBINE_GROUP * kc

    def place(e, start0, c, rows):
        lo = start0 + c * kc
        st = pl.multiple_of(jnp.minimum(lo, cap - kc), BF16_ROWS)
        col = slot_ref[rows, e:e + 1]
        onehot = col == st + lax.broadcasted_iota(jnp.int32, (col.shape[0], kc), 1)
        if not isinstance(c, int) or c > 0:
            onehot = onehot & (col >= lo)
        return onehot.astype(BF16), st

    starts, extras = [], []
    for e in range(N_EXPERTS):
        o0 = (pl.program_id(0) * N_EXPERTS + e) * n_off + pl.program_id(1) * per_block
        start0 = (offs_ref[o0] // BF16_ROWS) * BF16_ROWS
        starts.append(start0)
        extras.append(jnp.maximum((offs_ref[o0 + per_block] - start0 + kc - 1) // kc - 1, 0))

    for e in range(N_EXPERTS):
        st = pl.multiple_of(jnp.minimum(starts[e], cap - kc), BF16_ROWS)
        yw_s[e * kc:(e + 1) * kc, :] = y_ref[pl.ds(e * cap + st, kc), :]

    def finish(rows, total):
        o_ref[rows, :] = _rmsnorm(x1_ref[rows, :] + mod_ref[5:6, :] * total, g_ref[...])

    accs = []
    for grp in range(ROW_GROUPS):
        rows = slice(grp * tg, (grp + 1) * tg)
        acc = None
        for gi in range(N_EXPERTS // COMBINE_GROUP):
            for i in range(COMBINE_GROUP):
                onehot, _ = place(gi * COMBINE_GROUP + i, starts[gi * COMBINE_GROUP + i], 0, rows)
                oh_s[gi, :, i * kc:(i + 1) * kc] = onehot
            part = _dot(oh_s[gi], yw_s[gi * group_k:(gi + 1) * group_k, :])
            acc = part if acc is None else acc + part
        finish(rows, acc)
        accs.append(acc)

    @pl.when(sum(extras) > 0)
    def _():
        every = slice(0, tm)
        for grp in range(ROW_GROUPS):
            acc_s[grp * tg:(grp + 1) * tg, :] = accs[grp]
        for e in range(N_EXPERTS):
            def more(c, carry, e=e):
                onehot, st = place(e, starts[e], c, every)
                acc_s[...] += _dot(onehot, y_ref[pl.ds(e * cap + st, kc), :])
                return carry
            lax.fori_loop(1, extras[e] + 1, more, 0)
        finish(every, acc_s[...])


def _combine(offs, slot_t, y, x1, mods, g, cap):
    b, t, d = x1.shape
    tm = min(COMBINE_BLOCK, t)
    n_off = offs.shape[0] // (b * N_EXPERTS)
    grid_spec = pltpu.PrefetchScalarGridSpec(
        num_scalar_prefetch=1,
        grid=(b, t // tm),
        in_specs=[pl.BlockSpec((None, tm, N_EXPERTS), lambda bb, i, o: (bb, i, 0)),
                  pl.BlockSpec((None, N_EXPERTS * cap, d), lambda bb, i, o: (bb, 0, 0),
                               pipeline_mode=pl.Buffered(1)),
                  pl.BlockSpec((None, tm, d), lambda bb, i, o: (bb, i, 0)),
                  pl.BlockSpec((None, SUBLANES, d), lambda bb, i, o: (bb, 0, 0)),
                  pl.BlockSpec((1, d), lambda bb, i, o: (0, 0))],
        out_specs=pl.BlockSpec((None, tm, d), lambda bb, i, o: (bb, i, 0)),
        scratch_shapes=[pltpu.VMEM((tm, d), F32),
                        pltpu.VMEM((N_EXPERTS // COMBINE_GROUP, tm // ROW_GROUPS,
                                    COMBINE_GROUP * min(SLOT_WINDOW, cap)), BF16),
                        pltpu.VMEM((N_EXPERTS * min(SLOT_WINDOW, cap), d), BF16)],
    )
    return pl.pallas_call(
        functools.partial(_combine_kernel, cap=cap, n_off=n_off),
        grid_spec=grid_spec,
        out_shape=jax.ShapeDtypeStruct((b, t, d), F32),
        compiler_params=_params("parallel", "parallel"),
        name="combine",
    )(offs, slot_t, y, x1, mods, g)


def kernel(x, c, ctx, c_ctx, w_ada, b_ada, norm1_g, norm2_g, w_in, w_four, conv_w, conv_b, lru_lambda, lru_wa, lru_ba, lru_wi, lru_bi, w_lru, w_out, w_router, w_gate_e, w_up_e, w_down_e, final_g):
    b, t, d = x.shape
    assert d == D_MODEL and b == SUBLANES and t % TOKEN_BLOCK == 0 and ctx.shape[1] % SCAN_CHUNK == 0
    assert w_ada.shape[0] == 1, "single-layer problem: the context stream is only read by the recurrence"
    cap = CAPACITY_FACTOR * t // N_EXPERTS
    col_x = D_FOURIER + D_RNN
    col_g = col_x + D_RNN
    col_a = col_g + D_MODEL

    pos = _pos_table(t)
    ct, st = _dft_tables(t)

    cc = jnp.concatenate([c, c_ctx[None], jnp.zeros((SUBLANES - 1, d), F32)], axis=0)
    mods = _ada(cc, w_ada[0], b_ada[0]).reshape(cc.shape[0], N_MOD, d)
    mods = jnp.pad(mods, ((0, 0), (0, SUBLANES - N_MOD), (0, 0)))

    w = w_in[0].astype(BF16)
    w_x = w[:, D_FOURIER:col_x]
    w_cat = jnp.concatenate([w[:, D_FOURIER:], w[:, :D_FOURIER]], axis=1)
    g1 = norm1_g[0].reshape(1, d)
    zx, zg, za, zb, zf, wg_b, wu_b, wd_b = _proj(
        x, pos, mods, None, g1, w_cat, (D_RNN, D_RNN, D_MODEL, D_MODEL, D_FOURIER),
        casts=(w_gate_e[0].reshape(N_EXPERTS * d, D_EXPERT), w_up_e[0].reshape(N_EXPERTS * d, D_EXPERT),
               w_down_e[0].reshape(N_EXPERTS * D_EXPERT, d)))
    (zx_ctx,) = _proj(ctx, None, mods, b, g1, w_x, (D_RNN,))

    for dr in range(2):
        vecs = jnp.concatenate([lru_lambda[0, dr][None], 0.5 * lru_ba[0, dr][None], 0.5 * lru_bi[0, dr][None],
                                conv_b[0][None], conv_w[0]], axis=0)
        wg = (0.5 * jnp.concatenate([lru_wa[0, dr], lru_wi[0, dr]], axis=-1)).astype(BF16)
        if dr == 0:
            u_lat = _conv(zx, vecs)
            u_ctx = _conv(zx_ctx, vecs)
        _, h_ctx = _scan(u_ctx, vecs, wg, jnp.zeros((b, D_RNN), F32), reverse=bool(dr))
        if dr == 0:
            h_fwd, _ = _scan(u_lat, vecs, wg, h_ctx, reverse=False)
        else:
            gr, _ = _scan(u_lat, vecs, wg, h_ctx, reverse=True, hf=h_fwd, zg=zg)

    f = _dft_apply(ct, st, zf)

    x1, h2, scores, score_rows = _merge(
        f, gr, za, zb, x, pos, mods, norm2_g[0].reshape(1, d),
        w_four[0].astype(BF16), w_lru[0].astype(BF16), w_out[0].astype(BF16), w_router[0].T.astype(BF16))

    slot, offs = _select(scores.reshape(b * N_EXPERTS, t), cap)
    offs = offs[:, :t // OFF_CHUNK + 1].reshape(-1)
    y = _experts(offs, slot, score_rows, h2, wg_b.reshape(N_EXPERTS, d, D_EXPERT),
                 wu_b.reshape(N_EXPERTS, d, D_EXPERT), wd_b.reshape(N_EXPERTS, D_EXPERT, d), cap)
    slot_t = jnp.swapaxes(slot.reshape(b, N_EXPERTS, t), 1, 2)
    return _combine(offs, slot_t, y.reshape(b, N_EXPERTS * cap, d), x1, mods, final_g.reshape(1, d), cap)
```

```python
import functools
import math

import numpy as np
import jax
import jax.numpy as jnp
from jax import lax
from jax.experimental import pallas as pl
from jax.experimental.pallas import tpu as pltpu

F32 = jnp.float32
BF16 = jnp.bfloat16

D_MODEL = 1024
GRID_W = 64
N_FOURIER_GROUPS = 4
FOURIER_GROUP = 128
D_FOURIER = N_FOURIER_GROUPS * FOURIER_GROUP
D_RNN = D_MODEL
N_RNN_HEADS = 8
RNN_HEAD = D_RNN // N_RNN_HEADS
CONV_W = 4
CONV_LEFT = 2
LRU_C = 8.0
LOG2E = math.log2(math.e)
N_EXPERTS = 16
CAPACITY_FACTOR = 2
D_EXPERT = 1536
N_MOD = 6
EPS = 1e-6
POS_MAX_PERIOD = 10000.0

LANES = 128
SUBLANES = 8
BF16_ROWS = 16
VMEM_LIMIT = 58 * 1024 * 1024

TOKEN_BLOCK = 512
ROW_GROUPS = 2
SCAN_CHUNK = 128


def _params(*sem):
    return pltpu.CompilerParams(dimension_semantics=sem, vmem_limit_bytes=VMEM_LIMIT)


def _sigmoid(x):
    return 0.5 * jnp.tanh(0.5 * x) + 0.5


def _rmsnorm(x, g):
    return x * lax.rsqrt(jnp.mean(x * x, axis=-1, keepdims=True) + EPS) * g


def _dot(a, b):
    return jnp.dot(a, b, preferred_element_type=F32)


def _pos_kernel(o_ref):
    rows = o_ref.shape[0]
    q = D_MODEL // 4
    k = lax.broadcasted_iota(jnp.int32, (1, q), 1).astype(F32)
    freqs = jnp.exp(-math.log(POS_MAX_PERIOD) * k / q)
    ang_r = lax.broadcasted_iota(jnp.int32, (rows, q), 0).astype(F32) * freqs
    ang_c = lax.broadcasted_iota(jnp.int32, (GRID_W, q), 0).astype(F32) * freqs
    shape = (rows, GRID_W, q)
    o_ref[:, :, 0 * q:1 * q] = jnp.broadcast_to(jnp.sin(ang_r)[:, None, :], shape)
    o_ref[:, :, 1 * q:2 * q] = jnp.broadcast_to(jnp.cos(ang_r)[:, None, :], shape)
    o_ref[:, :, 2 * q:3 * q] = jnp.broadcast_to(jnp.sin(ang_c)[None, :, :], shape)
    o_ref[:, :, 3 * q:4 * q] = jnp.broadcast_to(jnp.cos(ang_c)[None, :, :], shape)


def _pos_table(n_tokens):
    rows = n_tokens // GRID_W
    out = pl.pallas_call(
        _pos_kernel,
        out_shape=jax.ShapeDtypeStruct((rows, GRID_W, D_MODEL), F32),
        name="pos_table",
    )()
    return out.reshape(n_tokens, D_MODEL)


RADIX = 8


def _dft_kernel(c_ref, s_ref, *, n):
    ns = c_ref.shape[1]
    k2 = lax.broadcasted_iota(jnp.int32, (ns, ns), 0)
    t2 = lax.broadcasted_iota(jnp.int32, (ns, ns), 1)
    common = ((k2 * t2) & (ns - 1)).astype(F32) * (2.0 * math.pi / ns)
    cos_c = jnp.cos(common)
    sin_c = jnp.sin(common)
    scale = 1.0 / math.sqrt(n)
    t_row = lax.broadcasted_iota(jnp.int32, (1, ns), 1)
    for k1 in range(RADIX):
        slab = (k1 * t_row).astype(F32) * (2.0 * math.pi / n)
        cos_s = jnp.cos(slab) * scale
        sin_s = jnp.sin(slab) * scale
        c_ref[k1] = (cos_c * cos_s - sin_c * sin_s).astype(BF16)
        s_ref[k1] = (sin_c * cos_s + cos_c * sin_s).astype(BF16)


def _dft_tables(n):
    assert n & (n - 1) == 0 and n % (RADIX * BF16_ROWS) == 0
    ns = n // RADIX
    return pl.pallas_call(
        functools.partial(_dft_kernel, n=n),
        out_shape=[jax.ShapeDtypeStruct((RADIX, ns, ns), BF16)] * 2,
        compiler_params=_params(),
        name="dft_tables",
    )()


def _ada_kernel(c_ref, w_ref, b_ref, o_ref):
    c = c_ref[...]
    o_ref[...] = jnp.dot(c * _sigmoid(c), w_ref[...], preferred_element_type=F32,
                         precision=lax.Precision.HIGHEST) + b_ref[...]


def _ada(cc, w_ada, b_ada):
    n = w_ada.shape[1]
    tn = D_MODEL
    return pl.pallas_call(
        _ada_kernel,
        grid=(n // tn,),
        in_specs=[pl.BlockSpec(cc.shape, lambda j: (0, 0)),
                  pl.BlockSpec((D_MODEL, tn), lambda j: (0, j)),
                  pl.BlockSpec((1, tn), lambda j: (0, j))],
        out_specs=pl.BlockSpec((cc.shape[0], tn), lambda j: (0, j)),
        out_shape=jax.ShapeDtypeStruct((cc.shape[0], n), F32),
        compiler_params=_params("parallel"),
        name="ada",
    )(cc, w_ada, b_ada.reshape(1, n))


def _proj_kernel(*refs, cols, has_pos, n_cast):
    refs = list(refs)
    if n_cast:
        outs_cast = refs[-n_cast:]
        del refs[-n_cast:]
        n_in = len(refs) - len(cols)
        for w32_ref, w16_ref in zip(refs[n_in - n_cast:n_in], outs_cast):
            w16_ref[...] = w32_ref[...].astype(BF16)
        del refs[n_in - n_cast:n_in]
    if has_pos:
        x_ref, pos_ref, mod_ref, g_ref, w_ref = refs[:5]
        outs = refs[5:]
    else:
        x_ref, mod_ref, g_ref, w_ref = refs[:4]
        outs = refs[4:]
    tg = x_ref.shape[0] // ROW_GROUPS
    for grp in range(ROW_GROUPS):
        rs = slice(grp * tg, (grp + 1) * tg)
        x = x_ref[rs, :] + pos_ref[rs, :] if has_pos else x_ref[rs, :]
        h = _rmsnorm(x, g_ref[...]) * (1.0 + mod_ref[1:2, :]) + mod_ref[0:1, :]
        hb = h.astype(BF16)
        for o_ref, (c0, wd) in zip(outs, cols):
            o_ref[rs, :] = _dot(hb, w_ref[:, c0:c0 + wd]).astype(BF16)


def _proj(x, pos, mods, mod_row, g, w, cols, casts=()):
    b, t, d = x.shape
    tm = min(TOKEN_BLOCK, t)
    has_pos = pos is not None
    in_specs = [pl.BlockSpec((None, tm, d), lambda i, bb: (bb, i, 0))]
    args = [x]
    if has_pos:
        in_specs.append(pl.BlockSpec((tm, d), lambda i, bb: (i, 0)))
        args.append(pos)
    if mod_row is None:
        mod_map = lambda i, bb: (bb, 0, 0)
    else:
        mod_map = lambda i, bb: (mod_row, 0, 0)
    in_specs += [pl.BlockSpec((None, SUBLANES, d), mod_map),
                 pl.BlockSpec((1, d), lambda i, bb: (0, 0)),
                 pl.BlockSpec(w.shape, lambda i, bb: (0, 0), pipeline_mode=pl.Buffered(1))]
    args += [mods, g, w]
    out_specs = [pl.BlockSpec((None, tm, wd), lambda i, bb: (bb, i, 0)) for _, wd in cols]
    out_shape = [jax.ShapeDtypeStruct((b, t, wd), BF16) for _, wd in cols]
    for w32 in casts:
        spec, shape = _cast_specs(w32, (t // tm) * b, lambda i, bb: i * b + bb)
        in_specs.append(spec)
        args.append(w32)
        out_specs.append(spec)
        out_shape.append(shape)
    return pl.pallas_call(
        functools.partial(_proj_kernel, cols=cols, has_pos=has_pos, n_cast=len(casts)),
        grid=(t // tm, b),
        in_specs=in_specs,
        out_specs=out_specs,
        out_shape=out_shape,
        compiler_params=_params("parallel", "parallel"),
        name="proj",
    )(*args)


def _gelu(x):
    return 0.5 * x * (1.0 + jnp.tanh(math.sqrt(2.0 / math.pi) * (x + 0.044715 * x * x * x)))


def _conv_kernel(x_ref, xp_ref, xn_ref, vec_ref, o_ref):
    tm = x_ref.shape[0]
    edge = BF16_ROWS
    i = pl.program_id(0)
    first = i == 0
    last = i == pl.num_programs(0) - 1
    w = [vec_ref[4 + kk:5 + kk, :] for kk in range(CONV_W)]
    bias = vec_ref[3:4, :]

    def taps(xm2, xm1, x0, xp1):
        return bias + w[0] * xm2 + w[1] * xm1 + w[2] * x0 + w[3] * xp1

    x = x_ref[...].astype(F32)
    o_ref[...] = taps(pltpu.roll(x, 2, 0), pltpu.roll(x, 1, 0), x, pltpu.roll(x, tm - 1, 0)).astype(BF16)

    row = lax.broadcasted_iota(jnp.int32, (edge, x.shape[1]), 0)

    def shifted(before, cur, after, s):
        if s < 0:
            return jnp.where(row < -s, pltpu.roll(before, -s, 0), pltpu.roll(cur, -s, 0))
        return jnp.where(row < edge - s, pltpu.roll(cur, edge - s, 0), pltpu.roll(after, edge - s, 0))

    prev = jnp.where(first, 0.0, xp_ref[...].astype(F32))
    nxt = jnp.where(last, 0.0, xn_ref[...].astype(F32))
    head, head2 = x[0:edge], x[edge:2 * edge]
    tail, tail2 = x[tm - edge:tm], x[tm - 2 * edge:tm - edge]
    o_ref[0:edge, :] = taps(shifted(prev, head, head2, -2), shifted(prev, head, head2, -1), head,
                            shifted(prev, head, head2, 1)).astype(BF16)
    o_ref[tm - edge:tm, :] = taps(shifted(tail2, tail, nxt, -2), shifted(tail2, tail, nxt, -1), tail,
                                  shifted(tail2, tail, nxt, 1)).astype(BF16)


def _conv(zx, vecs):
    b, t, c = zx.shape
    tm = min(TOKEN_BLOCK, t)
    assert tm >= 2 * BF16_ROWS
    hpb = tm // BF16_ROWS
    n_halo = t // BF16_ROWS
    return pl.pallas_call(
        _conv_kernel,
        grid=(t // tm, b),
        in_specs=[pl.BlockSpec((None, tm, c), lambda i, bb: (bb, i, 0)),
                  pl.BlockSpec((None, BF16_ROWS, c), lambda i, bb: (bb, jnp.maximum(i * hpb - 1, 0), 0)),
                  pl.BlockSpec((None, BF16_ROWS, c), lambda i, bb: (bb, jnp.minimum((i + 1) * hpb, n_halo - 1), 0)),
                  pl.BlockSpec(vecs.shape, lambda i, bb: (0, 0))],
        out_specs=pl.BlockSpec((None, tm, c), lambda i, bb: (bb, i, 0)),
        out_shape=jax.ShapeDtypeStruct((b, t, c), BF16),
        compiler_params=_params("parallel", "parallel"),
        name="conv",
    )(zx, zx, zx, vecs)


def _scan_kernel(*refs, reverse, fuse):
    if fuse:
        u_ref, vec_ref, wg_ref, h0_ref, hf_ref, zg_ref, o_ref, hT_ref, a_s, b_s, h_s, hc_s = refs
    else:
        u_ref, vec_ref, wg_ref, h0_ref, o_ref, hT_ref, a_s, b_s, h_s, hc_s = refs
    nb, tc, c = u_ref.shape

    @pl.when(pl.program_id(0) == 0)
    def _():
        for hh in range(N_RNN_HEADS):
            hc_s[hh] = h0_ref[:, hh * RNN_HEAD:(hh + 1) * RNN_HEAD]

    lam = vec_ref[0:1, :]
    half_b_a = vec_ref[1:2, :]
    half_b_i = vec_ref[2:3, :]
    k_row = (-0.5 * LRU_C * LOG2E) * (jnp.maximum(-lam, 0.0) + jnp.log1p(jnp.exp(-jnp.abs(lam))))

    for bb in range(nb):
        ub = u_ref[bb]
        u = ub.astype(F32)
        rows = pl.ds(bb, tc, stride=nb)
        for hh in range(N_RNN_HEADS):
            sl = slice(hh * RNN_HEAD, (hh + 1) * RNN_HEAD)
            gates = _dot(ub[:, sl], wg_ref[hh])
            t_r = jnp.tanh(gates[:, :RNN_HEAD] + half_b_a[:, sl])
            t_i = jnp.tanh(gates[:, RNN_HEAD:] + half_b_i[:, sl])
            a = jnp.exp2(k_row[:, sl] * t_r + k_row[:, sl])
            a_s[hh, rows, :] = a
            scale = jnp.exp2((0.5 * LOG2E) * jnp.log(1.0 - a * a)) * u[:, sl]
            b_s[hh, rows, :] = scale * (0.5 * t_i + 0.5)

    def body(j, hs):
        t = (tc - 1 - j) if reverse else j
        idx = pl.ds(pl.multiple_of(t * nb, nb), nb)
        new = []
        for hh in range(N_RNN_HEADS):
            h = a_s[hh, idx, :] * hs[hh] + b_s[hh, idx, :]
            h_s[hh, idx, :] = h
            new.append(h)
        return tuple(new)

    hs = lax.fori_loop(0, tc, body, tuple(hc_s[hh] for hh in range(N_RNN_HEADS)), unroll=8)
    for hh in range(N_RNN_HEADS):
        hc_s[hh] = hs[hh]
        hT_ref[:, hh * RNN_HEAD:(hh + 1) * RNN_HEAD] = hs[hh]

    for bb in range(nb):
        for hh in range(N_RNN_HEADS):
            sl = slice(hh * RNN_HEAD, (hh + 1) * RNN_HEAD)
            hb = h_s[hh, pl.ds(bb, tc, stride=nb), :]
            if fuse:
                hb = _gelu(zg_ref[bb, :, sl].astype(F32)) * (hf_ref[bb, :, sl].astype(F32) + hb)
            o_ref[bb, :, sl] = hb.astype(BF16)


def _cast_specs(w, n_steps, step_index):
    rows, cols = w.shape
    assert rows % (n_steps * BF16_ROWS) == 0
    spec = pl.BlockSpec((rows // n_steps, cols), lambda *idx: (step_index(*idx), 0))
    return spec, jax.ShapeDtypeStruct((rows, cols), BF16)


def _scan(u, vecs, wg, h0, *, reverse, hf=None, zg=None):
    nb, t, c = u.shape
    tc = SCAN_CHUNK
    n_chunks = t // tc
    fuse = hf is not None

    def cidx(i):
        return (n_chunks - 1 - i) if reverse else i

    main_spec = pl.BlockSpec((nb, tc, c), lambda i: (0, cidx(i), 0))
    in_specs = [
        main_spec,
        pl.BlockSpec(vecs.shape, lambda i: (0, 0)),
        pl.BlockSpec(wg.shape, lambda i: (0, 0, 0)),
        pl.BlockSpec(h0.shape, lambda i: (0, 0)),
    ]
    args = [u, vecs, wg, h0]
    if fuse:
        in_specs += [main_spec, main_spec]
        args += [hf, zg]
    return pl.pallas_call(
        functools.partial(_scan_kernel, reverse=reverse, fuse=fuse),
        grid=(n_chunks,),
        in_specs=in_specs,
        out_specs=[main_spec, pl.BlockSpec((nb, c), lambda i: (0, 0))],
        out_shape=[jax.ShapeDtypeStruct((nb, t, c), BF16), jax.ShapeDtypeStruct((nb, c), F32)],
        scratch_shapes=[pltpu.VMEM((N_RNN_HEADS, nb * tc, RNN_HEAD), F32),
                        pltpu.VMEM((N_RNN_HEADS, nb * tc, RNN_HEAD), F32),
                        pltpu.VMEM((N_RNN_HEADS, nb * tc, RNN_HEAD), F32),
                        pltpu.VMEM((N_RNN_HEADS, nb, RNN_HEAD), F32)],
        compiler_params=_params("arbitrary"),
        name="scan_bwd" if reverse else "scan_fwd",
    )(*args)


DFT_ROWS = 64
DFT_COLS = 256


def _dftmm_kernel(zf_ref, cs_ref, c_ref, s_ref, o_ref, zc_s, zs_s, ar_s, ai_s, o_s):
    t, f = zf_ref.shape
    ns = t // RADIX
    rb = min(DFT_ROWS, ns)
    r2 = math.sqrt(0.5)

    cs_table = cs_ref[...].astype(BF16)
    for j in range(RADIX):
        rows = slice(j * ns, (j + 1) * ns)
        for g in range(N_FOURIER_GROUPS):
            sl = slice(g * FOURIER_GROUP, (g + 1) * FOURIER_GROUP)
            cs = _dot(zf_ref[rows, sl], cs_table)
            zc_s[rows, sl] = cs[:, :FOURIER_GROUP].astype(BF16)
            zs_s[rows, sl] = cs[:, FOURIER_GROUP:].astype(BF16)

    def dft4(x):
        s0 = (x[0][0] + x[2][0], x[0][1] + x[2][1])
        s1 = (x[0][0] - x[2][0], x[0][1] - x[2][1])
        s2 = (x[1][0] + x[3][0], x[1][1] + x[3][1])
        s3 = (x[1][0] - x[3][0], x[1][1] - x[3][1])
        return [(s0[0] + s2[0], s0[1] + s2[1]), (s1[0] + s3[1], s1[1] - s3[0]),
                (s0[0] - s2[0], s0[1] - s2[1]), (s1[0] - s3[1], s1[1] + s3[0])]

    def butterfly(i, carry):
        r0 = pl.multiple_of(i * rb, rb)
        w = [(zc_s[pl.ds(j * ns + r0, rb), :].astype(F32), -zs_s[pl.ds(j * ns + r0, rb), :].astype(F32))
             for j in range(RADIX)]
        u = [(w[j][0] + w[j + 4][0], w[j][1] + w[j + 4][1]) for j in range(4)]
        v = [(w[j][0] - w[j + 4][0], w[j][1] - w[j + 4][1]) for j in range(4)]
        v = [v[0],
             ((v[1][0] + v[1][1]) * r2, (v[1][1] - v[1][0]) * r2),
             (v[2][1], -v[2][0]),
             ((v[3][1] - v[3][0]) * r2, (-v[3][0] - v[3][1]) * r2)]
        even = dft4(u)
        odd = dft4(v)
        for m in range(4):
            for k1, a in ((2 * m, even[m]), (2 * m + 1, odd[m])):
                ar_s[k1, pl.ds(r0, rb), :] = a[0].astype(BF16)
                ai_s[k1, pl.ds(r0, rb), :] = a[1].astype(BF16)
        return carry

    lax.fori_loop(0, ns // rb, butterfly, 0)

    for c0 in range(0, f, DFT_COLS):
        for k1 in range(RADIX):
            res = (_dot(c_ref[k1], ar_s[k1, :, c0:c0 + DFT_COLS])
                   + _dot(s_ref[k1], ai_s[k1, :, c0:c0 + DFT_COLS]))
            for j in range(DFT_COLS // LANES):
                o_s[j, pl.ds(k1, ns, stride=RADIX), :] = res[:, j * LANES:(j + 1) * LANES]
        for j in range(DFT_COLS // LANES):
            o_ref[:, c0 + j * LANES:c0 + (j + 1) * LANES] = o_s[j].astype(BF16)


def _channel_dft_table():
    p = np.arange(FOURIER_GROUP)
    ang = 2.0 * np.pi * ((p[:, None] * p[None, :]) % FOURIER_GROUP) / FOURIER_GROUP
    cs = np.concatenate([np.cos(ang), np.sin(ang)], axis=1) / math.sqrt(FOURIER_GROUP)
    return jnp.asarray(cs, F32)


def _dft_apply(ct, st, zf):
    b, t, f = zf.shape
    ns = t // RADIX
    table = pl.BlockSpec((RADIX, ns, ns), lambda bb: (0, 0, 0), pipeline_mode=pl.Buffered(1))
    cs = _channel_dft_table()
    return pl.pallas_call(
        _dftmm_kernel,
        grid=(b,),
        in_specs=[pl.BlockSpec((None, t, f), lambda bb: (bb, 0, 0)),
                  pl.BlockSpec(cs.shape, lambda bb: (0, 0)),
                  table, table],
        out_specs=pl.BlockSpec((None, t, f), lambda bb: (bb, 0, 0)),
        out_shape=jax.ShapeDtypeStruct((b, t, f), BF16),
        scratch_shapes=[pltpu.VMEM((t, f), BF16),
                        pltpu.VMEM((t, f), BF16),
                        pltpu.VMEM((RADIX, ns, f), BF16),
                        pltpu.VMEM((RADIX, ns, f), BF16),
                        pltpu.VMEM((DFT_COLS // LANES, t, LANES), F32)],
        compiler_params=_params("parallel"),
        name="dft_apply",
    )(zf, cs, ct, st)


def _merge_kernel(f_ref, gr_ref, za_ref, zb_ref, x_ref, pos_ref, mod_ref, g2_ref,
                  wf_ref, wl_ref, wo_ref, wr_ref, x1_ref, h2_ref, sc_ref, scr_ref):
    y_four = _dot(f_ref[...], wf_ref[...])
    y_rnn = _dot(gr_ref[...], wl_ref[...])
    m = _sigmoid(za_ref[...].astype(F32)) * y_four + _sigmoid(zb_ref[...].astype(F32)) * y_rnn
    y = _dot(m.astype(BF16), wo_ref[...])
    x1 = (x_ref[...] + pos_ref[...]) + mod_ref[2:3, :] * y
    x1_ref[...] = x1
    h2 = (_rmsnorm(x1, g2_ref[...]) * (1.0 + mod_ref[4:5, :]) + mod_ref[3:4, :]).astype(BF16)
    h2_ref[...] = h2
    logits = lax.dot_general(wr_ref[...], h2, (((1,), (1,)), ((), ())), preferred_element_type=F32)
    e = jnp.exp(logits - jnp.max(logits, axis=0, keepdims=True))
    sc = e / jnp.sum(e, axis=0, keepdims=True)
    sc_ref[...] = sc
    hi = sc.astype(BF16).astype(F32)
    lo = sc - hi
    tm = sc.shape[1]
    row = lax.broadcasted_iota(jnp.int32, (BF16_ROWS, tm), 0)
    for ex in range(N_EXPERTS):
        tile = jnp.where(row == 0, hi[ex:ex + 1, :], jnp.where(row == 1, lo[ex:ex + 1, :], 0.0))
        scr_ref[ex] = tile.astype(BF16)


def _merge(f, gr, za, zb, x, pos, mods, g2, w_four, w_lru, w_out, w_rt):
    b, t, d = x.shape
    tm = min(TOKEN_BLOCK, t)
    tok = lambda wd: pl.BlockSpec((None, tm, wd), lambda i, bb: (bb, i, 0))
    full = lambda a: pl.BlockSpec(a.shape, lambda i, bb: (0,) * a.ndim)
    return pl.pallas_call(
        _merge_kernel,
        grid=(t // tm, b),
        in_specs=[tok(D_FOURIER), tok(d), tok(d), tok(d), tok(d),
                  pl.BlockSpec((tm, d), lambda i, bb: (i, 0)),
                  pl.BlockSpec((None, SUBLANES, d), lambda i, bb: (bb, 0, 0)),
                  full(g2), full(w_four), full(w_lru), full(w_out), full(w_rt)],
        out_specs=[tok(d), tok(d), pl.BlockSpec((None, N_EXPERTS, tm), lambda i, bb: (bb, 0, i)),
                   pl.BlockSpec((None, N_EXPERTS, BF16_ROWS, tm), lambda i, bb: (bb, 0, 0, i))],
        out_shape=[jax.ShapeDtypeStruct((b, t, d), F32),
                   jax.ShapeDtypeStruct((b, t, d), BF16),
                   jax.ShapeDtypeStruct((b, N_EXPERTS, t), F32),
                   jax.ShapeDtypeStruct((b, N_EXPERTS, BF16_ROWS, t), BF16)],
        compiler_params=_params("parallel", "parallel"),
        name="merge",
    )(f, gr, za, zb, x, pos, mods, g2, w_four, w_lru, w_out, w_rt)


MAX_EXP = 127
EXP_STEPS = 7
VAL_STEPS = 56


OFF_CHUNK = 256


def _select_kernel(s_ref, slot_ref, offs_ref, *, cap):
    rows, t = s_ref.shape

    def count(mask):
        return jnp.sum(jnp.where(mask, 1.0, 0.0), axis=1, keepdims=True)

    def exp_step(_, carry):
        k_lo, k_hi = carry
        k_mid = jnp.floor(0.5 * (k_lo + k_hi))
        ok = count(s_ref[...] >= jnp.exp2(-k_mid)) >= cap
        return jnp.where(ok, k_lo, k_mid), jnp.where(ok, k_mid, k_hi)

    k_lo, k_hi = lax.fori_loop(
        0, EXP_STEPS, exp_step,
        (jnp.full((rows, 1), -1.0, F32), jnp.full((rows, 1), float(MAX_EXP), F32)))
    lo0 = jnp.where(k_hi >= MAX_EXP, 0.0, jnp.exp2(-k_hi))
    hi0 = jnp.exp2(-k_lo)

    def val_step(_, carry):
        lo, hi = carry
        mid = 0.5 * (lo + hi)
        ok = count(s_ref[...] >= mid) >= cap
        return jnp.where(ok, mid, lo), jnp.where(ok, hi, mid)

    lo, hi = lax.fori_loop(0, VAL_STEPS, val_step, (lo0, hi0))
    s = s_ref[...]
    gt = s >= hi
    eq = (s >= lo) & (s < hi)
    need = cap - count(gt)

    p = lax.broadcasted_iota(jnp.int32, (LANES, LANES), 0)
    q = lax.broadcasted_iota(jnp.int32, (LANES, LANES), 1)
    tri = (p < q).astype(BF16)
    run_gt = jnp.zeros((rows, 1), F32)
    run_eq = jnp.zeros((rows, 1), F32)
    off_lane = lax.broadcasted_iota(jnp.int32, (rows, LANES), 1)
    offs = jnp.zeros((rows, LANES), F32)
    for j in range(t // LANES):
        if (j * LANES) % OFF_CHUNK == 0:
            offs = jnp.where(off_lane == (j * LANES) // OFF_CHUNK, run_gt + jnp.minimum(run_eq, need), offs)
        sl = slice(j * LANES, (j + 1) * LANES)
        g = gt[:, sl]
        q_ = eq[:, sl]
        pg = _dot(g.astype(BF16), tri) + run_gt
        pe = _dot(q_.astype(BF16), tri) + run_eq
        sel = g | (q_ & (pe < need))
        pos = pg + jnp.minimum(pe, need)
        slot_ref[:, sl] = jnp.where(sel, pos, -1.0).astype(jnp.int32)
        run_gt = run_gt + count(g)
        run_eq = run_eq + count(q_)
    offs = jnp.where(off_lane == t // OFF_CHUNK, run_gt + jnp.minimum(run_eq, need), offs)
    offs_ref[...] = offs.astype(jnp.int32)


def _select(scores, cap):
    rows, t = scores.shape
    assert t % OFF_CHUNK == 0 and t // OFF_CHUNK < LANES
    return pl.pallas_call(
        functools.partial(_select_kernel, cap=cap),
        out_shape=[jax.ShapeDtypeStruct((rows, t), jnp.int32),
                   jax.ShapeDtypeStruct((rows, LANES), jnp.int32)],
        compiler_params=_params(),
        name="select",
    )(scores)


GATHER_BLOCK = 512
GATHER_WINDOW = 128


def _expert_kernel(offs_ref, slot_ref, sc_ref, h_ref, wg_ref, wu_ref, wd_ref, o_ref,
                   xg_s, gs_s, rows_s, gates_s, *, cap, n_off):
    t = h_ref.shape[0]
    tk = min(GATHER_BLOCK, t)
    win = min(GATHER_WINDOW, cap)
    per_block = tk // OFF_CHUNK
    slot_iota = lax.broadcasted_iota(jnp.int32, (win, tk), 0)
    xg_s[...] = jnp.zeros_like(xg_s)
    gs_s[...] = jnp.zeros_like(gs_s)
    row0 = (pl.program_id(0) * N_EXPERTS + pl.program_id(1)) * n_off

    def gather(sl, lo):
        onehot = (slot_iota + lo == slot_ref[:, sl]).astype(BF16)
        rows = _dot(onehot, h_ref[sl, :])
        gates = lax.dot_general(onehot, sc_ref[:, sl], (((1,), (1,)), ((), ())), preferred_element_type=F32)
        return rows, gates

    n_blocks = t // tk
    starts = [pl.multiple_of((offs_ref[row0 + j * per_block] // SUBLANES) * SUBLANES, SUBLANES)
              for j in range(n_blocks)]
    for j in range(n_blocks):
        rows_s[j], gates_s[j] = gather(slice(j * tk, (j + 1) * tk), starts[j])
    for j in range(n_blocks):
        sl = slice(j * tk, (j + 1) * tk)
        st = starts[j]
        last = offs_ref[row0 + (j + 1) * per_block]
        xg_s[pl.ds(st, SUBLANES), :] += rows_s[j, 0:SUBLANES, :]
        xg_s[pl.ds(st + SUBLANES, win - SUBLANES), :] = rows_s[j, SUBLANES:win, :]
        gs_s[pl.ds(st, win), :] += gates_s[j]

        def more(c, carry, sl=sl, st=st):
            lo = pl.multiple_of(st + c * win, SUBLANES)
            rows, gates = gather(sl, lo)
            xg_s[pl.ds(lo, win), :] = rows
            gs_s[pl.ds(lo, win), :] += gates
            return carry

        lax.fori_loop(1, (last - st + win - 1) // win, more, 0)
    xb = xg_s[0:cap, :].astype(BF16)
    gate = _dot(xb, wg_ref[...])
    hid = (gate * _sigmoid(gate)) * _dot(xb, wu_ref[...])
    y = _dot(hid.astype(BF16), wd_ref[...])
    o_ref[...] = (y * (gs_s[0:cap, 0:1] + gs_s[0:cap, 1:2])).astype(BF16)


def _experts(offs, slot, score_rows, h2, wg, wu, wd, cap):
    b, t, d = h2.shape
    n_off = offs.shape[0] // (b * N_EXPERTS)
    win = min(GATHER_WINDOW, cap)
    grid_spec = pltpu.PrefetchScalarGridSpec(
        num_scalar_prefetch=1,
        grid=(b, N_EXPERTS),
        in_specs=[pl.BlockSpec((None, 1, t), lambda bb, e, o: (bb * N_EXPERTS + e, 0, 0)),
                  pl.BlockSpec((None, None, BF16_ROWS, t), lambda bb, e, o: (bb, e, 0, 0)),
                  pl.BlockSpec((None, t, d), lambda bb, e, o: (bb, 0, 0)),
                  pl.BlockSpec((None, d, D_EXPERT), lambda bb, e, o: (e, 0, 0)),
                  pl.BlockSpec((None, d, D_EXPERT), lambda bb, e, o: (e, 0, 0)),
                  pl.BlockSpec((None, D_EXPERT, d), lambda bb, e, o: (e, 0, 0))],
        out_specs=pl.BlockSpec((None, cap, d), lambda bb, e, o: (bb * N_EXPERTS + e, 0, 0)),
        scratch_shapes=[pltpu.VMEM((cap + win, d), F32), pltpu.VMEM((cap + win, BF16_ROWS), F32),
                        pltpu.VMEM((t // min(GATHER_BLOCK, t), win, d), F32),
                        pltpu.VMEM((t // min(GATHER_BLOCK, t), win, BF16_ROWS), F32)],
    )
    return pl.pallas_call(
        functools.partial(_expert_kernel, cap=cap, n_off=n_off),
        grid_spec=grid_spec,
        out_shape=jax.ShapeDtypeStruct((b * N_EXPERTS, cap, d), BF16),
        compiler_params=_params("parallel", "parallel"),
        name="experts",
    )(offs, slot.reshape(b * N_EXPERTS, 1, t), score_rows, h2, wg, wu, wd)


COMBINE_BLOCK = 1024
SLOT_WINDOW = 256


COMBINE_GROUP = 8


def _combine_kernel(offs_ref, slot_ref, y_ref, x1_ref, mod_ref, g_ref, o_ref, acc_s, oh_s, yw_s, *, cap, n_off):
    tm = x1_ref.shape[0]
    kc = min(SLOT_WINDOW, cap)
    per_block = tm // OFF_CHUNK
    tg = tm // ROW_GROUPS
    group_k = COMBINE_GROUP * kc

    def place(e, start0, c, rows):
        lo = start0 + c * kc
        st = pl.multiple_of(jnp.minimum(lo, cap - kc), BF16_ROWS)
        col = slot_ref[rows, e:e + 1]
        onehot = col == st + lax.broadcasted_iota(jnp.int32, (col.shape[0], kc), 1)
        if not isinstance(c, int) or c > 0:
            onehot = onehot & (col >= lo)
        return onehot.astype(BF16), st

    starts, extras = [], []
    for e in range(N_EXPERTS):
        o0 = (pl.program_id(0) * N_EXPERTS + e) * n_off + pl.program_id(1) * per_block
        start0 = (offs_ref[o0] // BF16_ROWS) * BF16_ROWS
        starts.append(start0)
        extras.append(jnp.maximum((offs_ref[o0 + per_block] - start0 + kc - 1) // kc - 1, 0))

    for e in range(N_EXPERTS):
        st = pl.multiple_of(jnp.minimum(starts[e], cap - kc), BF16_ROWS)
        yw_s[e * kc:(e + 1) * kc, :] = y_ref[pl.ds(e * cap + st, kc), :]

    def finish(rows, total):
        o_ref[rows, :] = _rmsnorm(x1_ref[rows, :] + mod_ref[5:6, :] * total, g_ref[...])

    accs = []
    for grp in range(ROW_GROUPS):
        rows = slice(grp * tg, (grp + 1) * tg)
        acc = None
        for gi in range(N_EXPERTS // COMBINE_GROUP):
            for i in range(COMBINE_GROUP):
                onehot, _ = place(gi * COMBINE_GROUP + i, starts[gi * COMBINE_GROUP + i], 0, rows)
                oh_s[gi, :, i * kc:(i + 1) * kc] = onehot
            part = _dot(oh_s[gi], yw_s[gi * group_k:(gi + 1) * group_k, :])
            acc = part if acc is None else acc + part
        finish(rows, acc)
        accs.append(acc)

    @pl.when(sum(extras) > 0)
    def _():
        every = slice(0, tm)
        for grp in range(ROW_GROUPS):
            acc_s[grp * tg:(grp + 1) * tg, :] = accs[grp]
        for e in range(N_EXPERTS):
            def more(c, carry, e=e):
                onehot, st = place(e, starts[e], c, every)
                acc_s[...] += _dot(onehot, y_ref[pl.ds(e * cap + st, kc), :])
                return carry
            lax.fori_loop(1, extras[e] + 1, more, 0)
        finish(every, acc_s[...])


def _combine(offs, slot_t, y, x1, mods, g, cap):
    b, t, d = x1.shape
    tm = min(COMBINE_BLOCK, t)
    n_off = offs.shape[0] // (b * N_EXPERTS)
    grid_spec = pltpu.PrefetchScalarGridSpec(
        num_scalar_prefetch=1,
        grid=(b, t // tm),
        in_specs=[pl.BlockSpec((None, tm, N_EXPERTS), lambda bb, i, o: (bb, i, 0)),
                  pl.BlockSpec((None, N_EXPERTS * cap, d), lambda bb, i, o: (bb, 0, 0),
                               pipeline_mode=pl.Buffered(1)),
                  pl.BlockSpec((None, tm, d), lambda bb, i, o: (bb, i, 0)),
                  pl.BlockSpec((None, SUBLANES, d), lambda bb, i, o: (bb, 0, 0)),
                  pl.BlockSpec((1, d), lambda bb, i, o: (0, 0))],
        out_specs=pl.BlockSpec((None, tm, d), lambda bb, i, o: (bb, i, 0)),
        scratch_shapes=[pltpu.VMEM((tm, d), F32),
                        pltpu.VMEM((N_EXPERTS // COMBINE_GROUP, tm // ROW_GROUPS,
                                    COMBINE_GROUP * min(SLOT_WINDOW, cap)), BF16),
                        pltpu.VMEM((N_EXPERTS * min(SLOT_WINDOW, cap), d), BF16)],
    )
    return pl.pallas_call(
        functools.partial(_combine_kernel, cap=cap, n_off=n_off),
        grid_spec=grid_spec,
        out_shape=jax.ShapeDtypeStruct((b, t, d), F32),
        compiler_params=_params("parallel", "parallel"),
        name="combine",
    )(offs, slot_t, y, x1, mods, g)


def kernel(x, c, ctx, c_ctx, w_ada, b_ada, norm1_g, norm2_g, w_in, w_four, conv_w, conv_b, lru_lambda, lru_wa, lru_ba, lru_wi, lru_bi, w_lru, w_out, w_router, w_gate_e, w_up_e, w_down_e, final_g):
    b, t, d = x.shape
    assert d == D_MODEL and b == SUBLANES and t % TOKEN_BLOCK == 0 and ctx.shape[1] % SCAN_CHUNK == 0
    assert w_ada.shape[0] == 1, "single-layer problem: the context stream is only read by the recurrence"
    cap = CAPACITY_FACTOR * t // N_EXPERTS
    col_x = D_FOURIER + D_RNN
    col_g = col_x + D_RNN
    col_a = col_g + D_MODEL

    pos = _pos_table(t)
    ct, st = _dft_tables(t)

    cc = jnp.concatenate([c, c_ctx[None], jnp.zeros((SUBLANES - 1, d), F32)], axis=0)
    mods = _ada(cc, w_ada[0], b_ada[0]).reshape(cc.shape[0], N_MOD, d)
    mods = jnp.pad(mods, ((0, 0), (0, SUBLANES - N_MOD), (0, 0)))

    w = w_in[0].astype(BF16)
    g1 = norm1_g[0].reshape(1, d)
    zx, zg, za, zb, zf, wg_b, wu_b, wd_b = _proj(
        x, pos, mods, None, g1, w,
        ((D_FOURIER, D_RNN), (col_x, D_RNN), (col_g, D_MODEL), (col_a, D_MODEL), (0, D_FOURIER)),
        casts=(w_gate_e[0].reshape(N_EXPERTS * d, D_EXPERT), w_up_e[0].reshape(N_EXPERTS * d, D_EXPERT),
               w_down_e[0].reshape(N_EXPERTS * D_EXPERT, d)))
    (zx_ctx,) = _proj(ctx, None, mods, b, g1, w, ((D_FOURIER, D_RNN),))

    for dr in range(2):
        vecs = jnp.concatenate([lru_lambda[0, dr][None], 0.5 * lru_ba[0, dr][None], 0.5 * lru_bi[0, dr][None],
                                conv_b[0][None], conv_w[0]], axis=0)
        wg = (0.5 * jnp.concatenate([lru_wa[0, dr], lru_wi[0, dr]], axis=-1)).astype(BF16)
        if dr == 0:
            u_lat = _conv(zx, vecs)
            u_ctx = _conv(zx_ctx, vecs)
        _, h_ctx = _scan(u_ctx, vecs, wg, jnp.zeros((b, D_RNN), F32), reverse=bool(dr))
        if dr == 0:
            h_fwd, _ = _scan(u_lat, vecs, wg, h_ctx, reverse=False)
        else:
            gr, _ = _scan(u_lat, vecs, wg, h_ctx, reverse=True, hf=h_fwd, zg=zg)

    f = _dft_apply(ct, st, zf)

    x1, h2, scores, score_rows = _merge(
        f, gr, za, zb, x, pos, mods, norm2_g[0].reshape(1, d),
        w_four[0].astype(BF16), w_lru[0].astype(BF16), w_out[0].astype(BF16), w_router[0].T.astype(BF16))

    slot, offs = _select(scores.reshape(b * N_EXPERTS, t), cap)
    offs = offs[:, :t // OFF_CHUNK + 1].reshape(-1)
    y = _experts(offs, slot, score_rows, h2, wg_b.reshape(N_EXPERTS, d, D_EXPERT),
                 wu_b.reshape(N_EXPERTS, d, D_EXPERT), wd_b.reshape(N_EXPERTS, D_EXPERT, d), cap)
    slot_t = jnp.swapaxes(slot.reshape(b, N_EXPERTS, t), 1, 2)
    return _combine(offs, slot_t, y.reshape(b, N_EXPERTS * cap, d), x1, mods, final_g.reshape(1, d), cap)
```

```python
import functools
import math

import numpy as np
import jax
import jax.numpy as jnp
from jax import lax
from jax.experimental import pallas as pl
from jax.experimental.pallas import tpu as pltpu

F32 = jnp.float32
BF16 = jnp.bfloat16

D_MODEL = 1024
GRID_W = 64
N_FOURIER_GROUPS = 4
FOURIER_GROUP = 128
D_FOURIER = N_FOURIER_GROUPS * FOURIER_GROUP
D_RNN = D_MODEL
N_RNN_HEADS = 8
RNN_HEAD = D_RNN // N_RNN_HEADS
CONV_W = 4
CONV_LEFT = 2
LRU_C = 8.0
LOG2E = math.log2(math.e)
N_EXPERTS = 16
CAPACITY_FACTOR = 2
D_EXPERT = 1536
N_MOD = 6
EPS = 1e-6
POS_MAX_PERIOD = 10000.0

LANES = 128
SUBLANES = 8
BF16_ROWS = 16
VMEM_LIMIT = 58 * 1024 * 1024

TOKEN_BLOCK = 512
ROW_GROUPS = 2
SCAN_CHUNK = 128


def _params(*sem):
    return pltpu.CompilerParams(dimension_semantics=sem, vmem_limit_bytes=VMEM_LIMIT)


def _sigmoid(x):
    return 0.5 * jnp.tanh(0.5 * x) + 0.5


def _rmsnorm(x, g):
    return x * lax.rsqrt(jnp.mean(x * x, axis=-1, keepdims=True) + EPS) * g


def _dot(a, b):
    return jnp.dot(a, b, preferred_element_type=F32)


def _pos_kernel(o_ref):
    rows = o_ref.shape[0]
    q = D_MODEL // 4
    k = lax.broadcasted_iota(jnp.int32, (1, q), 1).astype(F32)
    freqs = jnp.exp(-math.log(POS_MAX_PERIOD) * k / q)
    ang_r = lax.broadcasted_iota(jnp.int32, (rows, q), 0).astype(F32) * freqs
    ang_c = lax.broadcasted_iota(jnp.int32, (GRID_W, q), 0).astype(F32) * freqs
    shape = (rows, GRID_W, q)
    o_ref[:, :, 0 * q:1 * q] = jnp.broadcast_to(jnp.sin(ang_r)[:, None, :], shape)
    o_ref[:, :, 1 * q:2 * q] = jnp.broadcast_to(jnp.cos(ang_r)[:, None, :], shape)
    o_ref[:, :, 2 * q:3 * q] = jnp.broadcast_to(jnp.sin(ang_c)[None, :, :], shape)
    o_ref[:, :, 3 * q:4 * q] = jnp.broadcast_to(jnp.cos(ang_c)[None, :, :], shape)


def _pos_table(n_tokens):
    rows = n_tokens // GRID_W
    out = pl.pallas_call(
        _pos_kernel,
        out_shape=jax.ShapeDtypeStruct((rows, GRID_W, D_MODEL), F32),
        name="pos_table",
    )()
    return out.reshape(n_tokens, D_MODEL)


RADIX = 8


def _dft_kernel(c_ref, s_ref, *, n):
    ns = c_ref.shape[1]
    k2 = lax.broadcasted_iota(jnp.int32, (ns, ns), 0)
    t2 = lax.broadcasted_iota(jnp.int32, (ns, ns), 1)
    common = ((k2 * t2) & (ns - 1)).astype(F32) * (2.0 * math.pi / ns)
    cos_c = jnp.cos(common)
    sin_c = jnp.sin(common)
    scale = 1.0 / math.sqrt(n)
    t_row = lax.broadcasted_iota(jnp.int32, (1, ns), 1)
    for k1 in range(RADIX):
        slab = (k1 * t_row).astype(F32) * (2.0 * math.pi / n)
        cos_s = jnp.cos(slab) * scale
        sin_s = jnp.sin(slab) * scale
        c_ref[k1] = (cos_c * cos_s - sin_c * sin_s).astype(BF16)
        s_ref[k1] = (sin_c * cos_s + cos_c * sin_s).astype(BF16)


def _dft_tables(n):
    assert n & (n - 1) == 0 and n % (RADIX * BF16_ROWS) == 0
    ns = n // RADIX
    return pl.pallas_call(
        functools.partial(_dft_kernel, n=n),
        out_shape=[jax.ShapeDtypeStruct((RADIX, ns, ns), BF16)] * 2,
        compiler_params=_params(),
        name="dft_tables",
    )()


def _ada_kernel(c_ref, w_ref, b_ref, o_ref):
    c = c_ref[...]
    o_ref[...] = jnp.dot(c * _sigmoid(c), w_ref[...], preferred_element_type=F32,
                         precision=lax.Precision.HIGHEST) + b_ref[...]


def _ada(cc, w_ada, b_ada):
    n = w_ada.shape[1]
    tn = D_MODEL
    return pl.pallas_call(
        _ada_kernel,
        grid=(n // tn,),
        in_specs=[pl.BlockSpec(cc.shape, lambda j: (0, 0)),
                  pl.BlockSpec((D_MODEL, tn), lambda j: (0, j)),
                  pl.BlockSpec((1, tn), lambda j: (0, j))],
        out_specs=pl.BlockSpec((cc.shape[0], tn), lambda j: (0, j)),
        out_shape=jax.ShapeDtypeStruct((cc.shape[0], n), F32),
        compiler_params=_params("parallel"),
        name="ada",
    )(cc, w_ada, b_ada.reshape(1, n))


def _proj_kernel(*refs, cols, has_pos, n_cast):
    refs = list(refs)
    if n_cast:
        outs_cast = refs[-n_cast:]
        del refs[-n_cast:]
        n_in = len(refs) - len(cols)
        for w32_ref, w16_ref in zip(refs[n_in - n_cast:n_in], outs_cast):
            w16_ref[...] = w32_ref[...].astype(BF16)
        del refs[n_in - n_cast:n_in]
    if has_pos:
        x_ref, pos_ref, mod_ref, g_ref, w_ref = refs[:5]
        outs = refs[5:]
    else:
        x_ref, mod_ref, g_ref, w_ref = refs[:4]
        outs = refs[4:]
    tg = x_ref.shape[0] // ROW_GROUPS
    for grp in range(ROW_GROUPS):
        rs = slice(grp * tg, (grp + 1) * tg)
        x = x_ref[rs, :] + pos_ref[rs, :] if has_pos else x_ref[rs, :]
        h = _rmsnorm(x, g_ref[...]) * (1.0 + mod_ref[1:2, :]) + mod_ref[0:1, :]
        hb = h.astype(BF16)
        for o_ref, (c0, wd) in zip(outs, cols):
            o_ref[rs, :] = _dot(hb, w_ref[:, c0:c0 + wd]).astype(BF16)


def _proj(x, pos, mods, mod_row, g, w, cols, casts=()):
    b, t, d = x.shape
    tm = min(TOKEN_BLOCK, t)
    has_pos = pos is not None
    in_specs = [pl.BlockSpec((None, tm, d), lambda i, bb: (bb, i, 0))]
    args = [x]
    if has_pos:
        in_specs.append(pl.BlockSpec((tm, d), lambda i, bb: (i, 0)))
        args.append(pos)
    if mod_row is None:
        mod_map = lambda i, bb: (bb, 0, 0)
    else:
        mod_map = lambda i, bb: (mod_row, 0, 0)
    in_specs += [pl.BlockSpec((None, SUBLANES, d), mod_map),
                 pl.BlockSpec((1, d), lambda i, bb: (0, 0)),
                 pl.BlockSpec(w.shape, lambda i, bb: (0, 0), pipeline_mode=pl.Buffered(1))]
    args += [mods, g, w]
    out_specs = [pl.BlockSpec((None, tm, wd), lambda i, bb: (bb, i, 0)) for _, wd in cols]
    out_shape = [jax.ShapeDtypeStruct((b, t, wd), BF16) for _, wd in cols]
    for w32 in casts:
        spec, shape = _cast_specs(w32, (t // tm) * b, lambda i, bb: i * b + bb)
        in_specs.append(spec)
        args.append(w32)
        out_specs.append(spec)
        out_shape.append(shape)
    return pl.pallas_call(
        functools.partial(_proj_kernel, cols=cols, has_pos=has_pos, n_cast=len(casts)),
        grid=(t // tm, b),
        in_specs=in_specs,
        out_specs=out_specs,
        out_shape=out_shape,
        compiler_params=_params("parallel", "parallel"),
        name="proj",
    )(*args)


def _gelu(x):
    return 0.5 * x * (1.0 + jnp.tanh(math.sqrt(2.0 / math.pi) * (x + 0.044715 * x * x * x)))


def _conv_kernel(x_ref, xp_ref, xn_ref, vec_ref, o_ref):
    tm = x_ref.shape[0]
    edge = BF16_ROWS
    i = pl.program_id(0)
    first = i == 0
    last = i == pl.num_programs(0) - 1
    w = [vec_ref[4 + kk:5 + kk, :] for kk in range(CONV_W)]
    bias = vec_ref[3:4, :]

    def taps(xm2, xm1, x0, xp1):
        return bias + w[0] * xm2 + w[1] * xm1 + w[2] * x0 + w[3] * xp1

    x = x_ref[...].astype(F32)
    o_ref[...] = taps(pltpu.roll(x, 2, 0), pltpu.roll(x, 1, 0), x, pltpu.roll(x, tm - 1, 0)).astype(BF16)

    row = lax.broadcasted_iota(jnp.int32, (edge, x.shape[1]), 0)

    def shifted(before, cur, after, s):
        if s < 0:
            return jnp.where(row < -s, pltpu.roll(before, -s, 0), pltpu.roll(cur, -s, 0))
        return jnp.where(row < edge - s, pltpu.roll(cur, edge - s, 0), pltpu.roll(after, edge - s, 0))

    prev = jnp.where(first, 0.0, xp_ref[...].astype(F32))
    nxt = jnp.where(last, 0.0, xn_ref[...].astype(F32))
    head, head2 = x[0:edge], x[edge:2 * edge]
    tail, tail2 = x[tm - edge:tm], x[tm - 2 * edge:tm - edge]
    o_ref[0:edge, :] = taps(shifted(prev, head, head2, -2), shifted(prev, head, head2, -1), head,
                            shifted(prev, head, head2, 1)).astype(BF16)
    o_ref[tm - edge:tm, :] = taps(shifted(tail2, tail, nxt, -2), shifted(tail2, tail, nxt, -1), tail,
                                  shifted(tail2, tail, nxt, 1)).astype(BF16)


def _conv(zx, vecs):
    b, t, c = zx.shape
    tm = min(TOKEN_BLOCK, t)
    assert tm >= 2 * BF16_ROWS
    hpb = tm // BF16_ROWS
    n_halo = t // BF16_ROWS
    return pl.pallas_call(
        _conv_kernel,
        grid=(t // tm, b),
        in_specs=[pl.BlockSpec((None, tm, c), lambda i, bb: (bb, i, 0)),
                  pl.BlockSpec((None, BF16_ROWS, c), lambda i, bb: (bb, jnp.maximum(i * hpb - 1, 0), 0)),
                  pl.BlockSpec((None, BF16_ROWS, c), lambda i, bb: (bb, jnp.minimum((i + 1) * hpb, n_halo - 1), 0)),
                  pl.BlockSpec(vecs.shape, lambda i, bb: (0, 0))],
        out_specs=pl.BlockSpec((None, tm, c), lambda i, bb: (bb, i, 0)),
        out_shape=jax.ShapeDtypeStruct((b, t, c), BF16),
        compiler_params=_params("parallel", "parallel"),
        name="conv",
    )(zx, zx, zx, vecs)


def _scan_kernel(*refs, reverse, fuse):
    if fuse:
        u_ref, vec_ref, wg_ref, h0_ref, hf_ref, zg_ref, o_ref, hT_ref, a_s, b_s, h_s, hc_s = refs
    else:
        u_ref, vec_ref, wg_ref, h0_ref, o_ref, hT_ref, a_s, b_s, h_s, hc_s = refs
    nb, tc, c = u_ref.shape

    @pl.when(pl.program_id(0) == 0)
    def _():
        for hh in range(N_RNN_HEADS):
            hc_s[hh] = h0_ref[:, hh * RNN_HEAD:(hh + 1) * RNN_HEAD]

    lam = vec_ref[0:1, :]
    half_b_a = vec_ref[1:2, :]
    half_b_i = vec_ref[2:3, :]
    k_row = (-0.5 * LRU_C * LOG2E) * (jnp.maximum(-lam, 0.0) + jnp.log1p(jnp.exp(-jnp.abs(lam))))

    for bb in range(nb):
        ub = u_ref[bb]
        u = ub.astype(F32)
        rows = pl.ds(bb, tc, stride=nb)
        for hh in range(N_RNN_HEADS):
            sl = slice(hh * RNN_HEAD, (hh + 1) * RNN_HEAD)
            gates = _dot(ub[:, sl], wg_ref[hh])
            t_r = jnp.tanh(gates[:, :RNN_HEAD] + half_b_a[:, sl])
            t_i = jnp.tanh(gates[:, RNN_HEAD:] + half_b_i[:, sl])
            a = jnp.exp2(k_row[:, sl] * t_r + k_row[:, sl])
            a_s[hh, rows, :] = a
            scale = jnp.exp2((0.5 * LOG2E) * jnp.log(1.0 - a * a)) * u[:, sl]
            b_s[hh, rows, :] = scale * (0.5 * t_i + 0.5)

    def body(j, hs):
        t = (tc - 1 - j) if reverse else j
        idx = pl.ds(pl.multiple_of(t * nb, nb), nb)
        new = []
        for hh in range(N_RNN_HEADS):
            h = a_s[hh, idx, :] * hs[hh] + b_s[hh, idx, :]
            h_s[hh, idx, :] = h
            new.append(h)
        return tuple(new)

    hs = lax.fori_loop(0, tc, body, tuple(hc_s[hh] for hh in range(N_RNN_HEADS)), unroll=8)
    for hh in range(N_RNN_HEADS):
        hc_s[hh] = hs[hh]
        hT_ref[:, hh * RNN_HEAD:(hh + 1) * RNN_HEAD] = hs[hh]

    for bb in range(nb):
        for hh in range(N_RNN_HEADS):
            sl = slice(hh * RNN_HEAD, (hh + 1) * RNN_HEAD)
            hb = h_s[hh, pl.ds(bb, tc, stride=nb), :]
            if fuse:
                hb = _gelu(zg_ref[bb, :, sl].astype(F32)) * (hf_ref[bb, :, sl].astype(F32) + hb)
            o_ref[bb, :, sl] = hb.astype(BF16)


def _cast_specs(w, n_steps, step_index):
    rows, cols = w.shape
    assert rows % (n_steps * BF16_ROWS) == 0
    spec = pl.BlockSpec((rows // n_steps, cols), lambda *idx: (step_index(*idx), 0))
    return spec, jax.ShapeDtypeStruct((rows, cols), BF16)


def _scan(u, vecs, wg, h0, *, reverse, hf=None, zg=None):
    nb, t, c = u.shape
    tc = SCAN_CHUNK
    n_chunks = t // tc
    fuse = hf is not None

    def cidx(i):
        return (n_chunks - 1 - i) if reverse else i

    main_spec = pl.BlockSpec((nb, tc, c), lambda i: (0, cidx(i), 0))
    in_specs = [
        main_spec,
        pl.BlockSpec(vecs.shape, lambda i: (0, 0)),
        pl.BlockSpec(wg.shape, lambda i: (0, 0, 0)),
        pl.BlockSpec(h0.shape, lambda i: (0, 0)),
    ]
    args = [u, vecs, wg, h0]
    if fuse:
        in_specs += [main_spec, main_spec]
        args += [hf, zg]
    return pl.pallas_call(
        functools.partial(_scan_kernel, reverse=reverse, fuse=fuse),
        grid=(n_chunks,),
        in_specs=in_specs,
        out_specs=[main_spec, pl.BlockSpec((nb, c), lambda i: (0, 0))],
        out_shape=[jax.ShapeDtypeStruct((nb, t, c), BF16), jax.ShapeDtypeStruct((nb, c), F32)],
        scratch_shapes=[pltpu.VMEM((N_RNN_HEADS, nb * tc, RNN_HEAD), F32),
                        pltpu.VMEM((N_RNN_HEADS, nb * tc, RNN_HEAD), F32),
                        pltpu.VMEM((N_RNN_HEADS, nb * tc, RNN_HEAD), F32),
                        pltpu.VMEM((N_RNN_HEADS, nb, RNN_HEAD), F32)],
        compiler_params=_params("arbitrary"),
        name="scan_bwd" if reverse else "scan_fwd",
    )(*args)


DFT_ROWS = 64
DFT_COLS = 256


def _dftmm_kernel(zf_ref, cs_ref, c_ref, s_ref, o_ref, zc_s, zs_s, ar_s, ai_s, o_s):
    t, f = zf_ref.shape
    ns = t // RADIX
    rb = min(DFT_ROWS, ns)
    r2 = math.sqrt(0.5)

    cs_table = cs_ref[...].astype(BF16)
    for j in range(RADIX):
        rows = slice(j * ns, (j + 1) * ns)
        for g in range(N_FOURIER_GROUPS):
            sl = slice(g * FOURIER_GROUP, (g + 1) * FOURIER_GROUP)
            cs = _dot(zf_ref[rows, sl], cs_table)
            zc_s[rows, sl] = cs[:, :FOURIER_GROUP].astype(BF16)
            zs_s[rows, sl] = cs[:, FOURIER_GROUP:].astype(BF16)

    def dft4(x):
        s0 = (x[0][0] + x[2][0], x[0][1] + x[2][1])
        s1 = (x[0][0] - x[2][0], x[0][1] - x[2][1])
        s2 = (x[1][0] + x[3][0], x[1][1] + x[3][1])
        s3 = (x[1][0] - x[3][0], x[1][1] - x[3][1])
        return [(s0[0] + s2[0], s0[1] + s2[1]), (s1[0] + s3[1], s1[1] - s3[0]),
                (s0[0] - s2[0], s0[1] - s2[1]), (s1[0] - s3[1], s1[1] + s3[0])]

    def butterfly(i, carry):
        r0 = pl.multiple_of(i * rb, rb)
        w = [(zc_s[pl.ds(j * ns + r0, rb), :].astype(F32), -zs_s[pl.ds(j * ns + r0, rb), :].astype(F32))
             for j in range(RADIX)]
        u = [(w[j][0] + w[j + 4][0], w[j][1] + w[j + 4][1]) for j in range(4)]
        v = [(w[j][0] - w[j + 4][0], w[j][1] - w[j + 4][1]) for j in range(4)]
        v = [v[0],
             ((v[1][0] + v[1][1]) * r2, (v[1][1] - v[1][0]) * r2),
             (v[2][1], -v[2][0]),
             ((v[3][1] - v[3][0]) * r2, (-v[3][0] - v[3][1]) * r2)]
        even = dft4(u)
        odd = dft4(v)
        for m in range(4):
            for k1, a in ((2 * m, even[m]), (2 * m + 1, odd[m])):
                ar_s[k1, pl.ds(r0, rb), :] = a[0].astype(BF16)
                ai_s[k1, pl.ds(r0, rb), :] = a[1].astype(BF16)
        return carry

    lax.fori_loop(0, ns // rb, butterfly, 0)

    for c0 in range(0, f, DFT_COLS):
        for k1 in range(RADIX):
            res = (_dot(c_ref[k1], ar_s[k1, :, c0:c0 + DFT_COLS])
                   + _dot(s_ref[k1], ai_s[k1, :, c0:c0 + DFT_COLS]))
            for j in range(DFT_COLS // LANES):
                o_s[j, pl.ds(k1, ns, stride=RADIX), :] = res[:, j * LANES:(j + 1) * LANES]
        for j in range(DFT_COLS // LANES):
            o_ref[:, c0 + j * LANES:c0 + (j + 1) * LANES] = o_s[j].astype(BF16)


def _channel_dft_table():
    p = np.arange(FOURIER_GROUP)
    ang = 2.0 * np.pi * ((p[:, None] * p[None, :]) % FOURIER_GROUP) / FOURIER_GROUP
    cs = np.concatenate([np.cos(ang), np.sin(ang)], axis=1) / math.sqrt(FOURIER_GROUP)
    return jnp.asarray(cs, F32)


def _dft_apply(ct, st, zf):
    b, t, f = zf.shape
    ns = t // RADIX
    table = pl.BlockSpec((RADIX, ns, ns), lambda bb: (0, 0, 0), pipeline_mode=pl.Buffered(1))
    cs = _channel_dft_table()
    return pl.pallas_call(
        _dftmm_kernel,
        grid=(b,),
        in_specs=[pl.BlockSpec((None, t, f), lambda bb: (bb, 0, 0)),
                  pl.BlockSpec(cs.shape, lambda bb: (0, 0)),
                  table, table],
        out_specs=pl.BlockSpec((None, t, f), lambda bb: (bb, 0, 0)),
        out_shape=jax.ShapeDtypeStruct((b, t, f), BF16),
        scratch_shapes=[pltpu.VMEM((t, f), BF16),
                        pltpu.VMEM((t, f), BF16),
                        pltpu.VMEM((RADIX, ns, f), BF16),
                        pltpu.VMEM((RADIX, ns, f), BF16),
                        pltpu.VMEM((DFT_COLS // LANES, t, LANES), F32)],
        compiler_params=_params("parallel"),
        name="dft_apply",
    )(zf, cs, ct, st)


def _merge_kernel(f_ref, gr_ref, za_ref, zb_ref, x_ref, pos_ref, mod_ref, g2_ref,
                  wf_ref, wl_ref, wo_ref, wr_ref, x1_ref, h2_ref, sc_ref, scr_ref):
    y_four = _dot(f_ref[...], wf_ref[...])
    y_rnn = _dot(gr_ref[...], wl_ref[...])
    m = _sigmoid(za_ref[...].astype(F32)) * y_four + _sigmoid(zb_ref[...].astype(F32)) * y_rnn
    y = _dot(m.astype(BF16), wo_ref[...])
    x1 = (x_ref[...] + pos_ref[...]) + mod_ref[2:3, :] * y
    x1_ref[...] = x1
    h2 = (_rmsnorm(x1, g2_ref[...]) * (1.0 + mod_ref[4:5, :]) + mod_ref[3:4, :]).astype(BF16)
    h2_ref[...] = h2
    logits = lax.dot_general(wr_ref[...], h2, (((1,), (1,)), ((), ())), preferred_element_type=F32)
    e = jnp.exp(logits - jnp.max(logits, axis=0, keepdims=True))
    sc = e / jnp.sum(e, axis=0, keepdims=True)
    sc_ref[...] = sc
    hi = sc.astype(BF16).astype(F32)
    lo = sc - hi
    tm = sc.shape[1]
    row = lax.broadcasted_iota(jnp.int32, (BF16_ROWS, tm), 0)
    for ex in range(N_EXPERTS):
        tile = jnp.where(row == 0, hi[ex:ex + 1, :], jnp.where(row == 1, lo[ex:ex + 1, :], 0.0))
        scr_ref[ex] = tile.astype(BF16)


def _merge(f, gr, za, zb, x, pos, mods, g2, w_four, w_lru, w_out, w_rt):
    b, t, d = x.shape
    tm = min(TOKEN_BLOCK, t)
    tok = lambda wd: pl.BlockSpec((None, tm, wd), lambda i, bb: (bb, i, 0))
    full = lambda a: pl.BlockSpec(a.shape, lambda i, bb: (0,) * a.ndim)
    return pl.pallas_call(
        _merge_kernel,
        grid=(t // tm, b),
        in_specs=[tok(D_FOURIER), tok(d), tok(d), tok(d), tok(d),
                  pl.BlockSpec((tm, d), lambda i, bb: (i, 0)),
                  pl.BlockSpec((None, SUBLANES, d), lambda i, bb: (bb, 0, 0)),
                  full(g2), full(w_four), full(w_lru), full(w_out), full(w_rt)],
        out_specs=[tok(d), tok(d), pl.BlockSpec((None, N_EXPERTS, tm), lambda i, bb: (bb, 0, i)),
                   pl.BlockSpec((None, N_EXPERTS, BF16_ROWS, tm), lambda i, bb: (bb, 0, 0, i))],
        out_shape=[jax.ShapeDtypeStruct((b, t, d), F32),
                   jax.ShapeDtypeStruct((b, t, d), BF16),
                   jax.ShapeDtypeStruct((b, N_EXPERTS, t), F32),
                   jax.ShapeDtypeStruct((b, N_EXPERTS, BF16_ROWS, t), BF16)],
        compiler_params=_params("parallel", "parallel"),
        name="merge",
    )(f, gr, za, zb, x, pos, mods, g2, w_four, w_lru, w_out, w_rt)


MAX_EXP = 127
EXP_STEPS = 7
VAL_STEPS = 56


OFF_CHUNK = 256


def _select_kernel(s_ref, slot_ref, offs_ref, *, cap):
    rows, t = s_ref.shape

    def count(mask):
        return jnp.sum(jnp.where(mask, 1.0, 0.0), axis=1, keepdims=True)

    def exp_step(_, carry):
        k_lo, k_hi = carry
        k_mid = jnp.floor(0.5 * (k_lo + k_hi))
        ok = count(s_ref[...] >= jnp.exp2(-k_mid)) >= cap
        return jnp.where(ok, k_lo, k_mid), jnp.where(ok, k_mid, k_hi)

    k_lo, k_hi = lax.fori_loop(
        0, EXP_STEPS, exp_step,
        (jnp.full((rows, 1), -1.0, F32), jnp.full((rows, 1), float(MAX_EXP), F32)))
    lo0 = jnp.where(k_hi >= MAX_EXP, 0.0, jnp.exp2(-k_hi))
    hi0 = jnp.exp2(-k_lo)

    def val_step(_, carry):
        lo, hi = carry
        mid = 0.5 * (lo + hi)
        ok = count(s_ref[...] >= mid) >= cap
        return jnp.where(ok, mid, lo), jnp.where(ok, hi, mid)

    lo, hi = lax.fori_loop(0, VAL_STEPS, val_step, (lo0, hi0))
    s = s_ref[...]
    gt = s >= hi
    eq = (s >= lo) & (s < hi)
    need = cap - count(gt)

    p = lax.broadcasted_iota(jnp.int32, (LANES, LANES), 0)
    q = lax.broadcasted_iota(jnp.int32, (LANES, LANES), 1)
    tri = (p < q).astype(BF16)
    run_gt = jnp.zeros((rows, 1), F32)
    run_eq = jnp.zeros((rows, 1), F32)
    off_lane = lax.broadcasted_iota(jnp.int32, (rows, LANES), 1)
    offs = jnp.zeros((rows, LANES), F32)
    for j in range(t // LANES):
        if (j * LANES) % OFF_CHUNK == 0:
            offs = jnp.where(off_lane == (j * LANES) // OFF_CHUNK, run_gt + jnp.minimum(run_eq, need), offs)
        sl = slice(j * LANES, (j + 1) * LANES)
        g = gt[:, sl]
        q_ = eq[:, sl]
        pg = _dot(g.astype(BF16), tri) + run_gt
        pe = _dot(q_.astype(BF16), tri) + run_eq
        sel = g | (q_ & (pe < need))
        pos = pg + jnp.minimum(pe, need)
        slot_ref[:, sl] = jnp.where(sel, pos, -1.0).astype(jnp.int32)
        run_gt = run_gt + count(g)
        run_eq = run_eq + count(q_)
    offs = jnp.where(off_lane == t // OFF_CHUNK, run_gt + jnp.minimum(run_eq, need), offs)
    offs_ref[...] = offs.astype(jnp.int32)


def _select(scores, cap):
    rows, t = scores.shape
    assert t % OFF_CHUNK == 0 and t // OFF_CHUNK < LANES
    return pl.pallas_call(
        functools.partial(_select_kernel, cap=cap),
        out_shape=[jax.ShapeDtypeStruct((rows, t), jnp.int32),
                   jax.ShapeDtypeStruct((rows, LANES), jnp.int32)],
        compiler_params=_params(),
        name="select",
    )(scores)


GATHER_BLOCK = 512
GATHER_WINDOW = 128


def _expert_kernel(offs_ref, slot_ref, sc_ref, h_ref, wg_ref, wu_ref, wd_ref, o_ref,
                   xg_s, gs_s, rows_s, gates_s, *, cap, n_off):
    t = h_ref.shape[0]
    tk = min(GATHER_BLOCK, t)
    win = min(GATHER_WINDOW, cap)
    per_block = tk // OFF_CHUNK
    slot_iota = lax.broadcasted_iota(jnp.int32, (win, tk), 0)
    pad = jnp.zeros((SUBLANES, D_MODEL), F32)
    xg_s[0:SUBLANES, :] = pad
    gs_s[...] = jnp.zeros_like(gs_s)
    row0 = (pl.program_id(0) * N_EXPERTS + pl.program_id(1)) * n_off

    def gather(sl, lo):
        onehot = (slot_iota + lo == slot_ref[:, sl]).astype(BF16)
        rows = _dot(onehot, h_ref[sl, :])
        gates = lax.dot_general(onehot, sc_ref[:, sl], (((1,), (1,)), ((), ())), preferred_element_type=F32)
        return rows, gates

    n_blocks = t // tk
    starts = [pl.multiple_of((offs_ref[row0 + j * per_block] // SUBLANES) * SUBLANES, SUBLANES)
              for j in range(n_blocks)]
    for j in range(n_blocks):
        rows_s[j], gates_s[j] = gather(slice(j * tk, (j + 1) * tk), starts[j])
    for j in range(n_blocks):
        sl = slice(j * tk, (j + 1) * tk)
        st = starts[j]
        last = offs_ref[row0 + (j + 1) * per_block]
        xg_s[pl.ds(st, SUBLANES), :] += rows_s[j, 0:SUBLANES, :]
        xg_s[pl.ds(st + SUBLANES, win - SUBLANES), :] = rows_s[j, SUBLANES:win, :]
        xg_s[pl.ds(st + win, SUBLANES), :] = pad
        gs_s[pl.ds(st, win), :] += gates_s[j]

        def more(c, carry, sl=sl, st=st):
            lo = pl.multiple_of(st + c * win, SUBLANES)
            rows, gates = gather(sl, lo)
            xg_s[pl.ds(lo, win), :] = rows
            xg_s[pl.ds(lo + win, SUBLANES), :] = pad
            gs_s[pl.ds(lo, win), :] += gates
            return carry

        lax.fori_loop(1, (last - st + win - 1) // win, more, 0)
    xb = xg_s[0:cap, :].astype(BF16)
    gate = _dot(xb, wg_ref[...])
    hid = (gate * _sigmoid(gate)) * _dot(xb, wu_ref[...])
    y = _dot(hid.astype(BF16), wd_ref[...])
    o_ref[...] = (y * (gs_s[0:cap, 0:1] + gs_s[0:cap, 1:2])).astype(BF16)


def _experts(offs, slot, score_rows, h2, wg, wu, wd, cap):
    b, t, d = h2.shape
    n_off = offs.shape[0] // (b * N_EXPERTS)
    win = min(GATHER_WINDOW, cap)
    grid_spec = pltpu.PrefetchScalarGridSpec(
        num_scalar_prefetch=1,
        grid=(b, N_EXPERTS),
        in_specs=[pl.BlockSpec((None, 1, t), lambda bb, e, o: (bb * N_EXPERTS + e, 0, 0)),
                  pl.BlockSpec((None, None, BF16_ROWS, t), lambda bb, e, o: (bb, e, 0, 0)),
                  pl.BlockSpec((None, t, d), lambda bb, e, o: (bb, 0, 0)),
                  pl.BlockSpec((None, d, D_EXPERT), lambda bb, e, o: (e, 0, 0)),
                  pl.BlockSpec((None, d, D_EXPERT), lambda bb, e, o: (e, 0, 0)),
                  pl.BlockSpec((None, D_EXPERT, d), lambda bb, e, o: (e, 0, 0))],
        out_specs=pl.BlockSpec((None, cap, d), lambda bb, e, o: (bb * N_EXPERTS + e, 0, 0)),
        scratch_shapes=[pltpu.VMEM((cap + win + SUBLANES, d), F32), pltpu.VMEM((cap + win, BF16_ROWS), F32),
                        pltpu.VMEM((t // min(GATHER_BLOCK, t), win, d), F32),
                        pltpu.VMEM((t // min(GATHER_BLOCK, t), win, BF16_ROWS), F32)],
    )
    return pl.pallas_call(
        functools.partial(_expert_kernel, cap=cap, n_off=n_off),
        grid_spec=grid_spec,
        out_shape=jax.ShapeDtypeStruct((b * N_EXPERTS, cap, d), BF16),
        compiler_params=_params("parallel", "parallel"),
        name="experts",
    )(offs, slot.reshape(b * N_EXPERTS, 1, t), score_rows, h2, wg, wu, wd)


COMBINE_BLOCK = 1024
SLOT_WINDOW = 256


COMBINE_GROUP = 8


def _combine_kernel(offs_ref, slot_ref, y_ref, x1_ref, mod_ref, g_ref, o_ref, acc_s, oh_s, yw_s, *, cap, n_off):
    tm = x1_ref.shape[0]
    kc = min(SLOT_WINDOW, cap)
    per_block = tm // OFF_CHUNK
    tg = tm // ROW_GROUPS
    group_k = COMBINE_GROUP * kc

    def place(e, start0, c, rows):
        lo = start0 + c * kc
        st = pl.multiple_of(jnp.minimum(lo, cap - kc), BF16_ROWS)
        col = slot_ref[rows, e:e + 1]
        onehot = col == st + lax.broadcasted_iota(jnp.int32, (col.shape[0], kc), 1)
        if not isinstance(c, int) or c > 0:
            onehot = onehot & (col >= lo)
        return onehot.astype(BF16), st

    starts, extras = [], []
    for e in range(N_EXPERTS):
        o0 = (pl.program_id(0) * N_EXPERTS + e) * n_off + pl.program_id(1) * per_block
        start0 = (offs_ref[o0] // BF16_ROWS) * BF16_ROWS
        starts.append(start0)
        extras.append(jnp.maximum((offs_ref[o0 + per_block] - start0 + kc - 1) // kc - 1, 0))

    for e in range(N_EXPERTS):
        st = pl.multiple_of(jnp.minimum(starts[e], cap - kc), BF16_ROWS)
        yw_s[e * kc:(e + 1) * kc, :] = y_ref[pl.ds(e * cap + st, kc), :]

    def finish(rows, total):
        o_ref[rows, :] = _rmsnorm(x1_ref[rows, :] + mod_ref[5:6, :] * total, g_ref[...])

    accs = []
    for grp in range(ROW_GROUPS):
        rows = slice(grp * tg, (grp + 1) * tg)
        acc = None
        for gi in range(N_EXPERTS // COMBINE_GROUP):
            for i in range(COMBINE_GROUP):
                onehot, _ = place(gi * COMBINE_GROUP + i, starts[gi * COMBINE_GROUP + i], 0, rows)
                oh_s[gi, :, i * kc:(i + 1) * kc] = onehot
            part = _dot(oh_s[gi], yw_s[gi * group_k:(gi + 1) * group_k, :])
            acc = part if acc is None else acc + part
        finish(rows, acc)
        accs.append(acc)

    @pl.when(sum(extras) > 0)
    def _():
        every = slice(0, tm)
        for grp in range(ROW_GROUPS):
            acc_s[grp * tg:(grp + 1) * tg, :] = accs[grp]
        for e in range(N_EXPERTS):
            def more(c, carry, e=e):
                onehot, st = place(e, starts[e], c, every)
                acc_s[...] += _dot(onehot, y_ref[pl.ds(e * cap + st, kc), :])
                return carry
            lax.fori_loop(1, extras[e] + 1, more, 0)
        finish(every, acc_s[...])


def _combine(offs, slot_t, y, x1, mods, g, cap):
    b, t, d = x1.shape
    tm = min(COMBINE_BLOCK, t)
    n_off = offs.shape[0] // (b * N_EXPERTS)
    grid_spec = pltpu.PrefetchScalarGridSpec(
        num_scalar_prefetch=1,
        grid=(b, t // tm),
        in_specs=[pl.BlockSpec((None, tm, N_EXPERTS), lambda bb, i, o: (bb, i, 0)),
                  pl.BlockSpec((None, N_EXPERTS * cap, d), lambda bb, i, o: (bb, 0, 0),
                               pipeline_mode=pl.Buffered(1)),
                  pl.BlockSpec((None, tm, d), lambda bb, i, o: (bb, i, 0)),
                  pl.BlockSpec((None, SUBLANES, d), lambda bb, i, o: (bb, 0, 0)),
                  pl.BlockSpec((1, d), lambda bb, i, o: (0, 0))],
        out_specs=pl.BlockSpec((None, tm, d), lambda bb, i, o: (bb, i, 0)),
        scratch_shapes=[pltpu.VMEM((tm, d), F32),
                        pltpu.VMEM((N_EXPERTS // COMBINE_GROUP, tm // ROW_GROUPS,
                                    COMBINE_GROUP * min(SLOT_WINDOW, cap)), BF16),
                        pltpu.VMEM((N_EXPERTS * min(SLOT_WINDOW, cap), d), BF16)],
    )
    return pl.pallas_call(
        functools.partial(_combine_kernel, cap=cap, n_off=n_off),
        grid_spec=grid_spec,
        out_shape=jax.ShapeDtypeStruct((b, t, d), F32),
        compiler_params=_params("parallel", "parallel"),
        name="combine",
    )(offs, slot_t, y, x1, mods, g)


def kernel(x, c, ctx, c_ctx, w_ada, b_ada, norm1_g, norm2_g, w_in, w_four, conv_w, conv_b, lru_lambda, lru_wa, lru_ba, lru_wi, lru_bi, w_lru, w_out, w_router, w_gate_e, w_up_e, w_down_e, final_g):
    b, t, d = x.shape
    assert d == D_MODEL and b == SUBLANES and t % TOKEN_BLOCK == 0 and ctx.shape[1] % SCAN_CHUNK == 0
    assert w_ada.shape[0] == 1, "single-layer problem: the context stream is only read by the recurrence"
    cap = CAPACITY_FACTOR * t // N_EXPERTS
    col_x = D_FOURIER + D_RNN
    col_g = col_x + D_RNN
    col_a = col_g + D_MODEL

    pos = _pos_table(t)
    ct, st = _dft_tables(t)

    cc = jnp.concatenate([c, c_ctx[None], jnp.zeros((SUBLANES - 1, d), F32)], axis=0)
    mods = _ada(cc, w_ada[0], b_ada[0]).reshape(cc.shape[0], N_MOD, d)
    mods = jnp.pad(mods, ((0, 0), (0, SUBLANES - N_MOD), (0, 0)))

    w = w_in[0].astype(BF16)
    g1 = norm1_g[0].reshape(1, d)
    zx, zg, za, zb, zf, wg_b, wu_b, wd_b = _proj(
        x, pos, mods, None, g1, w,
        ((D_FOURIER, D_RNN), (col_x, D_RNN), (col_g, D_MODEL), (col_a, D_MODEL), (0, D_FOURIER)),
        casts=(w_gate_e[0].reshape(N_EXPERTS * d, D_EXPERT), w_up_e[0].reshape(N_EXPERTS * d, D_EXPERT),
               w_down_e[0].reshape(N_EXPERTS * D_EXPERT, d)))
    (zx_ctx,) = _proj(ctx, None, mods, b, g1, w, ((D_FOURIER, D_RNN),))

    for dr in range(2):
        vecs = jnp.concatenate([lru_lambda[0, dr][None], 0.5 * lru_ba[0, dr][None], 0.5 * lru_bi[0, dr][None],
                                conv_b[0][None], conv_w[0]], axis=0)
        wg = (0.5 * jnp.concatenate([lru_wa[0, dr], lru_wi[0, dr]], axis=-1)).astype(BF16)
        if dr == 0:
            u_lat = _conv(zx, vecs)
            u_ctx = _conv(zx_ctx, vecs)
        _, h_ctx = _scan(u_ctx, vecs, wg, jnp.zeros((b, D_RNN), F32), reverse=bool(dr))
        if dr == 0:
            h_fwd, _ = _scan(u_lat, vecs, wg, h_ctx, reverse=False)
        else:
            gr, _ = _scan(u_lat, vecs, wg, h_ctx, reverse=True, hf=h_fwd, zg=zg)

    f = _dft_apply(ct, st, zf)

    x1, h2, scores, score_rows = _merge(
        f, gr, za, zb, x, pos, mods, norm2_g[0].reshape(1, d),
        w_four[0].astype(BF16), w_lru[0].astype(BF16), w_out[0].astype(BF16), w_router[0].T.astype(BF16))

    slot, offs = _select(scores.reshape(b * N_EXPERTS, t), cap)
    offs = offs[:, :t // OFF_CHUNK + 1].reshape(-1)
    y = _experts(offs, slot, score_rows, h2, wg_b.reshape(N_EXPERTS, d, D_EXPERT),
                 wu_b.reshape(N_EXPERTS, d, D_EXPERT), wd_b.reshape(N_EXPERTS, D_EXPERT, d), cap)
    slot_t = jnp.swapaxes(slot.reshape(b, N_EXPERTS, t), 1, 2)
    return _combine(offs, slot_t, y.reshape(b, N_EXPERTS * cap, d), x1, mods, final_g.reshape(1, d), cap)
```

```python
import functools
import math

import numpy as np
import jax
import jax.numpy as jnp
from jax import lax
from jax.experimental import pallas as pl
from jax.experimental.pallas import tpu as pltpu

F32 = jnp.float32
BF16 = jnp.bfloat16

D_MODEL = 1024
GRID_W = 64
N_FOURIER_GROUPS = 4
FOURIER_GROUP = 128
D_FOURIER = N_FOURIER_GROUPS * FOURIER_GROUP
D_RNN = D_MODEL
N_RNN_HEADS = 8
RNN_HEAD = D_RNN // N_RNN_HEADS
CONV_W = 4
CONV_LEFT = 2
LRU_C = 8.0
LOG2E = math.log2(math.e)
N_EXPERTS = 16
CAPACITY_FACTOR = 2
D_EXPERT = 1536
N_MOD = 6
EPS = 1e-6
POS_MAX_PERIOD = 10000.0

LANES = 128
SUBLANES = 8
BF16_ROWS = 16
VMEM_LIMIT = 58 * 1024 * 1024

TOKEN_BLOCK = 512
ROW_GROUPS = 2
MERGE_BLOCK = 1024
SCAN_CHUNK = 128


def _params(*sem):
    return pltpu.CompilerParams(dimension_semantics=sem, vmem_limit_bytes=VMEM_LIMIT)


def _sigmoid(x):
    return 0.5 * jnp.tanh(0.5 * x) + 0.5


def _rmsnorm(x, g):
    return x * lax.rsqrt(jnp.mean(x * x, axis=-1, keepdims=True) + EPS) * g


def _dot(a, b):
    return jnp.dot(a, b, preferred_element_type=F32)


def _pos_kernel(o_ref):
    rows = o_ref.shape[0]
    q = D_MODEL // 4
    k = lax.broadcasted_iota(jnp.int32, (1, q), 1).astype(F32)
    freqs = jnp.exp(-math.log(POS_MAX_PERIOD) * k / q)
    ang_r = lax.broadcasted_iota(jnp.int32, (rows, q), 0).astype(F32) * freqs
    ang_c = lax.broadcasted_iota(jnp.int32, (GRID_W, q), 0).astype(F32) * freqs
    shape = (rows, GRID_W, q)
    o_ref[:, :, 0 * q:1 * q] = jnp.broadcast_to(jnp.sin(ang_r)[:, None, :], shape)
    o_ref[:, :, 1 * q:2 * q] = jnp.broadcast_to(jnp.cos(ang_r)[:, None, :], shape)
    o_ref[:, :, 2 * q:3 * q] = jnp.broadcast_to(jnp.sin(ang_c)[None, :, :], shape)
    o_ref[:, :, 3 * q:4 * q] = jnp.broadcast_to(jnp.cos(ang_c)[None, :, :], shape)


def _pos_table(n_tokens):
    rows = n_tokens // GRID_W
    out = pl.pallas_call(
        _pos_kernel,
        out_shape=jax.ShapeDtypeStruct((rows, GRID_W, D_MODEL), F32),
        name="pos_table",
    )()
    return out.reshape(n_tokens, D_MODEL)


RADIX = 8


def _dft_kernel(c_ref, s_ref, *, n):
    ns = c_ref.shape[1]
    k2 = lax.broadcasted_iota(jnp.int32, (ns, ns), 0)
    t2 = lax.broadcasted_iota(jnp.int32, (ns, ns), 1)
    common = ((k2 * t2) & (ns - 1)).astype(F32) * (2.0 * math.pi / ns)
    cos_c = jnp.cos(common)
    sin_c = jnp.sin(common)
    scale = 1.0 / math.sqrt(n)
    t_row = lax.broadcasted_iota(jnp.int32, (1, ns), 1)
    for k1 in range(RADIX):
        slab = (k1 * t_row).astype(F32) * (2.0 * math.pi / n)
        cos_s = jnp.cos(slab) * scale
        sin_s = jnp.sin(slab) * scale
        c_ref[k1] = (cos_c * cos_s - sin_c * sin_s).astype(BF16)
        s_ref[k1] = (sin_c * cos_s + cos_c * sin_s).astype(BF16)


def _dft_tables(n):
    assert n & (n - 1) == 0 and n % (RADIX * BF16_ROWS) == 0
    ns = n // RADIX
    return pl.pallas_call(
        functools.partial(_dft_kernel, n=n),
        out_shape=[jax.ShapeDtypeStruct((RADIX, ns, ns), BF16)] * 2,
        compiler_params=_params(),
        name="dft_tables",
    )()


def _ada_kernel(c_ref, w_ref, b_ref, o_ref):
    c = c_ref[...]
    o_ref[...] = jnp.dot(c * _sigmoid(c), w_ref[...], preferred_element_type=F32,
                         precision=lax.Precision.HIGHEST) + b_ref[...]


def _ada(cc, w_ada, b_ada):
    n = w_ada.shape[1]
    tn = D_MODEL
    return pl.pallas_call(
        _ada_kernel,
        grid=(n // tn,),
        in_specs=[pl.BlockSpec(cc.shape, lambda j: (0, 0)),
                  pl.BlockSpec((D_MODEL, tn), lambda j: (0, j)),
                  pl.BlockSpec((1, tn), lambda j: (0, j))],
        out_specs=pl.BlockSpec((cc.shape[0], tn), lambda j: (0, j)),
        out_shape=jax.ShapeDtypeStruct((cc.shape[0], n), F32),
        compiler_params=_params("parallel"),
        name="ada",
    )(cc, w_ada, b_ada.reshape(1, n))


def _proj_kernel(*refs, cols, has_pos, n_cast):
    refs = list(refs)
    if n_cast:
        outs_cast = refs[-n_cast:]
        del refs[-n_cast:]
        n_in = len(refs) - len(cols)
        for w32_ref, w16_ref in zip(refs[n_in - n_cast:n_in], outs_cast):
            w16_ref[...] = w32_ref[...].astype(BF16)
        del refs[n_in - n_cast:n_in]
    if has_pos:
        x_ref, pos_ref, mod_ref, g_ref, w_ref = refs[:5]
        outs = refs[5:]
    else:
        x_ref, mod_ref, g_ref, w_ref = refs[:4]
        outs = refs[4:]
    tg = x_ref.shape[0] // ROW_GROUPS
    for grp in range(ROW_GROUPS):
        rs = slice(grp * tg, (grp + 1) * tg)
        x = x_ref[rs, :] + pos_ref[rs, :] if has_pos else x_ref[rs, :]
        h = _rmsnorm(x, g_ref[...]) * (1.0 + mod_ref[1:2, :]) + mod_ref[0:1, :]
        hb = h.astype(BF16)
        for o_ref, (c0, wd) in zip(outs, cols):
            o_ref[rs, :] = _dot(hb, w_ref[:, c0:c0 + wd]).astype(BF16)


def _proj(x, pos, mods, mod_row, g, w, cols, casts=()):
    b, t, d = x.shape
    tm = min(TOKEN_BLOCK, t)
    has_pos = pos is not None
    in_specs = [pl.BlockSpec((None, tm, d), lambda i, bb: (bb, i, 0))]
    args = [x]
    if has_pos:
        in_specs.append(pl.BlockSpec((tm, d), lambda i, bb: (i, 0)))
        args.append(pos)
    if mod_row is None:
        mod_map = lambda i, bb: (bb, 0, 0)
    else:
        mod_map = lambda i, bb: (mod_row, 0, 0)
    in_specs += [pl.BlockSpec((None, SUBLANES, d), mod_map),
                 pl.BlockSpec((1, d), lambda i, bb: (0, 0)),
                 pl.BlockSpec(w.shape, lambda i, bb: (0, 0), pipeline_mode=pl.Buffered(1))]
    args += [mods, g, w]
    out_specs = [pl.BlockSpec((None, tm, wd), lambda i, bb: (bb, i, 0)) for _, wd in cols]
    out_shape = [jax.ShapeDtypeStruct((b, t, wd), BF16) for _, wd in cols]
    for w32 in casts:
        spec, shape = _cast_specs(w32, (t // tm) * b, lambda i, bb: i * b + bb)
        in_specs.append(spec)
        args.append(w32)
        out_specs.append(spec)
        out_shape.append(shape)
    return pl.pallas_call(
        functools.partial(_proj_kernel, cols=cols, has_pos=has_pos, n_cast=len(casts)),
        grid=(t // tm, b),
        in_specs=in_specs,
        out_specs=out_specs,
        out_shape=out_shape,
        compiler_params=_params("parallel", "parallel"),
        name="proj",
    )(*args)


def _gelu(x):
    return 0.5 * x * (1.0 + jnp.tanh(math.sqrt(2.0 / math.pi) * (x + 0.044715 * x * x * x)))


def _conv_kernel(x_ref, xp_ref, xn_ref, vec_ref, o_ref):
    tm = x_ref.shape[0]
    edge = BF16_ROWS
    i = pl.program_id(0)
    first = i == 0
    last = i == pl.num_programs(0) - 1
    w = [vec_ref[4 + kk:5 + kk, :] for kk in range(CONV_W)]
    bias = vec_ref[3:4, :]

    def taps(xm2, xm1, x0, xp1):
        return bias + w[0] * xm2 + w[1] * xm1 + w[2] * x0 + w[3] * xp1

    x = x_ref[...].astype(F32)
    o_ref[...] = taps(pltpu.roll(x, 2, 0), pltpu.roll(x, 1, 0), x, pltpu.roll(x, tm - 1, 0)).astype(BF16)

    row = lax.broadcasted_iota(jnp.int32, (edge, x.shape[1]), 0)

    def shifted(before, cur, after, s):
        if s < 0:
            return jnp.where(row < -s, pltpu.roll(before, -s, 0), pltpu.roll(cur, -s, 0))
        return jnp.where(row < edge - s, pltpu.roll(cur, edge - s, 0), pltpu.roll(after, edge - s, 0))

    prev = jnp.where(first, 0.0, xp_ref[...].astype(F32))
    nxt = jnp.where(last, 0.0, xn_ref[...].astype(F32))
    head, head2 = x[0:edge], x[edge:2 * edge]
    tail, tail2 = x[tm - edge:tm], x[tm - 2 * edge:tm - edge]
    o_ref[0:edge, :] = taps(shifted(prev, head, head2, -2), shifted(prev, head, head2, -1), head,
                            shifted(prev, head, head2, 1)).astype(BF16)
    o_ref[tm - edge:tm, :] = taps(shifted(tail2, tail, nxt, -2), shifted(tail2, tail, nxt, -1), tail,
                                  shifted(tail2, tail, nxt, 1)).astype(BF16)


def _conv(zx, vecs):
    b, t, c = zx.shape
    tm = min(TOKEN_BLOCK, t)
    assert tm >= 2 * BF16_ROWS
    hpb = tm // BF16_ROWS
    n_halo = t // BF16_ROWS
    return pl.pallas_call(
        _conv_kernel,
        grid=(t // tm, b),
        in_specs=[pl.BlockSpec((None, tm, c), lambda i, bb: (bb, i, 0)),
                  pl.BlockSpec((None, BF16_ROWS, c), lambda i, bb: (bb, jnp.maximum(i * hpb - 1, 0), 0)),
                  pl.BlockSpec((None, BF16_ROWS, c), lambda i, bb: (bb, jnp.minimum((i + 1) * hpb, n_halo - 1), 0)),
                  pl.BlockSpec(vecs.shape, lambda i, bb: (0, 0))],
        out_specs=pl.BlockSpec((None, tm, c), lambda i, bb: (bb, i, 0)),
        out_shape=jax.ShapeDtypeStruct((b, t, c), BF16),
        compiler_params=_params("parallel", "parallel"),
        name="conv",
    )(zx, zx, zx, vecs)


def _scan_kernel(*refs, reverse, fuse):
    if fuse:
        u_ref, vec_ref, wg_ref, h0_ref, hf_ref, zg_ref, o_ref, hT_ref, a_s, b_s, h_s, hc_s = refs
    else:
        u_ref, vec_ref, wg_ref, h0_ref, o_ref, hT_ref, a_s, b_s, h_s, hc_s = refs
    nb, tc, c = u_ref.shape

    @pl.when(pl.program_id(0) == 0)
    def _():
        for hh in range(N_RNN_HEADS):
            hc_s[hh] = h0_ref[:, hh * RNN_HEAD:(hh + 1) * RNN_HEAD]

    lam = vec_ref[0:1, :]
    half_b_a = vec_ref[1:2, :]
    half_b_i = vec_ref[2:3, :]
    k_row = (-0.5 * LRU_C * LOG2E) * (jnp.maximum(-lam, 0.0) + jnp.log1p(jnp.exp(-jnp.abs(lam))))

    for bb in range(nb):
        ub = u_ref[bb]
        u = ub.astype(F32)
        rows = pl.ds(bb, tc, stride=nb)
        for hh in range(N_RNN_HEADS):
            sl = slice(hh * RNN_HEAD, (hh + 1) * RNN_HEAD)
            gates = _dot(ub[:, sl], wg_ref[hh])
            t_r = jnp.tanh(gates[:, :RNN_HEAD] + half_b_a[:, sl])
            t_i = jnp.tanh(gates[:, RNN_HEAD:] + half_b_i[:, sl])
            a = jnp.exp2(k_row[:, sl] * t_r + k_row[:, sl])
            a_s[hh, rows, :] = a
            scale = jnp.exp2((0.5 * LOG2E) * jnp.log(1.0 - a * a)) * u[:, sl]
            b_s[hh, rows, :] = scale * (0.5 * t_i + 0.5)

    def body(j, hs):
        t = (tc - 1 - j) if reverse else j
        idx = pl.ds(pl.multiple_of(t * nb, nb), nb)
        new = []
        for hh in range(N_RNN_HEADS):
            h = a_s[hh, idx, :] * hs[hh] + b_s[hh, idx, :]
            h_s[hh, idx, :] = h
            new.append(h)
        return tuple(new)

    hs = lax.fori_loop(0, tc, body, tuple(hc_s[hh] for hh in range(N_RNN_HEADS)), unroll=8)
    for hh in range(N_RNN_HEADS):
        hc_s[hh] = hs[hh]
        hT_ref[:, hh * RNN_HEAD:(hh + 1) * RNN_HEAD] = hs[hh]

    for bb in range(nb):
        for hh in range(N_RNN_HEADS):
            sl = slice(hh * RNN_HEAD, (hh + 1) * RNN_HEAD)
            hb = h_s[hh, pl.ds(bb, tc, stride=nb), :]
            if fuse:
                hb = _gelu(zg_ref[bb, :, sl].astype(F32)) * (hf_ref[bb, :, sl].astype(F32) + hb)
            o_ref[bb, :, sl] = hb.astype(BF16)


def _cast_specs(w, n_steps, step_index):
    rows, cols = w.shape
    assert rows % (n_steps * BF16_ROWS) == 0
    spec = pl.BlockSpec((rows // n_steps, cols), lambda *idx: (step_index(*idx), 0))
    return spec, jax.ShapeDtypeStruct((rows, cols), BF16)


def _scan(u, vecs, wg, h0, *, reverse, hf=None, zg=None):
    nb, t, c = u.shape
    tc = SCAN_CHUNK
    n_chunks = t // tc
    fuse = hf is not None

    def cidx(i):
        return (n_chunks - 1 - i) if reverse else i

    main_spec = pl.BlockSpec((nb, tc, c), lambda i: (0, cidx(i), 0))
    in_specs = [
        main_spec,
        pl.BlockSpec(vecs.shape, lambda i: (0, 0)),
        pl.BlockSpec(wg.shape, lambda i: (0, 0, 0)),
        pl.BlockSpec(h0.shape, lambda i: (0, 0)),
    ]
    args = [u, vecs, wg, h0]
    if fuse:
        in_specs += [main_spec, main_spec]
        args += [hf, zg]
    return pl.pallas_call(
        functools.partial(_scan_kernel, reverse=reverse, fuse=fuse),
        grid=(n_chunks,),
        in_specs=in_specs,
        out_specs=[main_spec, pl.BlockSpec((nb, c), lambda i: (0, 0))],
        out_shape=[jax.ShapeDtypeStruct((nb, t, c), BF16), jax.ShapeDtypeStruct((nb, c), F32)],
        scratch_shapes=[pltpu.VMEM((N_RNN_HEADS, nb * tc, RNN_HEAD), F32),
                        pltpu.VMEM((N_RNN_HEADS, nb * tc, RNN_HEAD), F32),
                        pltpu.VMEM((N_RNN_HEADS, nb * tc, RNN_HEAD), F32),
                        pltpu.VMEM((N_RNN_HEADS, nb, RNN_HEAD), F32)],
        compiler_params=_params("arbitrary"),
        name="scan_bwd" if reverse else "scan_fwd",
    )(*args)


DFT_ROWS = 64
DFT_COLS = 256


def _dftmm_kernel(zf_ref, cs_ref, c_ref, s_ref, o_ref, zc_s, zs_s, ar_s, ai_s, o_s):
    t, f = zf_ref.shape
    ns = t // RADIX
    rb = min(DFT_ROWS, ns)
    r2 = math.sqrt(0.5)

    cs_table = cs_ref[...].astype(BF16)
    for j in range(RADIX):
        rows = slice(j * ns, (j + 1) * ns)
        for g in range(N_FOURIER_GROUPS):
            sl = slice(g * FOURIER_GROUP, (g + 1) * FOURIER_GROUP)
            cs = _dot(zf_ref[rows, sl], cs_table)
            zc_s[rows, sl] = cs[:, :FOURIER_GROUP].astype(BF16)
            zs_s[rows, sl] = cs[:, FOURIER_GROUP:].astype(BF16)

    def dft4(x):
        s0 = (x[0][0] + x[2][0], x[0][1] + x[2][1])
        s1 = (x[0][0] - x[2][0], x[0][1] - x[2][1])
        s2 = (x[1][0] + x[3][0], x[1][1] + x[3][1])
        s3 = (x[1][0] - x[3][0], x[1][1] - x[3][1])
        return [(s0[0] + s2[0], s0[1] + s2[1]), (s1[0] + s3[1], s1[1] - s3[0]),
                (s0[0] - s2[0], s0[1] - s2[1]), (s1[0] - s3[1], s1[1] + s3[0])]

    def butterfly(i, carry):
        r0 = pl.multiple_of(i * rb, rb)
        w = [(zc_s[pl.ds(j * ns + r0, rb), :].astype(F32), -zs_s[pl.ds(j * ns + r0, rb), :].astype(F32))
             for j in range(RADIX)]
        u = [(w[j][0] + w[j + 4][0], w[j][1] + w[j + 4][1]) for j in range(4)]
        v = [(w[j][0] - w[j + 4][0], w[j][1] - w[j + 4][1]) for j in range(4)]
        v = [v[0],
             ((v[1][0] + v[1][1]) * r2, (v[1][1] - v[1][0]) * r2),
             (v[2][1], -v[2][0]),
             ((v[3][1] - v[3][0]) * r2, (-v[3][0] - v[3][1]) * r2)]
        even = dft4(u)
        odd = dft4(v)
        for m in range(4):
            for k1, a in ((2 * m, even[m]), (2 * m + 1, odd[m])):
                ar_s[k1, pl.ds(r0, rb), :] = a[0].astype(BF16)
                ai_s[k1, pl.ds(r0, rb), :] = a[1].astype(BF16)
        return carry

    lax.fori_loop(0, ns // rb, butterfly, 0)

    for c0 in range(0, f, DFT_COLS):
        for k1 in range(RADIX):
            res = (_dot(c_ref[k1], ar_s[k1, :, c0:c0 + DFT_COLS])
                   + _dot(s_ref[k1], ai_s[k1, :, c0:c0 + DFT_COLS]))
            for j in range(DFT_COLS // LANES):
                o_s[j, pl.ds(k1, ns, stride=RADIX), :] = res[:, j * LANES:(j + 1) * LANES]
        for j in range(DFT_COLS // LANES):
            o_ref[:, c0 + j * LANES:c0 + (j + 1) * LANES] = o_s[j].astype(BF16)


def _channel_dft_table():
    p = np.arange(FOURIER_GROUP)
    ang = 2.0 * np.pi * ((p[:, None] * p[None, :]) % FOURIER_GROUP) / FOURIER_GROUP
    cs = np.concatenate([np.cos(ang), np.sin(ang)], axis=1) / math.sqrt(FOURIER_GROUP)
    return jnp.asarray(cs, F32)


def _dft_apply(ct, st, zf):
    b, t, f = zf.shape
    ns = t // RADIX
    table = pl.BlockSpec((RADIX, ns, ns), lambda bb: (0, 0, 0), pipeline_mode=pl.Buffered(1))
    cs = _channel_dft_table()
    return pl.pallas_call(
        _dftmm_kernel,
        grid=(b,),
        in_specs=[pl.BlockSpec((None, t, f), lambda bb: (bb, 0, 0)),
                  pl.BlockSpec(cs.shape, lambda bb: (0, 0)),
                  table, table],
        out_specs=pl.BlockSpec((None, t, f), lambda bb: (bb, 0, 0)),
        out_shape=jax.ShapeDtypeStruct((b, t, f), BF16),
        scratch_shapes=[pltpu.VMEM((t, f), BF16),
                        pltpu.VMEM((t, f), BF16),
                        pltpu.VMEM((RADIX, ns, f), BF16),
                        pltpu.VMEM((RADIX, ns, f), BF16),
                        pltpu.VMEM((DFT_COLS // LANES, t, LANES), F32)],
        compiler_params=_params("parallel"),
        name="dft_apply",
    )(zf, cs, ct, st)


def _merge_kernel(f_ref, gr_ref, za_ref, zb_ref, x_ref, pos_ref, mod_ref, g2_ref,
                  wf_ref, wl_ref, wo_ref, wr_ref, x1_ref, h2_ref, sc_ref, scr_ref):
    y_four = _dot(f_ref[...], wf_ref[...])
    y_rnn = _dot(gr_ref[...], wl_ref[...])
    m = _sigmoid(za_ref[...].astype(F32)) * y_four + _sigmoid(zb_ref[...].astype(F32)) * y_rnn
    y = _dot(m.astype(BF16), wo_ref[...])
    x1 = (x_ref[...] + pos_ref[...]) + mod_ref[2:3, :] * y
    x1_ref[...] = x1
    h2 = (_rmsnorm(x1, g2_ref[...]) * (1.0 + mod_ref[4:5, :]) + mod_ref[3:4, :]).astype(BF16)
    h2_ref[...] = h2
    logits = lax.dot_general(wr_ref[...], h2, (((1,), (1,)), ((), ())), preferred_element_type=F32)
    e = jnp.exp(logits - jnp.max(logits, axis=0, keepdims=True))
    sc = e / jnp.sum(e, axis=0, keepdims=True)
    sc_ref[...] = sc
    hi = sc.astype(BF16).astype(F32)
    lo = sc - hi
    tm = sc.shape[1]
    row = lax.broadcasted_iota(jnp.int32, (BF16_ROWS, tm), 0)
    for ex in range(N_EXPERTS):
        tile = jnp.where(row == 0, hi[ex:ex + 1, :], jnp.where(row == 1, lo[ex:ex + 1, :], 0.0))
        scr_ref[ex] = tile.astype(BF16)


def _merge(f, gr, za, zb, x, pos, mods, g2, w_four, w_lru, w_out, w_rt):
    b, t, d = x.shape
    tm = min(MERGE_BLOCK, t)
    tok = lambda wd: pl.BlockSpec((None, tm, wd), lambda i, bb: (bb, i, 0))
    full = lambda a: pl.BlockSpec(a.shape, lambda i, bb: (0,) * a.ndim)
    return pl.pallas_call(
        _merge_kernel,
        grid=(t // tm, b),
        in_specs=[tok(D_FOURIER), tok(d), tok(d), tok(d), tok(d),
                  pl.BlockSpec((tm, d), lambda i, bb: (i, 0)),
                  pl.BlockSpec((None, SUBLANES, d), lambda i, bb: (bb, 0, 0)),
                  full(g2), full(w_four), full(w_lru), full(w_out), full(w_rt)],
        out_specs=[tok(d), tok(d), pl.BlockSpec((None, N_EXPERTS, tm), lambda i, bb: (bb, 0, i)),
                   pl.BlockSpec((None, N_EXPERTS, BF16_ROWS, tm), lambda i, bb: (bb, 0, 0, i))],
        out_shape=[jax.ShapeDtypeStruct((b, t, d), F32),
                   jax.ShapeDtypeStruct((b, t, d), BF16),
                   jax.ShapeDtypeStruct((b, N_EXPERTS, t), F32),
                   jax.ShapeDtypeStruct((b, N_EXPERTS, BF16_ROWS, t), BF16)],
        compiler_params=_params("parallel", "parallel"),
        name="merge",
    )(f, gr, za, zb, x, pos, mods, g2, w_four, w_lru, w_out, w_rt)


MAX_EXP = 127
EXP_STEPS = 7
VAL_STEPS = 56


OFF_CHUNK = 256


def _select_kernel(s_ref, slot_ref, offs_ref, *, cap):
    rows, t = s_ref.shape

    def count(mask):
        return jnp.sum(jnp.where(mask, 1.0, 0.0), axis=1, keepdims=True)

    def exp_step(_, carry):
        k_lo, k_hi = carry
        k_mid = jnp.floor(0.5 * (k_lo + k_hi))
        ok = count(s_ref[...] >= jnp.exp2(-k_mid)) >= cap
        return jnp.where(ok, k_lo, k_mid), jnp.where(ok, k_mid, k_hi)

    k_lo, k_hi = lax.fori_loop(
        0, EXP_STEPS, exp_step,
        (jnp.full((rows, 1), -1.0, F32), jnp.full((rows, 1), float(MAX_EXP), F32)))
    lo0 = jnp.where(k_hi >= MAX_EXP, 0.0, jnp.exp2(-k_hi))
    hi0 = jnp.exp2(-k_lo)

    def val_step(_, carry):
        lo, hi = carry
        mid = 0.5 * (lo + hi)
        ok = count(s_ref[...] >= mid) >= cap
        return jnp.where(ok, mid, lo), jnp.where(ok, hi, mid)

    lo, hi = lax.fori_loop(0, VAL_STEPS, val_step, (lo0, hi0))
    s = s_ref[...]
    gt = s >= hi
    eq = (s >= lo) & (s < hi)
    need = cap - count(gt)

    p = lax.broadcasted_iota(jnp.int32, (LANES, LANES), 0)
    q = lax.broadcasted_iota(jnp.int32, (LANES, LANES), 1)
    tri = (p < q).astype(BF16)
    run_gt = jnp.zeros((rows, 1), F32)
    run_eq = jnp.zeros((rows, 1), F32)
    off_lane = lax.broadcasted_iota(jnp.int32, (rows, LANES), 1)
    offs = jnp.zeros((rows, LANES), F32)
    for j in range(t // LANES):
        if (j * LANES) % OFF_CHUNK == 0:
            offs = jnp.where(off_lane == (j * LANES) // OFF_CHUNK, run_gt + jnp.minimum(run_eq, need), offs)
        sl = slice(j * LANES, (j + 1) * LANES)
        g = gt[:, sl]
        q_ = eq[:, sl]
        pg = _dot(g.astype(BF16), tri) + run_gt
        pe = _dot(q_.astype(BF16), tri) + run_eq
        sel = g | (q_ & (pe < need))
        pos = pg + jnp.minimum(pe, need)
        slot_ref[:, sl] = jnp.where(sel, pos, -1.0).astype(jnp.int32)
        run_gt = run_gt + count(g)
        run_eq = run_eq + count(q_)
    offs = jnp.where(off_lane == t // OFF_CHUNK, run_gt + jnp.minimum(run_eq, need), offs)
    offs_ref[...] = offs.astype(jnp.int32)


def _select(scores, cap):
    rows, t = scores.shape
    assert t % OFF_CHUNK == 0 and t // OFF_CHUNK < LANES
    return pl.pallas_call(
        functools.partial(_select_kernel, cap=cap),
        out_shape=[jax.ShapeDtypeStruct((rows, t), jnp.int32),
                   jax.ShapeDtypeStruct((rows, LANES), jnp.int32)],
        compiler_params=_params(),
        name="select",
    )(scores)


GATHER_BLOCK = 512
GATHER_WINDOW = 128


def _expert_kernel(offs_ref, slot_ref, sc_ref, h_ref, wg_ref, wu_ref, wd_ref, o_ref,
                   xg_s, gs_s, rows_s, gates_s, *, cap, n_off):
    t = h_ref.shape[0]
    tk = min(GATHER_BLOCK, t)
    win = min(GATHER_WINDOW, cap)
    per_block = tk // OFF_CHUNK
    slot_iota = lax.broadcasted_iota(jnp.int32, (win, tk), 0)
    pad = jnp.zeros((SUBLANES, D_MODEL), F32)
    xg_s[0:SUBLANES, :] = pad
    gs_s[...] = jnp.zeros_like(gs_s)
    row0 = (pl.program_id(0) * N_EXPERTS + pl.program_id(1)) * n_off

    def gather(sl, lo):
        onehot = (slot_iota + lo == slot_ref[:, sl]).astype(BF16)
        rows = _dot(onehot, h_ref[sl, :])
        gates = lax.dot_general(onehot, sc_ref[:, sl], (((1,), (1,)), ((), ())), preferred_element_type=F32)
        return rows, gates

    n_blocks = t // tk
    starts = [pl.multiple_of((offs_ref[row0 + j * per_block] // SUBLANES) * SUBLANES, SUBLANES)
              for j in range(n_blocks)]
    for j in range(n_blocks):
        rows_s[j], gates_s[j] = gather(slice(j * tk, (j + 1) * tk), starts[j])
    for j in range(n_blocks):
        sl = slice(j * tk, (j + 1) * tk)
        st = starts[j]
        last = offs_ref[row0 + (j + 1) * per_block]
        xg_s[pl.ds(st, SUBLANES), :] += rows_s[j, 0:SUBLANES, :]
        xg_s[pl.ds(st + SUBLANES, win - SUBLANES), :] = rows_s[j, SUBLANES:win, :]
        xg_s[pl.ds(st + win, SUBLANES), :] = pad
        gs_s[pl.ds(st, win), :] += gates_s[j]

        def more(c, carry, sl=sl, st=st):
            lo = pl.multiple_of(st + c * win, SUBLANES)
            rows, gates = gather(sl, lo)
            xg_s[pl.ds(lo, win), :] = rows
            xg_s[pl.ds(lo + win, SUBLANES), :] = pad
            gs_s[pl.ds(lo, win), :] += gates
            return carry

        lax.fori_loop(1, (last - st + win - 1) // win, more, 0)
    xb = xg_s[0:cap, :].astype(BF16)
    gate = _dot(xb, wg_ref[...])
    hid = (gate * _sigmoid(gate)) * _dot(xb, wu_ref[...])
    y = _dot(hid.astype(BF16), wd_ref[...])
    o_ref[...] = (y * (gs_s[0:cap, 0:1] + gs_s[0:cap, 1:2])).astype(BF16)


def _experts(offs, slot, score_rows, h2, wg, wu, wd, cap):
    b, t, d = h2.shape
    n_off = offs.shape[0] // (b * N_EXPERTS)
    win = min(GATHER_WINDOW, cap)
    grid_spec = pltpu.PrefetchScalarGridSpec(
        num_scalar_prefetch=1,
        grid=(b, N_EXPERTS),
        in_specs=[pl.BlockSpec((None, 1, t), lambda bb, e, o: (bb * N_EXPERTS + e, 0, 0)),
                  pl.BlockSpec((None, None, BF16_ROWS, t), lambda bb, e, o: (bb, e, 0, 0)),
                  pl.BlockSpec((None, t, d), lambda bb, e, o: (bb, 0, 0)),
                  pl.BlockSpec((None, d, D_EXPERT), lambda bb, e, o: (e, 0, 0)),
                  pl.BlockSpec((None, d, D_EXPERT), lambda bb, e, o: (e, 0, 0)),
                  pl.BlockSpec((None, D_EXPERT, d), lambda bb, e, o: (e, 0, 0))],
        out_specs=pl.BlockSpec((None, cap, d), lambda bb, e, o: (bb * N_EXPERTS + e, 0, 0)),
        scratch_shapes=[pltpu.VMEM((cap + win + SUBLANES, d), F32), pltpu.VMEM((cap + win, BF16_ROWS), F32),
                        pltpu.VMEM((t // min(GATHER_BLOCK, t), win, d), F32),
                        pltpu.VMEM((t // min(GATHER_BLOCK, t), win, BF16_ROWS), F32)],
    )
    return pl.pallas_call(
        functools.partial(_expert_kernel, cap=cap, n_off=n_off),
        grid_spec=grid_spec,
        out_shape=jax.ShapeDtypeStruct((b * N_EXPERTS, cap, d), BF16),
        compiler_params=_params("parallel", "parallel"),
        name="experts",
    )(offs, slot.reshape(b * N_EXPERTS, 1, t), score_rows, h2, wg, wu, wd)


COMBINE_BLOCK = 1024
SLOT_WINDOW = 256


COMBINE_GROUP = 8


def _combine_kernel(offs_ref, slot_ref, y_ref, x1_ref, mod_ref, g_ref, o_ref, acc_s, oh_s, yw_s, *, cap, n_off):
    tm = x1_ref.shape[0]
    kc = min(SLOT_WINDOW, cap)
    per_block = tm // OFF_CHUNK
    tg = tm // ROW_GROUPS
    group_k = COMBINE_GROUP * kc

    def place(e, start0, c, rows):
        lo = start0 + c * kc
        st = pl.multiple_of(jnp.minimum(lo, cap - kc), BF16_ROWS)
        col = slot_ref[rows, e:e + 1]
        onehot = col == st + lax.broadcasted_iota(jnp.int32, (col.shape[0], kc), 1)
        if not isinstance(c, int) or c > 0:
            onehot = onehot & (col >= lo)
        return onehot.astype(BF16), st

    starts, extras = [], []
    for e in range(N_EXPERTS):
        o0 = (pl.program_id(0) * N_EXPERTS + e) * n_off + pl.program_id(1) * per_block
        start0 = (offs_ref[o0] // BF16_ROWS) * BF16_ROWS
        starts.append(start0)
        extras.append(jnp.maximum((offs_ref[o0 + per_block] - start0 + kc - 1) // kc - 1, 0))

    for e in range(N_EXPERTS):
        st = pl.multiple_of(jnp.minimum(starts[e], cap - kc), BF16_ROWS)
        yw_s[e * kc:(e + 1) * kc, :] = y_ref[pl.ds(e * cap + st, kc), :]

    def finish(rows, total):
        o_ref[rows, :] = _rmsnorm(x1_ref[rows, :] + mod_ref[5:6, :] * total, g_ref[...])

    accs = []
    for grp in range(ROW_GROUPS):
        rows = slice(grp * tg, (grp + 1) * tg)
        acc = None
        for gi in range(N_EXPERTS // COMBINE_GROUP):
            for i in range(COMBINE_GROUP):
                onehot, _ = place(gi * COMBINE_GROUP + i, starts[gi * COMBINE_GROUP + i], 0, rows)
                oh_s[gi, :, i * kc:(i + 1) * kc] = onehot
            part = _dot(oh_s[gi], yw_s[gi * group_k:(gi + 1) * group_k, :])
            acc = part if acc is None else acc + part
        finish(rows, acc)
        accs.append(acc)

    @pl.when(sum(extras) > 0)
    def _():
        every = slice(0, tm)
        for grp in range(ROW_GROUPS):
            acc_s[grp * tg:(grp + 1) * tg, :] = accs[grp]
        for e in range(N_EXPERTS):
            def more(c, carry, e=e):
                onehot, st = place(e, starts[e], c, every)
                acc_s[...] += _dot(onehot, y_ref[pl.ds(e * cap + st, kc), :])
                return carry
            lax.fori_loop(1, extras[e] + 1, more, 0)
        finish(every, acc_s[...])


def _combine(offs, slot_t, y, x1, mods, g, cap):
    b, t, d = x1.shape
    tm = min(COMBINE_BLOCK, t)
    n_off = offs.shape[0] // (b * N_EXPERTS)
    grid_spec = pltpu.PrefetchScalarGridSpec(
        num_scalar_prefetch=1,
        grid=(b, t // tm),
        in_specs=[pl.BlockSpec((None, tm, N_EXPERTS), lambda bb, i, o: (bb, i, 0)),
                  pl.BlockSpec((None, N_EXPERTS * cap, d), lambda bb, i, o: (bb, 0, 0),
                               pipeline_mode=pl.Buffered(1)),
                  pl.BlockSpec((None, tm, d), lambda bb, i, o: (bb, i, 0)),
                  pl.BlockSpec((None, SUBLANES, d), lambda bb, i, o: (bb, 0, 0)),
                  pl.BlockSpec((1, d), lambda bb, i, o: (0, 0))],
        out_specs=pl.BlockSpec((None, tm, d), lambda bb, i, o: (bb, i, 0)),
        scratch_shapes=[pltpu.VMEM((tm, d), F32),
                        pltpu.VMEM((N_EXPERTS // COMBINE_GROUP, tm // ROW_GROUPS,
                                    COMBINE_GROUP * min(SLOT_WINDOW, cap)), BF16),
                        pltpu.VMEM((N_EXPERTS * min(SLOT_WINDOW, cap), d), BF16)],
    )
    return pl.pallas_call(
        functools.partial(_combine_kernel, cap=cap, n_off=n_off),
        grid_spec=grid_spec,
        out_shape=jax.ShapeDtypeStruct((b, t, d), F32),
        compiler_params=_params("parallel", "parallel"),
        name="combine",
    )(offs, slot_t, y, x1, mods, g)


def kernel(x, c, ctx, c_ctx, w_ada, b_ada, norm1_g, norm2_g, w_in, w_four, conv_w, conv_b, lru_lambda, lru_wa, lru_ba, lru_wi, lru_bi, w_lru, w_out, w_router, w_gate_e, w_up_e, w_down_e, final_g):
    b, t, d = x.shape
    assert d == D_MODEL and b == SUBLANES and t % TOKEN_BLOCK == 0 and ctx.shape[1] % SCAN_CHUNK == 0
    assert w_ada.shape[0] == 1, "single-layer problem: the context stream is only read by the recurrence"
    cap = CAPACITY_FACTOR * t // N_EXPERTS
    col_x = D_FOURIER + D_RNN
    col_g = col_x + D_RNN
    col_a = col_g + D_MODEL

    pos = _pos_table(t)
    ct, st = _dft_tables(t)

    cc = jnp.concatenate([c, c_ctx[None], jnp.zeros((SUBLANES - 1, d), F32)], axis=0)
    mods = _ada(cc, w_ada[0], b_ada[0]).reshape(cc.shape[0], N_MOD, d)
    mods = jnp.pad(mods, ((0, 0), (0, SUBLANES - N_MOD), (0, 0)))

    w = w_in[0].astype(BF16)
    g1 = norm1_g[0].reshape(1, d)
    zx, zg, za, zb, zf, wg_b, wu_b, wd_b = _proj(
        x, pos, mods, None, g1, w,
        ((D_FOURIER, D_RNN), (col_x, D_RNN), (col_g, D_MODEL), (col_a, D_MODEL), (0, D_FOURIER)),
        casts=(w_gate_e[0].reshape(N_EXPERTS * d, D_EXPERT), w_up_e[0].reshape(N_EXPERTS * d, D_EXPERT),
               w_down_e[0].reshape(N_EXPERTS * D_EXPERT, d)))
    (zx_ctx,) = _proj(ctx, None, mods, b, g1, w, ((D_FOURIER, D_RNN),))

    for dr in range(2):
        vecs = jnp.concatenate([lru_lambda[0, dr][None], 0.5 * lru_ba[0, dr][None], 0.5 * lru_bi[0, dr][None],
                                conv_b[0][None], conv_w[0]], axis=0)
        wg = (0.5 * jnp.concatenate([lru_wa[0, dr], lru_wi[0, dr]], axis=-1)).astype(BF16)
        if dr == 0:
            u_lat = _conv(zx, vecs)
            u_ctx = _conv(zx_ctx, vecs)
        _, h_ctx = _scan(u_ctx, vecs, wg, jnp.zeros((b, D_RNN), F32), reverse=bool(dr))
        if dr == 0:
            h_fwd, _ = _scan(u_lat, vecs, wg, h_ctx, reverse=False)
        else:
            gr, _ = _scan(u_lat, vecs, wg, h_ctx, reverse=True, hf=h_fwd, zg=zg)

    f = _dft_apply(ct, st, zf)

    x1, h2, scores, score_rows = _merge(
        f, gr, za, zb, x, pos, mods, norm2_g[0].reshape(1, d),
        w_four[0].astype(BF16), w_lru[0].astype(BF16), w_out[0].astype(BF16), w_router[0].T.astype(BF16))

    slot, offs = _select(scores.reshape(b * N_EXPERTS, t), cap)
    offs = offs[:, :t // OFF_CHUNK + 1].reshape(-1)
    y = _experts(offs, slot, score_rows, h2, wg_b.reshape(N_EXPERTS, d, D_EXPERT),
                 wu_b.reshape(N_EXPERTS, d, D_EXPERT), wd_b.reshape(N_EXPERTS, D_EXPERT, d), cap)
    slot_t = jnp.swapaxes(slot.reshape(b, N_EXPERTS, t), 1, 2)
    return _combine(offs, slot_t, y.reshape(b, N_EXPERTS * cap, d), x1, mods, final_g.reshape(1, d), cap)
```
